```python
import math
import jax
import jax.numpy as jnp
from jax import lax
import numpy as np

D_MODEL = 1024
BATCH = 8
SEQ = 4096
DEPTH = 4

GRID_W = 64
CTX_LEN = 256
N_MIXERS = 4
ALPHA = (2 * DEPTH) ** 0.25
BETA = (8 * DEPTH) ** -0.25
LN_EPS = 1e-5
SHORT_CONV = 3
MOD_CHUNKS = 6

HY_ORDER = 2
HY_BANDS = 8
HY_EMB = 1 + 2 * HY_BANDS
HY_FILTER_W = 64
HY_TARGET = 1e-2
HY_FAST = 0.3
HY_SLOW = 1.5

RW_HEAD = 64
RW_HEADS = D_MODEL // RW_HEAD
RW_DECAY_LORA = 64
RW_AAA_LORA = 64
RW_GATE_LORA = 160
RW_GN_EPS = 64e-5

SSD_INNER = 2 * D_MODEL
SSD_HEAD = 64
SSD_HEADS = SSD_INNER // SSD_HEAD
SSD_STATE = 128
SSD_GROUPS = 4
SSD_CHUNK = 64

GDN_HEAD = 128
GDN_QK_HEADS = D_MODEL // GDN_HEAD
GDN_V_HEADS = 2 * GDN_QK_HEADS
GDN_CHUNK = 64

FFN_DENSE = 2816
N_EXPERTS = 8
TOP_K = 2
FFN_EXPERT = 3584

F32 = jnp.float32

kernel_name = 'hybrid_hyena_rwkv7_ssd_gdn_moe_trunk'


def layer_norm(x, g, b):
    xf = x.astype(F32)
    xc = xf - jnp.mean(xf, -1, keepdims=True)
    var = jnp.mean(xc * xc, -1, keepdims=True)
    return (xc * lax.rsqrt(var + LN_EPS) * g.astype(F32) + b.astype(F32)).astype(x.dtype)


def rms_norm(x, g, eps=1e-6):
    xf = x.astype(F32)
    return (xf * lax.rsqrt(jnp.mean(xf * xf, -1, keepdims=True) + eps) * g.astype(F32)).astype(x.dtype)


def l2_normalize(x, eps=1e-6):
    xf = x.astype(F32)
    return xf * lax.rsqrt(jnp.sum(xf * xf, -1, keepdims=True) + eps)


def dwconv_centred(x, w, b=None):
    pad = w.shape[0] // 2
    y = lax.conv_general_dilated(x, w[:, None, :].astype(x.dtype), window_strides=(1,),
                                 padding=[(pad, pad)], dimension_numbers=('NWC', 'WIO', 'NWC'),
                                 feature_group_count=x.shape[-1])
    return y if b is None else y + b.astype(x.dtype)


def flip_tokens(t):
    return t[:, ::-1]


def keep_tokens(t):
    return t


def raster_to_column(x):
    B, L, C = x.shape
    rows = L // GRID_W
    return x.reshape(B, rows, GRID_W, C).transpose(0, 2, 1, 3).reshape(B, L, C)


def column_to_raster(x):
    B, L, C = x.shape
    rows = L // GRID_W
    return x.reshape(B, GRID_W, rows, C).transpose(0, 2, 1, 3).reshape(B, L, C)


def modulation(c, w, b):
    return jnp.split(jax.nn.silu(c) @ w + b, MOD_CHUNKS, axis=-1)


def hyena_filters(L, f_w1, f_b1, f_w2, f_b2, f_freq, f_w3, decay):
    pos = jnp.arange(L, dtype=F32)
    t01 = pos / max(L - 1, 1)
    bands = jnp.linspace(1e-4, HY_BANDS - 1, HY_BANDS, dtype=F32)
    ang = (2.0 * math.pi / L) * pos[:, None] * bands[None, :]
    feats = jnp.concatenate([t01[:, None], jnp.cos(ang), -jnp.sin(ang)], axis=-1)
    h = jnp.sin(f_freq[0].astype(F32) * (feats @ f_w1.astype(F32) + f_b1.astype(F32)))
    h = jnp.sin(f_freq[1].astype(F32) * (h @ f_w2.astype(F32) + f_b2.astype(F32)))
    k = (h @ f_w3.astype(F32)).reshape(L, 2 * HY_ORDER, D_MODEL)
    return k * jnp.exp(-t01[:, None, None] * jnp.abs(decay.astype(F32)))


def bidir_long_conv(z, k_fwd, k_bwd, skip):
    L = z.shape[1]
    kern = jnp.concatenate([k_fwd, jnp.zeros_like(k_fwd[:1]), k_bwd[:0:-1]], axis=0)
    kf = jnp.fft.rfft(kern, axis=0)
    zf = jnp.fft.rfft(z.astype(F32), n=2 * L, axis=1)
    y = jnp.fft.irfft(zf * kf[None], n=2 * L, axis=1)[:, :L]
    return (y + z.astype(F32) * skip.astype(F32)).astype(z.dtype)


def hyena_branch(u, w_in, b_in, conv_w, conv_b, f_w1, f_b1, f_w2, f_b2, f_freq, f_w3, decay, skip,
                 w_out, b_out):
    L = u.shape[1]
    proj = dwconv_centred(u @ w_in + b_in, conv_w, conv_b)
    v, x1, x2 = jnp.split(proj, 3, axis=-1)
    k = hyena_filters(L, f_w1, f_b1, f_w2, f_b2, f_freq, f_w3, decay)
    z = x1 * bidir_long_conv(v, k[:, 0], k[:, 1], skip[0])
    z = x2 * bidir_long_conv(z, k[:, 2], k[:, 3], skip[1])
    return z @ w_out + b_out


def hyena_mixer(uc, ux, with_ctx_out, *p):
    yx = hyena_branch(ux, *p)
    yc = hyena_branch(uc, *p) if with_ctx_out else None
    return yc, yx


def rwkv7_prep(u, mu, w_rkv, w0, w1, w2, a0, a1, a2, g1, g2, k_k, k_a):
    B, L, _ = u.shape
    H, N = RW_HEADS, RW_HEAD
    up = jnp.pad(u, ((0, 0), (1, 1), (0, 0)))
    xx = 0.5 * (up[:, :-2] + up[:, 2:]) - u
    xr, xw, xk, xv, xa, xg = [u + xx * mu[j] for j in range(6)]

    def heads(t):
        return t.astype(F32).reshape(B, L, H, N)

    r = heads(xr @ w_rkv[0])
    k = heads(xk @ w_rkv[1])
    v = heads(xv @ w_rkv[2])
    g = jax.nn.sigmoid(xg @ g1) @ g2
    kk = l2_normalize(k * k_k.astype(F32).reshape(H, N))
    dirs = []
    for d in range(2):
        logw = -jax.nn.softplus(-heads(w0[d] + jnp.tanh(xw @ w1[d]) @ w2[d])) - 0.5
        a = jax.nn.sigmoid(heads(a0[d] + (xa @ a1[d]) @ a2[d]))
        kd = k * (1.0 + (a - 1.0) * k_a.astype(F32).reshape(H, N))
        dirs.append((jnp.exp(-jnp.exp(logw)), kd, -kk, kk * a))
    return r, v, g, dirs


def wkv7_scan(S0, r, w, k, v, a, b):
    def step(S, inp):
        r_t, w_t, k_t, v_t, a_t, b_t = inp
        sa = jnp.einsum('bhvk,bhk->bhv', S, a_t)
        S = S * w_t[:, :, None, :] + sa[..., None] * b_t[:, :, None, :] + v_t[..., None] * k_t[:, :, None, :]
        return S, jnp.einsum('bhvk,bhk->bhv', S, r_t)
    xs = tuple(jnp.swapaxes(t, 0, 1) for t in (r, w, k, v, a, b))
    S, y = lax.scan(step, S0, xs)
    return S, jnp.swapaxes(y, 0, 1)


def rwkv7_mixer(uc, ux, with_ctx_out, mu, w_rkv, w0, w1, w2, a0, a1, a2, g1, g2, k_k, k_a, r_k,
                lnx_g, lnx_b, w_out):
    H, N = RW_HEADS, RW_HEAD
    rc, vc, gc, dirs_c = rwkv7_prep(uc, mu, w_rkv, w0, w1, w2, a0, a1, a2, g1, g2, k_k, k_a)
    rx, vx, gx, dirs_x = rwkv7_prep(ux, mu, w_rkv, w0, w1, w2, a0, a1, a2, g1, g2, k_k, k_a)
    S0 = jnp.zeros((ux.shape[0], H, N, N), F32)
    yc_sum, yx_sum = 0.0, 0.0
    for d in range(2):
        fl = flip_tokens if d == 1 else keep_tokens
        wc, kc, ac, bc = dirs_c[d]
        wx, kx, ax, bx = dirs_x[d]
        Sc, yc = wkv7_scan(S0, fl(rc), fl(wc), fl(kc), fl(vc), fl(ac), fl(bc))
        _, yx = wkv7_scan(Sc, fl(rx), fl(wx), fl(kx), fl(vx), fl(ax), fl(bx))
        yc_sum = yc_sum + fl(yc)
        yx_sum = yx_sum + fl(yx)

    def finish(y, r, v, g, dirs, dtype):
        B, L = y.shape[:2]
        yc0 = y - jnp.mean(y, -1, keepdims=True)
        var = jnp.mean(yc0 * yc0, -1, keepdims=True)
        y = yc0 * lax.rsqrt(var + RW_GN_EPS) * lnx_g.astype(F32).reshape(H, N) + lnx_b.astype(F32).reshape(H, N)
        rk = r_k.astype(F32)
        bonus = (jnp.sum(r * dirs[0][1] * rk, -1, keepdims=True) * v
                 + jnp.sum(r * dirs[1][1] * rk, -1, keepdims=True) * v)
        y = (y + bonus).reshape(B, L, D_MODEL).astype(dtype)
        return (y * g) @ w_out

    yx = finish(yx_sum, rx, vx, gx, dirs_x, ux.dtype)
    yc = finish(yc_sum, rc, vc, gc, dirs_c, uc.dtype) if with_ctx_out else None
    return yc, yx


def ssd_prep(u, w_in, conv_w, conv_b, dt_bias):
    B, L, _ = u.shape
    G, R, P, N = SSD_GROUPS, SSD_HEADS // SSD_GROUPS, SSD_HEAD, SSD_STATE
    z, xbc, dt = jnp.split(u @ w_in, [SSD_INNER, 2 * SSD_INNER + 2 * G * N], axis=-1)
    xbc = jax.nn.silu(dwconv_centred(xbc, conv_w, conv_b))
    xs, Bm, Cm = jnp.split(xbc, [SSD_INNER, SSD_INNER + G * N], axis=-1)
    xs = xs.astype(F32).reshape(B, L, G, R, P)
    Bm = Bm.astype(F32).reshape(B, L, G, N)
    Cm = Cm.astype(F32).reshape(B, L, G, N)
    dt = jax.nn.softplus(dt.astype(F32).reshape(B, L, 2, G, R) + dt_bias.astype(F32).reshape(2, G, R))
    return z, xs, Bm, Cm, dt


def ssd_scan(S0, x, Bm, Cm, dt, A):
    Bsz, L = x.shape[:2]
    nc = L // SSD_CHUNK

    def chunks(t):
        return jnp.moveaxis(t.reshape(Bsz, nc, SSD_CHUNK, *t.shape[2:]), 1, 0)

    tri = jnp.tril(jnp.ones((SSD_CHUNK, SSD_CHUNK), bool))[None, :, :, None, None]

    def step(S, inp):
        xq, Bq, Cq, dtq = inp
        cum = jnp.cumsum(dtq * A, axis=1)
        decay = jnp.exp(jnp.where(tri, cum[:, :, None] - cum[:, None], -jnp.inf))
        scores = jnp.einsum('bign,bjgn->bijg', Cq, Bq)[..., None] * decay * dtq[:, None]
        y = (jnp.einsum('bijgr,bjgrp->bigrp', scores, xq)
             + jnp.einsum('bign,bgrpn->bigrp', Cq, S) * jnp.exp(cum)[..., None])
        w_end = jnp.exp(cum[:, -1:] - cum) * dtq
        S = S * jnp.exp(cum[:, -1])[..., None, None] + jnp.einsum('bjgr,bjgn,bjgrp->bgrpn', w_end, Bq, xq)
        return S, y

    S, y = lax.scan(step, S0, tuple(chunks(t) for t in (x, Bm, Cm, dt)))
    return S, jnp.moveaxis(y, 0, 1).reshape(x.shape)


def ssd_mixer(uc, ux, with_ctx_out, w_in, conv_w, conv_b, dt_bias, A_log, d_skip, norm_g, w_out):
    G, R, P, N = SSD_GROUPS, SSD_HEADS // SSD_GROUPS, SSD_HEAD, SSD_STATE
    zc, xs_c, Bc, Cc, dtc = ssd_prep(uc, w_in, conv_w, conv_b, dt_bias)
    zx, xs_x, Bx, Cx, dtx = ssd_prep(ux, w_in, conv_w, conv_b, dt_bias)
    A = -jnp.exp(A_log.astype(F32)).reshape(2, G, R)
    d_gr = d_skip.astype(F32).reshape(G, R, 1)
    yc_sum, yx_sum = xs_c * d_gr, xs_x * d_gr
    S0 = jnp.zeros((ux.shape[0], G, R, P, N), F32)
    for d in range(2):
        fl = flip_tokens if d == 1 else keep_tokens
        Sc, yc = ssd_scan(S0, fl(xs_c), fl(Bc), fl(Cc), fl(dtc[:, :, d]), A[d])
        _, yx = ssd_scan(Sc, fl(xs_x), fl(Bx), fl(Cx), fl(dtx[:, :, d]), A[d])
        yc_sum = yc_sum + fl(yc)
        yx_sum = yx_sum + fl(yx)

    def finish(y, z):
        B, L = z.shape[:2]
        y = y.reshape(B, L, SSD_INNER).astype(z.dtype) * jax.nn.silu(z)
        y = rms_norm(y.reshape(B, L, G, SSD_INNER // G), norm_g.reshape(G, SSD_INNER // G))
        return y.reshape(B, L, SSD_INNER) @ w_out

    yx = finish(yx_sum, zx)
    yc = finish(yc_sum, zc) if with_ctx_out else None
    return yc, yx


def gdn_prep(u, w_in, conv_w, dt_bias, A_log):
    B, L, _ = u.shape
    Hk, Hv, Dh = GDN_QK_HEADS, GDN_V_HEADS, GDN_HEAD
    QK, VW = Hk * Dh, Hv * Dh
    qkv, z, ab = jnp.split(u @ w_in, [2 * QK + VW, 2 * QK + 2 * VW], axis=-1)
    qkv = jax.nn.silu(dwconv_centred(qkv, conv_w))
    q, k, v = jnp.split(qkv, [QK, 2 * QK], axis=-1)
    rep = Hv // Hk
    q = jnp.repeat(l2_normalize(q.reshape(B, L, Hk, Dh)), rep, axis=2) * (Dh ** -0.5)
    k = jnp.repeat(l2_normalize(k.reshape(B, L, Hk, Dh)), rep, axis=2)
    v = v.astype(F32).reshape(B, L, Hv, Dh)
    ab = ab.astype(F32).reshape(B, L, 2, 2, Hv)
    g = -jnp.exp(A_log.astype(F32)) * jax.nn.softplus(ab[:, :, 0] + dt_bias.astype(F32))
    beta = jax.nn.sigmoid(ab[:, :, 1])
    return q, k, v, z, g, beta


def gdn_scan(S0, q, k, v, g, beta):
    Bsz, L, H, Kd = k.shape
    Vd = v.shape[-1]
    C = GDN_CHUNK
    nc = L // C

    def chunks(t):
        t = t.reshape(Bsz, nc, C, H, *t.shape[3:])
        return jnp.moveaxis(jnp.moveaxis(t, 1, 0), 3, 2)

    eye = jnp.eye(C, dtype=F32)
    strict = jnp.tril(jnp.ones((C, C), bool), -1)
    incl = jnp.tril(jnp.ones((C, C), bool))

    def seg(Gq, mask):
        return jnp.exp(jnp.where(mask, Gq[..., :, None] - Gq[..., None, :], -jnp.inf))

    def step(S, inp):
        qn, kn, vn, gn, bn = inp
        Gn = jnp.cumsum(gn, axis=-1)
        kb = kn * bn[..., None]
        A = jnp.einsum('bhik,bhjk->bhij', kb, kn) * seg(Gn, strict)
        rhs = jnp.concatenate([vn * bn[..., None], kb * jnp.exp(Gn)[..., None]], axis=-1)
        sol = lax.linalg.triangular_solve(eye + A, rhs, left_side=True, lower=True, unit_diagonal=True)
        u = sol[..., :Vd] - jnp.einsum('bhck,bhkv->bhcv', sol[..., Vd:], S)
        attn = jnp.einsum('bhik,bhjk->bhij', qn, kn) * seg(Gn, incl)
        o = (jnp.einsum('bhck,bhkv->bhcv', qn * jnp.exp(Gn)[..., None], S)
             + jnp.einsum('bhij,bhjv->bhiv', attn, u))
        S = (S * jnp.exp(Gn[..., -1])[..., None, None]
             + jnp.einsum('bhck,bhcv->bhkv', kn * jnp.exp(Gn[..., -1:] - Gn)[..., None], u))
        return S, o

    S, o = lax.scan(step, S0, tuple(chunks(t) for t in (q, k, v, g, beta)))
    o = jnp.moveaxis(jnp.moveaxis(o, 0, 1), 2, 3).reshape(Bsz, L, H, Vd)
    return S, o


def gdn_mixer(uc, ux, with_ctx_out, w_in, conv_w, dt_bias, A_log, norm_g, w_out):
    Hv, Dh = GDN_V_HEADS, GDN_HEAD
    qc, kc, vc, zc, gc, bc = gdn_prep(uc, w_in, conv_w, dt_bias, A_log)
    qx, kx, vx, zx, gx, bx = gdn_prep(ux, w_in, conv_w, dt_bias, A_log)
    S0 = jnp.zeros((ux.shape[0], Hv, Dh, Dh), F32)
    oc_sum, ox_sum = 0.0, 0.0
    for d in range(2):
        fl = flip_tokens if d == 1 else keep_tokens
        Sc, oc = gdn_scan(S0, fl(qc), fl(kc), fl(vc), fl(gc[:, :, d]), fl(bc[:, :, d]))
        _, ox = gdn_scan(Sc, fl(qx), fl(kx), fl(vx), fl(gx[:, :, d]), fl(bx[:, :, d]))
        oc_sum = oc_sum + fl(oc)
        ox_sum = ox_sum + fl(ox)

    def finish(o, z):
        B, L = z.shape[:2]
        o = rms_norm(o, norm_g).astype(z.dtype) * jax.nn.silu(z.reshape(B, L, Hv, Dh))
        return o.reshape(B, L, Hv * Dh) @ w_out

    yx = finish(ox_sum, zx)
    yc = finish(oc_sum, zc) if with_ctx_out else None
    return yc, yx


def swiglu(h, w13, w2):
    gate, up = jnp.split(h @ w13, 2, axis=-1)
    return (jax.nn.silu(gate) * up) @ w2


def moe_swiglu(h, router, w13, w2):
    logits = (h @ router).astype(F32)
    top_val, top_idx = lax.top_k(logits, TOP_K)
    weights = jax.nn.softmax(top_val, axis=-1)
    combine = jnp.einsum('blk,blke->ble', weights,
                         jax.nn.one_hot(top_idx, N_EXPERTS, dtype=F32)).astype(h.dtype)
    out = jnp.zeros_like(h)
    for e in range(N_EXPERTS):
        out = out + combine[..., e:e + 1] * swiglu(h, w13[e], w2[e])
    return out


def channel_mixer(h, i, ffn_w13, ffn_w2, moe_router, moe_w13, moe_w2):
    if i % 2 == 0:
        return swiglu(h, ffn_w13[i // 2], ffn_w2[i // 2])
    return moe_swiglu(h, moe_router[i // 2], moe_w13[i // 2], moe_w2[i // 2])


def setup_inputs(seed: int = 0) -> dict:
    key = jax.random.key(seed)
    keys = iter(jax.random.split(key, 96))
    D = D_MODEL

    def nrm(shape, scale):
        return scale * jax.random.normal(next(keys), shape, F32)

    def gain(shape):
        return 1.0 + nrm(shape, 0.05)

    def dt_bias_init(shape):
        dt = jnp.exp(jax.random.uniform(next(keys), shape, F32, math.log(1e-3), math.log(1e-1)))
        return dt + jnp.log(-jnp.expm1(-dt))

    def a_log_init(shape):
        return jnp.log(jax.random.uniform(next(keys), shape, F32, 1.0, 16.0))

    n_dense = (DEPTH + 1) // 2
    n_moe = DEPTH // 2
    hy_rates = jnp.abs(jnp.linspace(math.log(HY_TARGET) / HY_SLOW, math.log(HY_TARGET) / HY_FAST, D, dtype=F32))
    rw_n = jnp.arange(D, dtype=F32) / (D - 1)
    rw_speed = -7.0 + 5.0 * rw_n ** (0.85 + (1.0 / (DEPTH - 1)) ** 0.5) + 0.5
    G, N = SSD_GROUPS, SSD_STATE
    ssd_in = 2 * SSD_INNER + 2 * G * N + 2 * SSD_HEADS
    qk_w, v_w = GDN_QK_HEADS * GDN_HEAD, GDN_V_HEADS * GDN_HEAD
    gdn_in = 2 * qk_w + 2 * v_w + 4 * GDN_V_HEADS
    return {
        'x': nrm((BATCH, SEQ, D), 1.0),
        'c': nrm((BATCH, D), 1.0),
        'ctx': nrm((BATCH, CTX_LEN, D), 1.0),
        'c_ctx': nrm((D,), 1.0),
        'mod_w': nrm((DEPTH, D, MOD_CHUNKS * D), 0.5 * D ** -0.5),
        'mod_b': nrm((DEPTH, MOD_CHUNKS * D), 0.02),
        'ln_g': gain((DEPTH, 2, D)),
        'ln_b': nrm((DEPTH, 2, D), 0.02),
        'hy_w_in': nrm((D, 3 * D), D ** -0.5),
        'hy_b_in': nrm((3 * D,), 0.02),
        'hy_conv_w': nrm((SHORT_CONV, 3 * D), SHORT_CONV ** -0.5),
        'hy_conv_b': nrm((3 * D,), 0.02),
        'hy_f_w1': nrm((HY_EMB, HY_FILTER_W), HY_EMB ** -0.5),
        'hy_f_b1': nrm((HY_FILTER_W,), 0.02),
        'hy_f_w2': nrm((HY_FILTER_W, HY_FILTER_W), HY_FILTER_W ** -0.5),
        'hy_f_b2': nrm((HY_FILTER_W,), 0.02),
        'hy_f_freq': gain((2, HY_FILTER_W)),
        'hy_f_w3': nrm((HY_FILTER_W, 2 * HY_ORDER * D), HY_FILTER_W ** -0.5),
        'hy_decay': hy_rates[None] * gain((2 * HY_ORDER, D)),
        'hy_skip': nrm((HY_ORDER, D), 1.0),
        'hy_w_out': nrm((D, D), BETA * D ** -0.5),
        'hy_b_out': nrm((D,), 0.02),
        'rw_mu': jax.random.uniform(next(keys), (6, D), F32, 0.0, 1.0),
        'rw_w_rkv': nrm((3, D, D), D ** -0.5),
        'rw_w0': rw_speed[None] + nrm((2, D), 0.1),
        'rw_w1': nrm((2, D, RW_DECAY_LORA), D ** -0.5),
        'rw_w2': nrm((2, RW_DECAY_LORA, D), 0.1 * RW_DECAY_LORA ** -0.5),
        'rw_a0': nrm((2, D), 0.1),
        'rw_a1': nrm((2, D, RW_AAA_LORA), D ** -0.5),
        'rw_a2': nrm((2, RW_AAA_LORA, D), 0.1 * RW_AAA_LORA ** -0.5),
        'rw_g1': nrm((D, RW_GATE_LORA), D ** -0.5),
        'rw_g2': nrm((RW_GATE_LORA, D), RW_GATE_LORA ** -0.5),
        'rw_k_k': 0.85 + nrm((D,), 0.05),
        'rw_k_a': gain((D,)),
        'rw_r_k': nrm((RW_HEADS, RW_HEAD), 0.1),
        'rw_lnx_g': gain((D,)),
        'rw_lnx_b': nrm((D,), 0.02),
        'rw_w_out': nrm((D, D), BETA * D ** -0.5),
        'ssd_w_in': nrm((D, ssd_in), D ** -0.5),
        'ssd_conv_w': nrm((SHORT_CONV, SSD_INNER + 2 * G * N), SHORT_CONV ** -0.5),
        'ssd_conv_b': nrm((SSD_INNER + 2 * G * N,), 0.02),
        'ssd_dt_bias': dt_bias_init((2, SSD_HEADS)),
        'ssd_A_log': a_log_init((2, SSD_HEADS)),
        'ssd_d_skip': gain((SSD_HEADS,)),
        'ssd_norm_g': gain((SSD_INNER,)),
        'ssd_w_out': nrm((SSD_INNER, D), BETA * SSD_INNER ** -0.5),
        'gdn_w_in': nrm((D, gdn_in), D ** -0.5),
        'gdn_conv_w': nrm((SHORT_CONV, 2 * qk_w + v_w), SHORT_CONV ** -0.5),
        'gdn_dt_bias': dt_bias_init((2, GDN_V_HEADS)),
        'gdn_A_log': a_log_init((2, GDN_V_HEADS)),
        'gdn_norm_g': gain((GDN_HEAD,)),
        'gdn_w_out': nrm((v_w, D), BETA * v_w ** -0.5),
        'ffn_w13': nrm((n_dense, D, 2 * FFN_DENSE), D ** -0.5),
        'ffn_w2': nrm((n_dense, FFN_DENSE, D), BETA * FFN_DENSE ** -0.5),
        'moe_router': nrm((n_moe, D, N_EXPERTS), D ** -0.5),
        'moe_w13': nrm((n_moe, N_EXPERTS, D, 2 * FFN_EXPERT), D ** -0.5),
        'moe_w2': nrm((n_moe, N_EXPERTS, FFN_EXPERT, D), BETA * FFN_EXPERT ** -0.5),
    }


def reference(x, c, ctx, c_ctx, mod_w, mod_b, ln_g, ln_b,
              hy_w_in, hy_b_in, hy_conv_w, hy_conv_b, hy_f_w1, hy_f_b1, hy_f_w2, hy_f_b2, hy_f_freq,
              hy_f_w3, hy_decay, hy_skip, hy_w_out, hy_b_out,
              rw_mu, rw_w_rkv, rw_w0, rw_w1, rw_w2, rw_a0, rw_a1, rw_a2, rw_g1, rw_g2, rw_k_k, rw_k_a,
              rw_r_k, rw_lnx_g, rw_lnx_b, rw_w_out,
              ssd_w_in, ssd_conv_w, ssd_conv_b, ssd_dt_bias, ssd_A_log, ssd_d_skip, ssd_norm_g, ssd_w_out,
              gdn_w_in, gdn_conv_w, gdn_dt_bias, gdn_A_log, gdn_norm_g, gdn_w_out,
              ffn_w13, ffn_w2, moe_router, moe_w13, moe_w2):
    hx, hc = x, ctx
    for i in range(DEPTH):
        last = i == DEPTH - 1
        sh1x, sc1x, g1x, sh2x, sc2x, g2x = [m[:, None, :] for m in modulation(c, mod_w[i], mod_b[i])]
        sh1c, sc1c, g1c, sh2c, sc2c, g2c = modulation(c_ctx, mod_w[i], mod_b[i])
        ux = hx * (1.0 + sc1x) + sh1x
        uc = hc * (1.0 + sc1c) + sh1c
        kind = i % N_MIXERS
        if kind == 0:
            yc, yx = hyena_mixer(uc, ux, not last, hy_w_in, hy_b_in, hy_conv_w, hy_conv_b, hy_f_w1, hy_f_b1,
                                 hy_f_w2, hy_f_b2, hy_f_freq, hy_f_w3, hy_decay, hy_skip, hy_w_out, hy_b_out)
        elif kind == 1:
            yc, yx = rwkv7_mixer(uc, ux, not last, rw_mu, rw_w_rkv, rw_w0, rw_w1, rw_w2, rw_a0, rw_a1, rw_a2,
                                 rw_g1, rw_g2, rw_k_k, rw_k_a, rw_r_k, rw_lnx_g, rw_lnx_b, rw_w_out)
        elif kind == 2:
            yc, yx = ssd_mixer(uc, raster_to_column(ux), not last, ssd_w_in, ssd_conv_w, ssd_conv_b,
                               ssd_dt_bias, ssd_A_log, ssd_d_skip, ssd_norm_g, ssd_w_out)
            yx = column_to_raster(yx)
        else:
            yc, yx = gdn_mixer(uc, raster_to_column(ux), not last, gdn_w_in, gdn_conv_w, gdn_dt_bias,
                               gdn_A_log, gdn_norm_g, gdn_w_out)
            yx = column_to_raster(yx)
        hx = layer_norm(ALPHA * hx + g1x * yx, ln_g[i, 0], ln_b[i, 0])
        fx = channel_mixer(hx * (1.0 + sc2x) + sh2x, i, ffn_w13, ffn_w2, moe_router, moe_w13, moe_w2)
        hx = layer_norm(ALPHA * hx + g2x * fx, ln_g[i, 1], ln_b[i, 1])
        if not last:
            hc = layer_norm(ALPHA * hc + g1c * yc, ln_g[i, 0], ln_b[i, 0])
            fc = channel_mixer(hc * (1.0 + sc2c) + sh2c, i, ffn_w13, ffn_w2, moe_router, moe_w13, moe_w2)
            hc = layer_norm(ALPHA * hc + g2c * fc, ln_g[i, 1], ln_b[i, 1])
    return hx
```

```python
import functools
import math

import jax
import jax.numpy as jnp
import numpy as np
from jax import lax
from jax.experimental import pallas as pl
from jax.experimental.pallas import tpu as pltpu

F32 = jnp.float32
BF16 = jnp.bfloat16
HIGHEST = lax.Precision.HIGHEST

GRID_W = 64
LN_EPS = 1e-5
HY_BANDS = 8
RW_HEAD = 64
RW_GN_EPS = 64e-5
SSD_HEAD = 64
SSD_STATE = 128
SSD_GROUPS = 4
GDN_HEAD = 128
CHUNK = 64
TOP_K = 2

V7X_LANES = 128
V7X_SUBLANES = 8
V7X_BF16_ROWS = 16
V7X_VMEM_BUDGET = 56 * 1024 * 1024

NT_DIMS = (((1,), (1,)), ((), ()))
TN_DIMS = (((0,), (0,)), ((), ()))


def _tile(n, pref, mult):
    best = None
    t = mult
    while t <= min(n, pref):
        if n % t == 0:
            best = t
        t += mult
    return best if best is not None else n


def _pad_to(a, axis, size):
    pad = size - a.shape[axis]
    if pad == 0:
        return a
    cfg = [(0, 0)] * a.ndim
    cfg[axis] = (0, pad)
    return jnp.pad(a, cfg)


def _cparams(sem):
    return pltpu.CompilerParams(dimension_semantics=sem, vmem_limit_bytes=V7X_VMEM_BUDGET)


def _silu(x):
    return x * jax.nn.sigmoid(x)


def _softplus(x):
    return jnp.maximum(x, 0.0) + jnp.log1p(jnp.exp(-jnp.abs(x)))


def _bdot(a, b):
    return jnp.dot(a.astype(BF16), b.astype(BF16), preferred_element_type=F32)


def _bdot_nt(a, b):
    return lax.dot_general(a.astype(BF16), b.astype(BF16), NT_DIMS, preferred_element_type=F32)


def _hdot(a, b):
    return jnp.dot(a, b, precision=HIGHEST, preferred_element_type=F32)


def _split_dot(x, m):
    hi = x.astype(BF16)
    lo = (x - hi.astype(F32)).astype(BF16)
    return jnp.dot(hi, m, preferred_element_type=F32) + jnp.dot(lo, m, preferred_element_type=F32)


def _transpose_via_eye(x, eye):
    return lax.dot_general(eye, x, NT_DIMS, precision=HIGHEST, preferred_element_type=F32)


def _block_ones(n, blk):
    i = np.arange(n)
    return jnp.asarray((i[:, None] // blk == i[None, :] // blk).astype(np.float32), BF16)


def _expand_mat(n_in_pad, n_heads, width):
    m = np.zeros((n_in_pad, n_heads * width), np.float32)
    for h in range(n_heads):
        m[h, h * width:(h + 1) * width] = 1.0
    return jnp.asarray(m, BF16)


def _perm_mat(ib, jb):
    n = ib * jb
    q = np.arange(n)
    p = (q % ib) * jb + q // ib
    m = np.zeros((n, n), np.float32)
    m[q, p] = 1.0
    return jnp.asarray(m, BF16)


def _time_masks(d, c):
    ii = lax.broadcasted_iota(jnp.int32, (c, c), 0)
    jj = lax.broadcasted_iota(jnp.int32, (c, c), 1)
    lag = (ii - jj) * (1 - 2 * d)
    return lag >= 0, lag > 0


def _mm_kernel(*refs, nk, pre, act, has_bias, has_perm):
    it = iter(refs)
    x_ref, w_ref = next(it), next(it)
    b_ref = next(it) if has_bias else None
    p_ref = next(it) if has_perm else None
    o_ref = next(it)
    acc_ref = next(it) if nk > 1 else None

    x = x_ref[...]
    x = x.reshape(-1, x.shape[-1])
    if pre is not None:
        x = pre(x.astype(F32))
    x = x.astype(BF16)
    if has_perm:
        x = jnp.dot(p_ref[...], x, preferred_element_type=F32).astype(BF16)
    part = jnp.dot(x, w_ref[...], preferred_element_type=F32)

    def finish(r):
        if has_bias:
            r = r + b_ref[...]
        if act is not None:
            r = act(r)
        o_ref[...] = r.astype(o_ref.dtype).reshape(o_ref.shape)

    if nk == 1:
        finish(part)
    else:
        k = pl.program_id(2)

        @pl.when(k == 0)
        def _():
            acc_ref[...] = jnp.zeros_like(acc_ref)

        acc_ref[...] += part

        @pl.when(k == nk - 1)
        def _():
            finish(acc_ref[...])


def _matmul(x, w, bias=None, *, out_dtype=F32, act=None, pre=None, tm=512, tn=512, tk=None, name="matmul"):
    m, kdim = x.shape
    n = w.shape[1]
    tm = _tile(m, tm, V7X_BF16_ROWS)
    tn = _tile(n, tn, V7X_LANES)
    tk = kdim if tk is None else _tile(kdim, tk, V7X_LANES)
    nk = kdim // tk
    in_specs = [pl.BlockSpec((tm, tk), lambda i, j, k: (i, k)),
                pl.BlockSpec((tk, tn), lambda i, j, k: (k, j))]
    args = [x, w]
    if bias is not None:
        in_specs.append(pl.BlockSpec((1, tn), lambda i, j, k: (0, j)))
        args.append(bias.reshape(1, n).astype(F32))
    return pl.pallas_call(
        functools.partial(_mm_kernel, nk=nk, pre=pre, act=act, has_bias=bias is not None, has_perm=False),
        grid=(m // tm, n // tn, nk),
        in_specs=in_specs,
        out_specs=pl.BlockSpec((tm, tn), lambda i, j, k: (i, j)),
        out_shape=jax.ShapeDtypeStruct((m, n), out_dtype),
        scratch_shapes=[pltpu.VMEM((tm, tn), F32)] if nk > 1 else [],
        compiler_params=_cparams(("parallel", "parallel", "arbitrary")),
        name=name,
    )(*args)


def _col_tiles(gw):
    ib = min(V7X_BF16_ROWS, gw)
    jb = min(32, gw)
    return ib, jb


def _matmul_r2c(u, w, bias, nb, *, out_dtype, tn=512, name="matmul_r2c"):
    gw = GRID_W
    kdim, n = w.shape
    ib, jb = _col_tiles(gw)
    tn = _tile(n, tn, V7X_LANES)
    u4 = u.reshape(nb, gw, gw, kdim)
    in_specs = [pl.BlockSpec((1, ib, jb, kdim), lambda b, i, j, c: (b, i, j, 0)),
                pl.BlockSpec((kdim, tn), lambda b, i, j, c: (0, c))]
    args = [u4, w]
    if bias is not None:
        in_specs.append(pl.BlockSpec((1, tn), lambda b, i, j, c: (0, c)))
        args.append(bias.reshape(1, n).astype(F32))
    in_specs.append(pl.BlockSpec((ib * jb, ib * jb), lambda b, i, j, c: (0, 0)))
    args.append(_perm_mat(ib, jb))
    out = pl.pallas_call(
        functools.partial(_mm_kernel, nk=1, pre=None, act=None, has_bias=bias is not None, has_perm=True),
        grid=(nb, gw // ib, gw // jb, n // tn),
        in_specs=in_specs,
        out_specs=pl.BlockSpec((1, jb, ib, tn), lambda b, i, j, c: (b, j, i, c)),
        out_shape=jax.ShapeDtypeStruct((nb, gw, gw, n), out_dtype),
        compiler_params=_cparams(("parallel", "parallel", "parallel", "arbitrary")),
        name=name,
    )(*args)
    return out.reshape(nb * gw * gw, n)


def _ln_epilogue(h, y, gate, ln_g, ln_b, alpha):
    pre = alpha * h + gate * y
    mu = jnp.mean(pre, axis=-1, keepdims=True)
    xc = pre - mu
    var = jnp.mean(xc * xc, axis=-1, keepdims=True)
    return xc * lax.rsqrt(var + LN_EPS) * ln_g + ln_b


def _route(logits, n_exp):
    lane = lax.broadcasted_iota(jnp.int32, logits.shape, 1)
    neg = jnp.float32(-jnp.inf)
    lg = jnp.where(lane < n_exp, logits, neg)
    big = jnp.int32(logits.shape[1])
    m1 = jnp.max(lg, axis=-1, keepdims=True)
    i1 = jnp.min(jnp.where(lg == m1, lane, big), axis=-1, keepdims=True)
    lg2 = jnp.where(lane == i1, neg, lg)
    m2 = jnp.max(lg2, axis=-1, keepdims=True)
    i2 = jnp.min(jnp.where(lg2 == m2, lane, big), axis=-1, keepdims=True)
    e2 = jnp.exp(m2 - m1)
    den = 1.0 + e2
    return jnp.where(lane == i1, 1.0 / den, jnp.where(lane == i2, e2 / den, 0.0))


def _finish_rows(refs, y, *, alpha, emit_u, n_exp):
    h_ref, gate_ref, lng_ref, lnb_ref = refs["h"], refs["gate"], refs["ln_g"], refs["ln_b"]
    h = h_ref[...].reshape(y.shape)
    hn = _ln_epilogue(h, y, gate_ref[0], lng_ref[...], lnb_ref[...], alpha)
    refs["h_out"][...] = hn.reshape(refs["h_out"].shape)
    if emit_u:
        u = hn * (1.0 + refs["scale"][0]) + refs["shift"][0]
        refs["u_out"][...] = u.astype(BF16).reshape(refs["u_out"].shape)
        if n_exp:
            logits = _hdot(u, refs["router"][...])
            refs["comb_out"][...] = _route(logits, n_exp).reshape(refs["comb_out"].shape)


class _RowLayout:
    def __init__(self, nb, seq, per_batch_mod, mode="rows", tm=512):
        self.nb, self.seq, self.mode, self.per_batch_mod = nb, seq, mode, per_batch_mod
        if mode == "rows":
            self.tm = _tile(seq, tm, V7X_BF16_ROWS)
            self.tps = seq // self.tm
            self.grid = (nb * self.tps,)
            self.rows = self.tm
        else:
            self.ib, self.jb = _col_tiles(GRID_W)
            self.grid = (nb, GRID_W // self.ib, GRID_W // self.jb)
            self.rows = self.ib * self.jb
        self.ngrid = len(self.grid)

    def sem(self, extra=()):
        return ("parallel",) * self.ngrid + tuple(extra)

    def _ix(self, fn):
        n = self.ngrid
        return lambda *g: fn(*g[:n])

    def raster(self, arr, c, cblock=0, lead=None):
        pre_shape = () if lead is None else (arr.shape[0],)
        pre_blk = () if lead is None else (1,)
        pre_ix = () if lead is None else (lead,)
        if self.mode == "rows":
            return arr, pl.BlockSpec(pre_blk + (self.tm, c), self._ix(lambda i: pre_ix + (i, cblock)))
        a4 = arr.reshape(pre_shape + (self.nb, GRID_W, GRID_W, arr.shape[-1]))
        return a4, pl.BlockSpec(pre_blk + (1, self.ib, self.jb, c),
                                self._ix(lambda b, i, j: pre_ix + (b, i, j, cblock)))

    def colmajor(self, arr, c, cblock=0, lead=None):
        assert self.mode == "cols"
        pre_shape = () if lead is None else (arr.shape[0],)
        pre_blk = () if lead is None else (1,)
        pre_ix = () if lead is None else (lead,)
        a4 = arr.reshape(pre_shape + (self.nb, GRID_W, GRID_W, arr.shape[-1]))
        return a4, pl.BlockSpec(pre_blk + (1, self.jb, self.ib, c),
                                self._ix(lambda b, i, j: pre_ix + (b, j, i, cblock)))

    def native(self, arr, c, cblock=0, lead=None):
        if self.mode == "cols":
            return self.colmajor(arr, c, cblock, lead)
        return self.raster(arr, c, cblock, lead)

    def mod(self, arr):
        d = arr.shape[-1]
        if not self.per_batch_mod:
            return arr, pl.BlockSpec((1, 1, d), self._ix(lambda *g: (0, 0, 0)))
        if self.mode == "rows":
            tps = self.tps
            return arr, pl.BlockSpec((1, 1, d), self._ix(lambda i: (i // tps, 0, 0)))
        return arr, pl.BlockSpec((1, 1, d), self._ix(lambda b, i, j: (b, 0, 0)))

    def const(self, arr):
        nd = arr.ndim
        return arr, pl.BlockSpec(arr.shape, self._ix(lambda *g: (0,) * nd))

    def out_raster(self, n_rows, c, dtype):
        if self.mode == "rows":
            return (jax.ShapeDtypeStruct((n_rows, c), dtype),
                    pl.BlockSpec((self.tm, c), self._ix(lambda i: (i, 0))))
        return (jax.ShapeDtypeStruct((self.nb, GRID_W, GRID_W, c), dtype),
                pl.BlockSpec((1, self.ib, self.jb, c), self._ix(lambda b, i, j: (b, i, j, 0))))


def _out_proj_call(lay, prologue, pro_inputs, w_out, bias, tail, *, alpha, name):
    d = w_out.shape[1]
    n_rows = lay.nb * lay.seq
    emit_u = "scale" in tail
    n_exp = tail.get("n_exp", 0)
    names, args, specs = [], [], []

    def add(nm, pair):
        names.append(nm)
        args.append(pair[0])
        specs.append(pair[1])

    for k, pair in enumerate(pro_inputs):
        add(f"p{k}", pair)
    add("w", lay.const(w_out))
    if bias is not None:
        add("bias", lay.const(bias.reshape(1, d).astype(F32)))
    if lay.mode == "cols":
        add("perm", lay.const(_perm_mat(lay.ib, lay.jb).T))
    add("h", lay.raster(tail["h"], d))
    add("gate", lay.mod(tail["gate"]))
    add("ln_g", lay.const(tail["ln_g"].reshape(1, d)))
    add("ln_b", lay.const(tail["ln_b"].reshape(1, d)))
    if emit_u:
        add("scale", lay.mod(tail["scale"]))
        add("shift", lay.mod(tail["shift"]))
    if n_exp:
        add("router", lay.const(tail["router"]))
    out_names, out_shapes, out_specs = [], [], []

    def add_out(nm, pair):
        out_names.append(nm)
        out_shapes.append(pair[0])
        out_specs.append(pair[1])

    add_out("h_out", lay.out_raster(n_rows, d, F32))
    if emit_u:
        add_out("u_out", lay.out_raster(n_rows, d, BF16))
    if n_exp:
        add_out("comb_out", lay.out_raster(n_rows, V7X_LANES, F32))
    n_pro = len(pro_inputs)
    n_in = len(names)

    def kernel(*refs):
        r = dict(zip(names + out_names, refs))
        z = prologue(*[refs[k] for k in range(n_pro)])
        z = z.astype(BF16)
        if lay.mode == "cols":
            z = jnp.dot(r["perm"][...], z, preferred_element_type=F32).astype(BF16)
        y = jnp.dot(z, r["w"][...], preferred_element_type=F32)
        if bias is not None:
            y = y + r["bias"][...]
        _finish_rows(r, y, alpha=alpha, emit_u=emit_u, n_exp=n_exp)

    outs = pl.pallas_call(
        kernel, grid=lay.grid, in_specs=specs, out_specs=out_specs, out_shape=out_shapes,
        compiler_params=_cparams(lay.sem()), name=name)(*args)
    outs = [o.reshape(n_rows, o.shape[-1]) for o in outs]
    res = {"h": outs[0]}
    if emit_u:
        res["u"] = outs[1]
    if n_exp:
        res["comb"] = outs[2]
    return res


def _modulate_kernel(h_ref, scale_ref, shift_ref, u_ref):
    u_ref[...] = (h_ref[...] * (1.0 + scale_ref[0]) + shift_ref[0]).astype(u_ref.dtype)


def _modulate(lay, h, scale, shift):
    n_rows, d = h.shape
    pairs = [lay.raster(h, d), lay.mod(scale), lay.mod(shift)]
    out = lay.out_raster(n_rows, d, BF16)
    return pl.pallas_call(
        _modulate_kernel, grid=lay.grid, in_specs=[pr[1] for pr in pairs], out_specs=out[1], out_shape=out[0],
        compiler_params=_cparams(lay.sem()), name="adaln_modulate")(*[pr[0] for pr in pairs])


def _ffn_kernel(*refs, names, n_steps, alpha, emit_u, use_comb, n_e, n_f):
    r = dict(zip(names, refs))
    e, f = pl.program_id(1), pl.program_id(2)
    step = e * n_f + f

    @pl.when(step == 0)
    def _():
        r["acc"][...] = jnp.zeros_like(r["acc"])

    u = r["u"][...]
    gate = jnp.dot(u, r["w1"][0], preferred_element_type=F32)
    up = jnp.dot(u, r["w3"][0], preferred_element_type=F32)
    hid = _silu(gate) * up
    if use_comb:
        comb = r["comb"][...]
        lane = lax.broadcasted_iota(jnp.int32, comb.shape, 1)
        ce = jnp.sum(jnp.where(lane == e, comb, 0.0), axis=-1, keepdims=True)
        hid = hid * ce
    r["acc"][...] += jnp.dot(hid.astype(BF16), r["w2"][0], preferred_element_type=F32)

    @pl.when(step == n_steps - 1)
    def _():
        _finish_rows(r, r["acc"][...], alpha=alpha, emit_u=emit_u, n_exp=0)


def _ffn_call(lay, u, w13, w2, comb, tail, *, alpha, name):
    n_e, d, f2 = w13.shape
    fdim = f2 // 2
    tf = _tile(fdim, 1536, V7X_LANES)
    n_f = fdim // tf
    n_rows = lay.nb * lay.seq
    emit_u = "scale" in tail
    names, args, specs = [], [], []

    def add(nm, pair):
        names.append(nm)
        args.append(pair[0])
        specs.append(pair[1])

    add("u", lay.raster(u, d))
    add("w1", (w13, pl.BlockSpec((1, d, tf), lambda i, e, f: (e, 0, f))))
    add("w3", (w13, pl.BlockSpec((1, d, tf), lambda i, e, f: (e, 0, n_f + f))))
    add("w2", (w2, pl.BlockSpec((1, tf, d), lambda i, e, f: (e, f, 0))))
    if comb is not None:
        add("comb", lay.raster(comb, V7X_LANES))
    add("h", lay.raster(tail["h"], d))
    add("gate", lay.mod(tail["gate"]))
    add("ln_g", lay.const(tail["ln_g"].reshape(1, d)))
    add("ln_b", lay.const(tail["ln_b"].reshape(1, d)))
    if emit_u:
        add("scale", lay.mod(tail["scale"]))
        add("shift", lay.mod(tail["shift"]))
    out_names = ["h_out"] + (["u_out"] if emit_u else [])
    outs = [lay.out_raster(n_rows, d, F32)] + ([lay.out_raster(n_rows, d, BF16)] if emit_u else [])
    kernel = functools.partial(
        _ffn_kernel, names=names + out_names + ["acc"], n_steps=n_e * n_f, alpha=alpha, emit_u=emit_u,
        use_comb=comb is not None, n_e=n_e, n_f=n_f)
    res = pl.pallas_call(
        kernel, grid=lay.grid + (n_e, n_f), in_specs=specs,
        out_specs=[o[1] for o in outs], out_shape=[o[0] for o in outs],
        scratch_shapes=[pltpu.VMEM((lay.rows, d), F32)],
        compiler_params=_cparams(lay.sem(("arbitrary", "arbitrary"))), name=name)(*args)
    out = {"h": res[0]}
    if emit_u:
        out["u"] = res[1]
    return out


def _shift_rows(cur, prev_row, next_row):
    n = cur.shape[0]
    rows = lax.broadcasted_iota(jnp.int32, cur.shape, 0)
    up = jnp.where(rows == 0, prev_row, pltpu.roll(cur, 1, 0))
    dn = jnp.where(rows == n - 1, next_row, pltpu.roll(cur, n - 1, 0))
    return up, dn


def _halo_rows(i, tps, xp_ref, xn_ref, hb):
    t = i % tps
    prev_row = jnp.where(t == 0, 0.0, xp_ref[hb - 1:hb, :].astype(F32))
    next_row = jnp.where(t == tps - 1, 0.0, xn_ref[0:1, :].astype(F32))
    return prev_row, next_row


def _halo_rows_index(tm, n_rows, hb):
    r = tm // hb
    last = n_rows // hb - 1
    return (lambda i: jnp.maximum(i * r - 1, 0)), (lambda i: jnp.minimum((i + 1) * r, last))


def _conv3_kernel(x_ref, xp_ref, xn_ref, w_ref, b_ref, o_ref, *, tps, act):
    i = pl.program_id(0)
    cur = x_ref[...].astype(F32)
    prev_row, next_row = _halo_rows(i, tps, xp_ref, xn_ref, V7X_SUBLANES)
    up, dn = _shift_rows(cur, prev_row, next_row)
    w = w_ref[...]
    y = w[0:1] * up + w[1:2] * cur + w[2:3] * dn + b_ref[...]
    if act is not None:
        y = act(y)
    o_ref[...] = y.astype(o_ref.dtype).reshape(o_ref.shape)


def _conv3(x, w, b, seq, *, act=None, out_dtype=BF16, time_major_nb=None, name="conv3"):
    n_rows, c = x.shape
    tm = _tile(seq, 512, V7X_SUBLANES)
    tps = seq // tm
    if b is None:
        b = jnp.zeros((c,), F32)
    if time_major_nb is None:
        ct = _tile(c, 1024, V7X_LANES)
        out_shape = jax.ShapeDtypeStruct((n_rows, c), out_dtype)
        out_spec = pl.BlockSpec((tm, ct), lambda i, cc: (i, cc))
    else:
        nb, d = time_major_nb
        ct = d
        out_shape = jax.ShapeDtypeStruct((c // d, seq, nb * d), out_dtype)
        out_spec = pl.BlockSpec((1, tm, d), lambda i, cc: (cc, i % tps, i // tps))
    hb = V7X_SUBLANES
    prev, nxt = _halo_rows_index(tm, n_rows, hb)
    return pl.pallas_call(
        functools.partial(_conv3_kernel, tps=tps, act=act),
        grid=(n_rows // tm, c // ct),
        in_specs=[pl.BlockSpec((tm, ct), lambda i, cc: (i, cc)),
                  pl.BlockSpec((hb, ct), lambda i, cc: (prev(i), cc)),
                  pl.BlockSpec((hb, ct), lambda i, cc: (nxt(i), cc)),
                  pl.BlockSpec((3, ct), lambda i, cc: (0, cc)),
                  pl.BlockSpec((1, ct), lambda i, cc: (0, cc))],
        out_specs=out_spec, out_shape=out_shape,
        compiler_params=_cparams(("parallel", "parallel")), name=name,
    )(x, x, x, w.astype(F32), b.reshape(1, c).astype(F32))


def _hy_filter_kernel(bands_ref, w1_ref, b1_ref, w2_ref, b2_ref, fr_ref, w3_ref, dec_ref, sum_ref, dif_ref,
                      *, seq, tl, d):
    i = pl.program_id(0)
    pos = (lax.broadcasted_iota(jnp.int32, (tl, V7X_LANES), 0) + i * tl).astype(F32)
    lane = lax.broadcasted_iota(jnp.int32, (tl, V7X_LANES), 1)
    t01 = pos / float(max(seq - 1, 1))
    ang = (2.0 * math.pi / seq) * pos * bands_ref[...]
    feats = jnp.where(lane == 0, t01, jnp.where(lane <= HY_BANDS, jnp.cos(ang), -jnp.sin(ang)))
    fr = fr_ref[...]
    h = jnp.sin(fr[0:1] * (_hdot(feats, w1_ref[...]) + b1_ref[...]))
    h = jnp.sin(fr[1:2] * (_hdot(h, w2_ref[...]) + b2_ref[...]))
    k = _hdot(h, w3_ref[...]) * jnp.exp(-t01[:, 0:1] * jnp.abs(dec_ref[...]))
    not_first = (pos[:, 0:1] > 0.0).astype(F32)
    for o in range(2):
        kf = k[:, (2 * o) * d:(2 * o + 1) * d]
        kb = k[:, (2 * o + 1) * d:(2 * o + 2) * d] * not_first
        sum_ref[:, o * d:(o + 1) * d] = (kf + kb).astype(sum_ref.dtype)
        dif_ref[:, o * d:(o + 1) * d] = (kb - kf).astype(dif_ref.dtype)


def _hy_filters(seq, p, d):
    lanes = V7X_LANES
    fw = p["hy_f_w1"].shape[1]
    bands = jnp.linspace(1e-4, HY_BANDS - 1, HY_BANDS, dtype=F32)
    bands_row = _pad_to(jnp.concatenate([jnp.zeros((1,), F32), bands, bands])[None, :], 1, lanes)
    w1 = _pad_to(_pad_to(p["hy_f_w1"].astype(F32), 0, lanes), 1, lanes)
    b1 = _pad_to(p["hy_f_b1"].astype(F32)[None, :], 1, lanes)
    w2 = _pad_to(_pad_to(p["hy_f_w2"].astype(F32), 0, lanes), 1, lanes)
    b2 = _pad_to(p["hy_f_b2"].astype(F32)[None, :], 1, lanes)
    fr = _pad_to(_pad_to(p["hy_f_freq"].astype(F32), 1, lanes), 0, V7X_SUBLANES)
    w3 = _pad_to(p["hy_f_w3"].astype(F32), 0, lanes)
    dec = p["hy_decay"].astype(F32).reshape(1, 4 * d)
    assert fw <= lanes
    tl = _tile(seq, 256, V7X_BF16_ROWS)
    full = lambda a: pl.BlockSpec(a.shape, lambda i: (0,) * a.ndim)
    ins = [bands_row, w1, b1, w2, b2, fr, w3, dec]
    return pl.pallas_call(
        functools.partial(_hy_filter_kernel, seq=seq, tl=tl, d=d),
        grid=(seq // tl,),
        in_specs=[full(a) for a in ins],
        out_specs=[pl.BlockSpec((tl, 2 * d), lambda i: (i, 0))] * 2,
        out_shape=[jax.ShapeDtypeStruct((seq, 2 * d), BF16)] * 2,
        compiler_params=_cparams(("parallel",)), name="hyena_filters",
    )(*ins)


def _dft_kernel(c_ref, s_ref, ct_ref, st_ref, *, seq, tr):
    i = pl.program_id(0)
    row = lax.broadcasted_iota(jnp.int32, (tr, seq), 0) + i * tr
    col = lax.broadcasted_iota(jnp.int32, (tr, seq), 1)
    scale = math.pi / (2 * seq)
    m = ((2 * row + 1) * col) & (4 * seq - 1)
    ang = m.astype(F32) * scale
    c_ref[...] = jnp.cos(ang).astype(BF16)
    s_ref[...] = jnp.sin(ang).astype(BF16)
    mt = ((2 * col + 1) * row) & (4 * seq - 1)
    angt = mt.astype(F32) * scale
    ct_ref[...] = jnp.cos(angt).astype(BF16)
    st_ref[...] = jnp.sin(angt).astype(BF16)


def _dft_mats(seq):
    assert seq & (seq - 1) == 0, "token count must be a power of two"
    tr = _tile(seq, 256, V7X_BF16_ROWS)
    spec = pl.BlockSpec((tr, seq), lambda i: (i, 0))
    return pl.pallas_call(
        functools.partial(_dft_kernel, seq=seq, tr=tr), grid=(seq // tr,), in_specs=[],
        out_specs=[spec] * 4, out_shape=[jax.ShapeDtypeStruct((seq, seq), BF16)] * 4,
        compiler_params=_cparams(("parallel",)), name="dft_matrices")()


def _hy_fwd_kernel(c_ref, s_ref, v_ref, kr_ref, ki_ref, wr_ref, wi_ref):
    v = v_ref[0]
    cv = jnp.dot(c_ref[...], v, preferred_element_type=F32)
    sv = jnp.dot(s_ref[...], v, preferred_element_type=F32)
    kr, ki = kr_ref[...], ki_ref[...]
    wr_ref[...] = (cv * kr + sv * ki).astype(wr_ref.dtype)
    wi_ref[...] = (cv * ki - sv * kr).astype(wi_ref.dtype)


def _hy_inv_kernel(ct_ref, st_ref, wr_ref, wi_ref, v_ref, g_ref, skip_ref, o_ref, *, seq):
    y = (jnp.dot(ct_ref[...], wr_ref[...], preferred_element_type=F32)
         - jnp.dot(st_ref[...], wi_ref[...], preferred_element_type=F32)) * (1.0 / seq)
    y = y + v_ref[0].astype(F32) * skip_ref[...]
    o_ref[0] = (g_ref[0].astype(F32) * y).astype(o_ref.dtype)


def _hy_long_conv(vsrc, v_idx, gsrc, g_idx, mats, kr, ki, order, skip, d):
    c, s, ct, st = mats
    _, seq, cols = vsrc.shape
    tm = _tile(seq, 512, V7X_BF16_ROWS)
    tn = _tile(d, 512, V7X_LANES)
    cpd = d // tn
    kspec = pl.BlockSpec((tm, tn), lambda i, j: (i, order * cpd + j % cpd))
    wr, wi = pl.pallas_call(
        _hy_fwd_kernel, grid=(seq // tm, cols // tn),
        in_specs=[pl.BlockSpec((tm, seq), lambda i, j: (i, 0)), pl.BlockSpec((tm, seq), lambda i, j: (i, 0)),
                  pl.BlockSpec((1, seq, tn), lambda i, j: (v_idx, 0, j)), kspec, kspec],
        out_specs=[pl.BlockSpec((tm, tn), lambda i, j: (i, j))] * 2,
        out_shape=[jax.ShapeDtypeStruct((seq, cols), BF16)] * 2,
        compiler_params=_cparams(("parallel", "parallel")), name="hyena_dft_fwd",
    )(c, s, vsrc, kr, ki)
    return pl.pallas_call(
        functools.partial(_hy_inv_kernel, seq=seq), grid=(seq // tm, cols // tn),
        in_specs=[pl.BlockSpec((tm, seq), lambda i, j: (i, 0)), pl.BlockSpec((tm, seq), lambda i, j: (i, 0)),
                  pl.BlockSpec((seq, tn), lambda i, j: (0, j)), pl.BlockSpec((seq, tn), lambda i, j: (0, j)),
                  pl.BlockSpec((1, tm, tn), lambda i, j: (v_idx, i, j)),
                  pl.BlockSpec((1, tm, tn), lambda i, j: (g_idx, i, j)),
                  pl.BlockSpec((1, tn), lambda i, j: (0, j % cpd))],
        out_specs=pl.BlockSpec((1, tm, tn), lambda i, j: (0, i, j)),
        out_shape=jax.ShapeDtypeStruct((1, seq, cols), BF16),
        compiler_params=_cparams(("parallel", "parallel")), name="hyena_dft_inv",
    )(ct, st, wr, wi, vsrc, gsrc, skip.reshape(1, d).astype(F32))


def _hyena_mixer(u, nb, seq, p, w):
    d = u.shape[1]
    proj = _matmul(u, w["hy_w_in"], p["hy_b_in"], out_dtype=F32, name="hyena_in_proj")
    planes = _conv3(proj, p["hy_conv_w"], p["hy_conv_b"], seq, time_major_nb=(nb, d), name="hyena_short_conv")
    ksum, kdif = _hy_filters(seq, p, d)
    mats = _dft_mats(seq)
    kr = _matmul(mats[0], ksum, out_dtype=F32, name="hyena_filter_spec_re")
    ki = _matmul(mats[1], kdif, out_dtype=F32, name="hyena_filter_spec_im")
    z1 = _hy_long_conv(planes, 0, planes, 1, mats, kr, ki, 0, p["hy_skip"][0], d)
    return _hy_long_conv(z1, 0, planes, 2, mats, kr, ki, 1, p["hy_skip"][1], d)


def _hyena_out(lay_rows, z, tail, p, w, nb, seq, *, alpha, name):
    d = w["hy_w_out"].shape[0]
    tm = lay_rows.tm
    tps = lay_rows.tps
    spec = pl.BlockSpec((1, tm, d), lay_rows._ix(lambda i: (0, i % tps, i // tps)))
    return _out_proj_call(lay_rows, lambda z_ref: z_ref[0], [(z, spec)], w["hy_w_out"], p["hy_b_out"], tail,
                          alpha=alpha, name=name)


def _rw_proj_kernel(*refs, names, tps, hd):
    r = dict(zip(names, refs))
    i = pl.program_id(0)
    cur = r["u"][...].astype(F32)
    prev_row, next_row = _halo_rows(i, tps, r["up"], r["un"], V7X_BF16_ROWS)
    up, dn = _shift_rows(cur, prev_row, next_row)
    xx = 0.5 * (up + dn) - cur
    mu = r["mu"][...]
    mix = lambda j: (cur + xx * mu[j:j + 1]).astype(BF16)
    xr, xw, xk, xv, xa, xg = [mix(j) for j in range(6)]
    qq = r["qq"][...]
    rr = jnp.dot(xr, r["w_rkv"][0], preferred_element_type=F32)
    kk0 = jnp.dot(xk, r["w_rkv"][1], preferred_element_type=F32)
    vv = jnp.dot(xv, r["w_rkv"][2], preferred_element_type=F32)
    gg = _bdot(jax.nn.sigmoid(jnp.dot(xg, r["g1"][...], preferred_element_type=F32)), r["g2"][...])
    kx = kk0 * r["k_k"][...]
    kkn = kx * lax.rsqrt(_split_dot(kx * kx, qq) + 1e-6)
    r["r"][...] = rr
    r["v"][...] = vv
    r["g"][...] = gg.astype(BF16)
    r["na"][...] = -kkn
    kd_sum = jnp.zeros_like(kk0)
    for dd in range(2):
        hw = jnp.tanh(jnp.dot(xw, r["lora_w1"][dd], preferred_element_type=F32))
        wpre = r["bias_w"][dd] + _bdot(hw, r["lora_w2"][dd])
        logw = -_softplus(-wpre) - 0.5
        r[f"w{dd}"][...] = jnp.exp(-jnp.exp(logw))
        a = jax.nn.sigmoid(
            r["bias_a"][dd] + _bdot(jnp.dot(xa, r["lora_a1"][dd], preferred_element_type=F32), r["lora_a2"][dd]))
        kd = kk0 * (1.0 + (a - 1.0) * r["k_a"][...])
        r[f"kd{dd}"][...] = kd
        r[f"b{dd}"][...] = kkn * a
        kd_sum = kd_sum + kd
    r["bonus"][...] = (_split_dot(rr * kd_sum * r["r_k"][...], qq) * vv).astype(BF16)


_RW_OUTS = ["r", "v", "g", "na", "w0", "w1", "kd0", "kd1", "b0", "b1", "bonus"]


def _rw_project(u, seq, p, w):
    n_rows, d = u.shape
    hb = V7X_BF16_ROWS
    tm = _tile(seq, 256, hb)
    tps = seq // tm
    lanes = V7X_LANES
    names, args, specs = [], [], []

    def add(nm, arr, spec=None):
        names.append(nm)
        args.append(arr)
        nd = arr.ndim
        specs.append(spec if spec is not None else pl.BlockSpec(arr.shape, lambda i: (0,) * nd))

    prev, nxt = _halo_rows_index(tm, n_rows, hb)
    add("u", u, pl.BlockSpec((tm, d), lambda i: (i, 0)))
    add("up", u, pl.BlockSpec((hb, d), lambda i: (prev(i), 0)))
    add("un", u, pl.BlockSpec((hb, d), lambda i: (nxt(i), 0)))
    add("mu", _pad_to(p["rw_mu"].astype(F32), 0, V7X_SUBLANES))
    add("w_rkv", w["rw_w_rkv"])
    add("g1", w["rw_g1"])
    add("g2", w["rw_g2"])
    add("bias_w", p["rw_w0"].astype(F32).reshape(2, 1, d))
    add("lora_w1", w["rw_w1"])
    add("lora_w2", w["rw_w2"])
    add("bias_a", p["rw_a0"].astype(F32).reshape(2, 1, d))
    add("lora_a1", w["rw_a1"])
    add("lora_a2", w["rw_a2"])
    add("k_k", p["rw_k_k"].astype(F32).reshape(1, d))
    add("k_a", p["rw_k_a"].astype(F32).reshape(1, d))
    add("r_k", p["rw_r_k"].astype(F32).reshape(1, d))
    add("qq", _block_ones(d, RW_HEAD))
    out_dt = {nm: (BF16 if nm in ("g", "bonus") else F32) for nm in _RW_OUTS}
    row_spec = pl.BlockSpec((tm, d), lambda i: (i, 0))
    outs = pl.pallas_call(
        functools.partial(_rw_proj_kernel, names=names + _RW_OUTS, tps=tps, hd=RW_HEAD),
        grid=(n_rows // tm,), in_specs=specs,
        out_specs=[row_spec] * len(_RW_OUTS),
        out_shape=[jax.ShapeDtypeStruct((n_rows, d), out_dt[nm]) for nm in _RW_OUTS],
        compiler_params=_cparams(("parallel",)), name="rwkv7_projections")(*args)
    return dict(zip(_RW_OUTS, outs))


def _rw_scan_kernel(r_ref, w_ref, k_ref, v_ref, a_ref, b_ref, s0_ref, qq_ref, y_ref, sfin_ref, s_ref,
                    *, nb, tc, n_chunks, d, hd):
    dirn, c = pl.program_id(0), pl.program_id(1)

    @pl.when(c == 0)
    def _():
        s_ref[...] = s0_ref[0]

    qq = qq_ref[...]
    ntile = d // qq.shape[0]
    wq = qq.shape[0]
    vi = lax.broadcasted_iota(jnp.int32, (hd, d), 0)
    li = lax.broadcasted_iota(jnp.int32, (hd, d), 1)
    diag = (li % hd == vi).astype(F32)

    def seg(x):
        xb = x.astype(BF16)
        return jnp.concatenate(
            [jnp.dot(xb[:, j * wq:(j + 1) * wq], qq, preferred_element_type=F32) for j in range(ntile)], axis=1)

    def step(t, carry):
        tt = jnp.where(dirn == 0, t, tc - 1 - t)
        for bi in range(nb):
            row = lambda ref: ref[bi, pl.ds(tt, 1), :]
            s = s_ref[bi]
            sab = seg(s * row(a_ref))
            vx = seg(jnp.broadcast_to(row(v_ref), (hd, d)) * diag)
            rowd = lambda ref: ref[0, bi, pl.ds(tt, 1), :]
            s_new = s * rowd(w_ref) + sab * rowd(b_ref) + vx * rowd(k_ref)
            s_ref[bi] = s_new
            yb = seg(s_new * row(r_ref))
            y_ref[0, bi, pl.ds(tt, 1), :] = jnp.sum(yb * diag, axis=0, keepdims=True)
        return carry

    lax.fori_loop(0, tc, step, 0)

    @pl.when(c == n_chunks - 1)
    def _():
        sfin_ref[0] = s_ref[...]


def _rw_scan(q, s0, nb, seq, d):
    hd = RW_HEAD
    tc = _tile(seq, 32, V7X_BF16_ROWS)
    n_chunks = seq // tc
    v3 = lambda a: a.reshape(nb, seq, d)
    cidx = lambda dd, c: c + dd * (n_chunks - 1 - 2 * c)
    tok = pl.BlockSpec((nb, tc, d), lambda dd, c: (0, cidx(dd, c), 0))
    tok_d = pl.BlockSpec((1, nb, tc, d), lambda dd, c: (dd, 0, cidx(dd, c), 0))
    stack = lambda a0, a1: jnp.stack([v3(a0), v3(a1)])
    qq = _block_ones(min(d, 256), hd)
    st_spec = pl.BlockSpec((1, nb, hd, d), lambda dd, c: (dd, 0, 0, 0))
    y, sfin = pl.pallas_call(
        functools.partial(_rw_scan_kernel, nb=nb, tc=tc, n_chunks=n_chunks, d=d, hd=hd),
        grid=(2, n_chunks),
        in_specs=[tok, tok_d, tok_d, tok, tok, tok_d, st_spec, pl.BlockSpec(qq.shape, lambda dd, c: (0, 0))],
        out_specs=[tok_d, st_spec],
        out_shape=[jax.ShapeDtypeStruct((2, nb, seq, d), F32), jax.ShapeDtypeStruct((2, nb, hd, d), F32)],
        scratch_shapes=[pltpu.VMEM((nb, hd, d), F32)],
        compiler_params=_cparams(("arbitrary", "arbitrary")), name="rwkv7_scan",
    )(v3(q["r"]), stack(q["w0"], q["w1"]), stack(q["kd0"], q["kd1"]), v3(q["v"]),
      v3(q["na"]), stack(q["b0"], q["b1"]), s0, qq)
    return y, sfin


def _rw_out(lay, y, q, tail, p, w, *, alpha, name):
    d = w["rw_w_out"].shape[0]
    n_rows = lay.nb * lay.seq
    y2 = y.reshape(2, n_rows, d)
    qq = _block_ones(d, RW_HEAD)
    lnx_g = p["rw_lnx_g"].astype(F32).reshape(1, d)
    lnx_b = p["rw_lnx_b"].astype(F32).reshape(1, d)
    yspec = pl.BlockSpec((2, lay.tm, d), lay._ix(lambda i: (0, i, 0)))

    def prologue(y_ref, bonus_ref, g_ref, qq_ref, lg_ref, lb_ref):
        yy = y_ref[0] + y_ref[1]
        inv = 1.0 / RW_HEAD
        mean = _split_dot(yy, qq_ref[...]) * inv
        yc = yy - mean
        var = _split_dot(yc * yc, qq_ref[...]) * inv
        yn = yc * lax.rsqrt(var + RW_GN_EPS) * lg_ref[...] + lb_ref[...]
        return (yn + bonus_ref[...].astype(F32)) * g_ref[...].astype(F32)

    pro = [(y2, yspec), lay.raster(q["bonus"], d), lay.raster(q["g"], d), lay.const(qq), lay.const(lnx_g),
           lay.const(lnx_b)]
    return _out_proj_call(lay, prologue, pro, w["rw_w_out"], None, tail, alpha=alpha, name=name)


def _ssd_scan_kernel(x_ref, b_ref, c_ref, dt_ref, dtb_ref, a_ref, s0_ref, xp_ref, eye_ref, y_ref, sfin_ref, s_ref,
                     *, n_chunks, ng, nr, hp, ns):
    dirn, c = pl.program_id(1), pl.program_id(2)
    cl = x_ref.shape[1]

    @pl.when(c == 0)
    def _():
        s_ref[...] = s0_ref[0, 0]

    incl, _ = _time_masks(dirn, cl)
    tri = incl.astype(F32)
    eye = eye_ref[...]
    xp = xp_ref[...]
    dt = _softplus(dt_ref[0] + dtb_ref[0])
    dta = dt * a_ref[0]
    cum = _hdot(tri, dta)
    tot = jnp.sum(dta, axis=0, keepdims=True)
    cum_t = _transpose_via_eye(cum, eye)
    dt_t = _transpose_via_eye(dt, eye)
    e_in = _split_dot(jnp.exp(cum), xp)
    e_end = _split_dot(jnp.exp(tot - cum) * dt, xp)
    e_tot = _split_dot(jnp.broadcast_to(jnp.exp(tot), (V7X_SUBLANES, tot.shape[1])), xp)[0:1]
    x = x_ref[0]
    xw = (x.astype(F32) * e_end).astype(BF16)
    gw = nr * hp
    pair = 2 * hp
    lane = lax.broadcasted_iota(jnp.int32, (cl, pair), 1)
    neg = jnp.float32(-jnp.inf)
    for g in range(ng):
        bm = b_ref[0, :, g * ns:(g + 1) * ns]
        cm = c_ref[0, :, g * ns:(g + 1) * ns]
        cb = lax.dot_general(cm, bm, NT_DIMS, preferred_element_type=F32)
        s_g = s_ref[g]
        y_inter = jnp.dot(cm, s_g.astype(BF16), preferred_element_type=F32) * e_in[:, g * gw:(g + 1) * gw]
        for rp in range(nr // 2):
            y_pair = jnp.zeros((cl, pair), F32)
            xpair = x[:, g * gw + rp * pair:g * gw + (rp + 1) * pair]
            for half in range(2):
                h = g * nr + rp * 2 + half
                dec = jnp.exp(jnp.where(incl, cum[:, h:h + 1] - cum_t[h:h + 1, :], neg))
                sc = (cb * dec * dt_t[h:h + 1, :]).astype(BF16)
                xh = jnp.where((lane >= half * hp) & (lane < (half + 1) * hp), xpair, jnp.zeros_like(xpair))
                y_pair = y_pair + jnp.dot(sc, xh, preferred_element_type=F32)
            lo = g * gw + rp * pair
            y_ref[0, 0, :, lo:lo + pair] = y_pair + y_inter[:, rp * pair:(rp + 1) * pair]
        bt = lax.dot_general(eye.astype(BF16), bm, NT_DIMS, preferred_element_type=F32).astype(BF16)
        s_ref[g] = s_g * e_tot[:, g * gw:(g + 1) * gw] + jnp.dot(
            bt, xw[:, g * gw:(g + 1) * gw], preferred_element_type=F32)

    @pl.when(c == n_chunks - 1)
    def _():
        sfin_ref[0, 0] = s_ref[...]


def _ssd_scan(xbc, dt_raw, s0, nb, seq, p):
    ng, ns, hp = SSD_GROUPS, SSD_STATE, SSD_HEAD
    inner = xbc.shape[1] - 2 * ng * ns
    nh = inner // hp
    nr = nh // ng
    lanes = V7X_LANES
    assert ns == lanes and nh <= lanes and nr % 2 == 0
    n_chunks = seq // CHUNK
    cidx = lambda dd, c: c + dd * (n_chunks - 1 - 2 * c)
    xbc3 = xbc.reshape(nb, seq, xbc.shape[1])
    dt3 = dt_raw.reshape(nb, seq, 2 * lanes)
    a = -jnp.exp(p["ssd_A_log"].astype(F32))
    a_pad = _pad_to(a, 1, lanes).reshape(2, 1, lanes)
    dtb = _pad_to(p["ssd_dt_bias"].astype(F32), 1, lanes).reshape(2, 1, lanes)
    xp = _expand_mat(lanes, nh, hp)
    eye = jnp.eye(lanes, dtype=F32)
    bw = ng * ns
    y, sfin = pl.pallas_call(
        functools.partial(_ssd_scan_kernel, n_chunks=n_chunks, ng=ng, nr=nr, hp=hp, ns=ns),
        grid=(nb, 2, n_chunks),
        in_specs=[pl.BlockSpec((1, CHUNK, inner), lambda b, dd, c: (b, cidx(dd, c), 0)),
                  pl.BlockSpec((1, CHUNK, bw), lambda b, dd, c: (b, cidx(dd, c), inner // bw)),
                  pl.BlockSpec((1, CHUNK, bw), lambda b, dd, c: (b, cidx(dd, c), inner // bw + 1)),
                  pl.BlockSpec((1, CHUNK, lanes), lambda b, dd, c: (b, cidx(dd, c), dd)),
                  pl.BlockSpec((1, 1, lanes), lambda b, dd, c: (dd, 0, 0)),
                  pl.BlockSpec((1, 1, lanes), lambda b, dd, c: (dd, 0, 0)),
                  pl.BlockSpec((1, 1, ng, ns, nr * hp), lambda b, dd, c: (b, dd, 0, 0, 0)),
                  pl.BlockSpec(xp.shape, lambda b, dd, c: (0, 0)),
                  pl.BlockSpec(eye.shape, lambda b, dd, c: (0, 0))],
        out_specs=[pl.BlockSpec((1, 1, CHUNK, inner), lambda b, dd, c: (dd, b, cidx(dd, c), 0)),
                   pl.BlockSpec((1, 1, ng, ns, nr * hp), lambda b, dd, c: (b, dd, 0, 0, 0))],
        out_shape=[jax.ShapeDtypeStruct((2, nb, seq, inner), F32),
                   jax.ShapeDtypeStruct((nb, 2, ng, ns, nr * hp), F32)],
        scratch_shapes=[pltpu.VMEM((ng, ns, nr * hp), F32)],
        compiler_params=_cparams(("parallel", "arbitrary", "arbitrary")), name="ssd_scan",
    )(xbc3, xbc3, xbc3, dt3, dtb, a_pad, s0, xp, eye)
    return y.reshape(2, nb * seq, inner), sfin


def _ssd_project(u, nb, seq, p, w, colmajor):
    mm = (lambda x, wt, **kw: _matmul_r2c(x, wt, None, nb, **kw)) if colmajor else (
        lambda x, wt, **kw: _matmul(x, wt, None, **kw))
    z = mm(u, w["ssd_w_z"], out_dtype=BF16, name="ssd_in_proj_z")
    xbc_raw = mm(u, w["ssd_w_xbc"], out_dtype=F32, name="ssd_in_proj_xbc")
    dt_raw = mm(u, w["ssd_w_dt"], out_dtype=F32, name="ssd_in_proj_dt")
    xbc = _conv3(xbc_raw, p["ssd_conv_w"], p["ssd_conv_b"], seq, act=_silu, name="ssd_short_conv")
    return z, xbc, dt_raw


def _ssd_out(lay, y, z, xbc, tail, p, w, *, alpha, name):
    inner = w["ssd_w_out"].shape[0]
    ng = SSD_GROUPS
    gwid = inner // ng
    d_row = jnp.repeat(p["ssd_d_skip"].astype(F32), SSD_HEAD).reshape(1, inner)
    ng_row = p["ssd_norm_g"].astype(F32).reshape(1, inner)

    def prologue(y0_ref, y1_ref, xs_ref, z_ref, d_ref, g_ref):
        flat = lambda ref: ref[...].reshape(-1, ref.shape[-1]).astype(F32)
        yy = flat(xs_ref) * d_ref[...] + flat(y0_ref) + flat(y1_ref)
        yy = yy * _silu(flat(z_ref))
        parts = []
        for g in range(ng):
            yg = yy[:, g * gwid:(g + 1) * gwid]
            ms = jnp.mean(yg * yg, axis=-1, keepdims=True)
            parts.append(yg * lax.rsqrt(ms + 1e-6))
        return jnp.concatenate(parts, axis=1) * g_ref[...]

    pro = [lay.native(y, inner, lead=0), lay.native(y, inner, lead=1), lay.native(xbc, inner),
           lay.native(z, inner), lay.const(d_row), lay.const(ng_row)]
    return _out_proj_call(lay, prologue, pro, w["ssd_w_out"], None, tail, alpha=alpha, name=name)


def _gdn_scan_kernel(q_ref, k_ref, v_ref, a_ref, bta_ref, dtb_ref, al_ref, s0_ref, eye_ref, o_ref, sfin_ref, s_ref,
                     *, n_chunks, hk, hv, dh):
    dirn, c = pl.program_id(1), pl.program_id(2)
    cl = q_ref.shape[1]

    @pl.when(c == 0)
    def _():
        s_ref[...] = s0_ref[0, 0]

    incl, strict = _time_masks(dirn, cl)
    tri = incl.astype(F32)
    eye = eye_ref[...]
    eye_b = eye.astype(BF16)
    neg = jnp.float32(-jnp.inf)
    g = al_ref[0] * _softplus(a_ref[0] + dtb_ref[0])
    beta = jax.nn.sigmoid(bta_ref[0])
    gn = _hdot(tri, g)
    gtot = jnp.sum(g, axis=0, keepdims=True)
    gn_t = _transpose_via_eye(gn, eye)
    e_in = jnp.exp(gn)
    e_end = jnp.exp(gtot - gn)
    e_tot = jnp.exp(gtot)
    rep = hv // hk
    ident = (lax.broadcasted_iota(jnp.int32, (cl, cl), 0) == lax.broadcasted_iota(jnp.int32, (cl, cl), 1)).astype(F32)
    n_sq = max(int(math.ceil(math.log2(cl))) - 1, 0)
    for hq in range(hk):
        qh = q_ref[0, :, hq * dh:(hq + 1) * dh].astype(F32)
        kh = k_ref[0, :, hq * dh:(hq + 1) * dh].astype(F32)
        qn = qh * lax.rsqrt(jnp.sum(qh * qh, axis=-1, keepdims=True) + 1e-6) * (dh ** -0.5)
        kn = kh * lax.rsqrt(jnp.sum(kh * kh, axis=-1, keepdims=True) + 1e-6)
        kn_b = kn.astype(BF16)
        for rr in range(rep):
            h = hq * rep + rr
            diff = gn[:, h:h + 1] - gn_t[h:h + 1, :]
            seg_s = jnp.exp(jnp.where(strict, diff, neg))
            seg_i = jnp.exp(jnp.where(incl, diff, neg))
            bh = beta[:, h:h + 1]
            kb = kn * bh
            gram = lax.dot_general(jnp.concatenate([kb, qn], axis=0).astype(BF16), kn_b, NT_DIMS,
                                   preferred_element_type=F32)
            nmat = -(gram[:cl] * seg_s)
            attn = gram[cl:] * seg_i
            minv = ident + nmat
            pw = nmat
            for _ in range(n_sq):
                pw = _hdot(pw, pw)
                minv = minv + _hdot(minv, pw)
            vh = v_ref[0, :, h * dh:(h + 1) * dh].astype(F32)
            rhs = jnp.concatenate([vh * bh, kb * e_in[:, h:h + 1]], axis=1)
            sol = _hdot(minv, rhs)
            s_h = s_ref[h]
            s_b = s_h.astype(BF16)
            u = sol[:, :dh] - jnp.dot(sol[:, dh:].astype(BF16), s_b, preferred_element_type=F32)
            u_b = u.astype(BF16)
            o = (jnp.dot((qn * e_in[:, h:h + 1]).astype(BF16), s_b, preferred_element_type=F32)
                 + jnp.dot(attn.astype(BF16), u_b, preferred_element_type=F32))
            o_ref[0, 0, :, h * dh:(h + 1) * dh] = o
            kd_t = lax.dot_general(eye_b, (kn * e_end[:, h:h + 1]).astype(BF16), NT_DIMS,
                                   preferred_element_type=F32).astype(BF16)
            s_ref[h] = s_h * e_tot[:, h:h + 1] + jnp.dot(kd_t, u_b, preferred_element_type=F32)

    @pl.when(c == n_chunks - 1)
    def _():
        sfin_ref[0, 0] = s_ref[...]


def _gdn_scan(qkv, ab, s0, nb, seq, p):
    dh = GDN_HEAD
    lanes = V7X_LANES
    hv = p["gdn_dt_bias"].shape[1]
    vw = hv * dh
    qk = (qkv.shape[1] - vw) // 2
    hk = qk // dh
    assert dh == lanes and hv <= lanes
    n_chunks = seq // CHUNK
    cidx = lambda dd, c: c + dd * (n_chunks - 1 - 2 * c)
    qkv3 = qkv.reshape(nb, seq, qkv.shape[1])
    ab3 = ab.reshape(nb, seq, 4 * lanes)
    dtb = _pad_to(p["gdn_dt_bias"].astype(F32), 1, lanes).reshape(2, 1, lanes)
    al = _pad_to(-jnp.exp(p["gdn_A_log"].astype(F32)), 1, lanes).reshape(2, 1, lanes)
    eye = jnp.eye(lanes, dtype=F32)
    o, sfin = pl.pallas_call(
        functools.partial(_gdn_scan_kernel, n_chunks=n_chunks, hk=hk, hv=hv, dh=dh),
        grid=(nb, 2, n_chunks),
        in_specs=[pl.BlockSpec((1, CHUNK, qk), lambda b, dd, c: (b, cidx(dd, c), 0)),
                  pl.BlockSpec((1, CHUNK, qk), lambda b, dd, c: (b, cidx(dd, c), 1)),
                  pl.BlockSpec((1, CHUNK, vw), lambda b, dd, c: (b, cidx(dd, c), 2 * qk // vw)),
                  pl.BlockSpec((1, CHUNK, lanes), lambda b, dd, c: (b, cidx(dd, c), 2 * dd)),
                  pl.BlockSpec((1, CHUNK, lanes), lambda b, dd, c: (b, cidx(dd, c), 2 * dd + 1)),
                  pl.BlockSpec((1, 1, lanes), lambda b, dd, c: (dd, 0, 0)),
                  pl.BlockSpec((1, 1, lanes), lambda b, dd, c: (dd, 0, 0)),
                  pl.BlockSpec((1, 1, hv, dh, dh), lambda b, dd, c: (b, dd, 0, 0, 0)),
                  pl.BlockSpec(eye.shape, lambda b, dd, c: (0, 0))],
        out_specs=[pl.BlockSpec((1, 1, CHUNK, vw), lambda b, dd, c: (dd, b, cidx(dd, c), 0)),
                   pl.BlockSpec((1, 1, hv, dh, dh), lambda b, dd, c: (b, dd, 0, 0, 0))],
        out_shape=[jax.ShapeDtypeStruct((2, nb, seq, vw), F32),
                   jax.ShapeDtypeStruct((nb, 2, hv, dh, dh), F32)],
        scratch_shapes=[pltpu.VMEM((hv, dh, dh), F32)],
        compiler_params=_cparams(("parallel", "arbitrary", "arbitrary")), name="gdn_scan",
    )(qkv3, qkv3, qkv3, ab3, ab3, dtb, al, s0, eye)
    return o.reshape(2, nb * seq, vw), sfin


def _gdn_project(u, nb, seq, p, w, colmajor):
    mm = (lambda x, wt, **kw: _matmul_r2c(x, wt, None, nb, **kw)) if colmajor else (
        lambda x, wt, **kw: _matmul(x, wt, None, **kw))
    qkv_raw = mm(u, w["gdn_w_qkv"], out_dtype=F32, name="gdn_in_proj_qkv")
    z = mm(u, w["gdn_w_z"], out_dtype=BF16, name="gdn_in_proj_z")
    ab = mm(u, w["gdn_w_ab"], out_dtype=F32, name="gdn_in_proj_ab")
    qkv = _conv3(qkv_raw, p["gdn_conv_w"], None, seq, act=_silu, name="gdn_short_conv")
    return qkv, z, ab


def _gdn_out(lay, o, z, tail, p, w, *, alpha, name):
    vw = w["gdn_w_out"].shape[0]
    dh = GDN_HEAD
    hv = vw // dh
    g_row = jnp.tile(p["gdn_norm_g"].astype(F32), hv).reshape(1, vw)

    def prologue(o0_ref, o1_ref, z_ref, g_ref):
        flat = lambda ref: ref[...].reshape(-1, ref.shape[-1]).astype(F32)
        oo = flat(o0_ref) + flat(o1_ref)
        parts = []
        for h in range(hv):
            oh = oo[:, h * dh:(h + 1) * dh]
            ms = jnp.mean(oh * oh, axis=-1, keepdims=True)
            parts.append(oh * lax.rsqrt(ms + 1e-6))
        return jnp.concatenate(parts, axis=1) * g_ref[...] * _silu(flat(z_ref))

    pro = [lay.native(o, vw, lead=0), lay.native(o, vw, lead=1), lay.native(z, vw), lay.const(g_row)]
    return _out_proj_call(lay, prologue, pro, w["gdn_w_out"], None, tail, alpha=alpha, name=name)


def _prep_weights(p):
    lanes = V7X_LANES
    bf = lambda a: a.astype(BF16)
    d = p["hy_w_out"].shape[0]
    w = {k: bf(p[k]) for k in ("hy_w_in", "hy_w_out", "rw_w_rkv", "rw_w_out", "ssd_w_out", "gdn_w_out",
                               "ffn_w13", "ffn_w2", "moe_w13", "moe_w2", "mod_w")}
    lora = lambda a, ax: bf(_pad_to(a, ax, lanes * ((a.shape[ax] + lanes - 1) // lanes)))
    w["rw_w1"], w["rw_w2"] = lora(p["rw_w1"], 2), lora(p["rw_w2"], 1)
    w["rw_a1"], w["rw_a2"] = lora(p["rw_a1"], 2), lora(p["rw_a2"], 1)
    w["rw_g1"], w["rw_g2"] = lora(p["rw_g1"], 1), lora(p["rw_g2"], 0)
    inner = p["ssd_w_out"].shape[0]
    nbc = 2 * SSD_GROUPS * SSD_STATE
    nh = inner // SSD_HEAD
    ws = p["ssd_w_in"]
    w["ssd_w_z"] = bf(ws[:, :inner])
    w["ssd_w_xbc"] = bf(ws[:, inner:2 * inner + nbc])
    wdt = ws[:, 2 * inner + nbc:].reshape(d, 2, nh)
    w["ssd_w_dt"] = bf(_pad_to(wdt, 2, lanes).reshape(d, 2 * lanes))
    vw = p["gdn_w_out"].shape[0]
    hv = vw // GDN_HEAD
    qkvw = p["gdn_conv_w"].shape[1]
    wg = p["gdn_w_in"]
    w["gdn_w_qkv"] = bf(wg[:, :qkvw])
    w["gdn_w_z"] = bf(wg[:, qkvw:qkvw + vw])
    wab = wg[:, qkvw + vw:].reshape(d, 2, 2, hv)
    wab = jnp.transpose(wab, (0, 2, 1, 3))
    w["gdn_w_ab"] = bf(_pad_to(wab, 3, lanes).reshape(d, 4 * lanes))
    w["moe_router"] = _pad_to(p["moe_router"].astype(F32), 2, lanes)
    return w


_PARAM_NAMES = (
    "mod_w mod_b ln_g ln_b hy_w_in hy_b_in hy_conv_w hy_conv_b hy_f_w1 hy_f_b1 hy_f_w2 hy_f_b2 hy_f_freq "
    "hy_f_w3 hy_decay hy_skip hy_w_out hy_b_out rw_mu rw_w_rkv rw_w0 rw_w1 rw_w2 rw_a0 rw_a1 rw_a2 rw_g1 rw_g2 "
    "rw_k_k rw_k_a rw_r_k rw_lnx_g rw_lnx_b rw_w_out ssd_w_in ssd_conv_w ssd_conv_b ssd_dt_bias ssd_A_log "
    "ssd_d_skip ssd_norm_g ssd_w_out gdn_w_in gdn_conv_w gdn_dt_bias gdn_A_log gdn_norm_g gdn_w_out ffn_w13 "
    "ffn_w2 moe_router moe_w13 moe_w2").split()


def _forward(x, c, ctx, c_ctx, p):
    nb, seq, d = x.shape
    lc = ctx.shape[1]
    depth = p["mod_w"].shape[0]
    alpha = (2 * depth) ** 0.25
    n_exp = p["moe_router"].shape[2]
    assert seq == GRID_W * GRID_W and seq % CHUNK == 0 and lc % CHUNK == 0
    w = _prep_weights(p)

    cc = _pad_to(jnp.concatenate([c, c_ctx[None, :]], axis=0).astype(F32), 0, V7X_BF16_ROWS)
    mods = [
        _matmul(cc, w["mod_w"][i], p["mod_b"][i], pre=_silu, out_dtype=F32, name="adaln_modulation")
        .reshape(cc.shape[0], 6, d) for i in range(depth)]

    def chunk(i, k, stream):
        m = mods[i][:, k]
        return m[:nb, None, :] if stream == "x" else m[nb:nb + 1, None, :]

    lay = {"x": _RowLayout(nb, seq, True), "c": _RowLayout(nb, lc, False),
           "xcol": _RowLayout(nb, seq, True, mode="cols")}
    seqs = {"x": seq, "c": lc}
    h = {"x": x.reshape(nb * seq, d).astype(F32), "c": ctx.reshape(nb * lc, d).astype(F32)}

    u = {s: _modulate(lay[s], h[s], chunk(0, 1, s), chunk(0, 0, s)) for s in ("x", "c")}

    for i in range(depth):
        last = i == depth - 1
        kind = i % 4
        moe = i % 2 == 1
        streams = ("x",) if last else ("x", "c")

        def tail1(s):
            t = dict(h=h[s], gate=chunk(i, 2, s), ln_g=p["ln_g"][i, 0], ln_b=p["ln_b"][i, 0],
                     scale=chunk(i, 4, s), shift=chunk(i, 3, s))
            if moe:
                t.update(router=w["moe_router"][i // 2], n_exp=n_exp)
            return t

        res = {}
        if kind == 0:
            for s in streams:
                z = _hyena_mixer(u[s], nb, seqs[s], p, w)
                res[s] = _hyena_out(lay[s], z, tail1(s), p, w, nb, seqs[s], alpha=alpha, name=f"hyena_out_{s}")
        elif kind == 1:
            qc = _rw_project(u["c"], lc, p, w)
            qx = _rw_project(u["x"], seq, p, w)
            s0 = jnp.zeros((2, nb, RW_HEAD, d), F32)
            yc, s_c = _rw_scan(qc, s0, nb, lc, d)
            yx, _ = _rw_scan(qx, s_c, nb, seq, d)
            ys, qs = {"x": yx, "c": yc}, {"x": qx, "c": qc}
            for s in streams:
                res[s] = _rw_out(lay[s], ys[s], qs[s], tail1(s), p, w, alpha=alpha, name=f"rwkv7_out_{s}")
        elif kind == 2:
            zc, xbc_c, dt_c = _ssd_project(u["c"], nb, lc, p, w, False)
            zx, xbc_x, dt_x = _ssd_project(u["x"], nb, seq, p, w, True)
            inner = w["ssd_w_out"].shape[0]
            s0 = jnp.zeros((nb, 2, SSD_GROUPS, SSD_STATE, inner // SSD_GROUPS), F32)
            yc, s_c = _ssd_scan(xbc_c, dt_c, s0, nb, lc, p)
            yx, _ = _ssd_scan(xbc_x, dt_x, s_c, nb, seq, p)
            res["x"] = _ssd_out(lay["xcol"], yx, zx, xbc_x, tail1("x"), p, w, alpha=alpha, name="ssd_out_x")
            if not last:
                res["c"] = _ssd_out(lay["c"], yc, zc, xbc_c, tail1("c"), p, w, alpha=alpha, name="ssd_out_c")
        else:
            qkv_c, zc, ab_c = _gdn_project(u["c"], nb, lc, p, w, False)
            qkv_x, zx, ab_x = _gdn_project(u["x"], nb, seq, p, w, True)
            vw = w["gdn_w_out"].shape[0]
            s0 = jnp.zeros((nb, 2, vw // GDN_HEAD, GDN_HEAD, GDN_HEAD), F32)
            oc, s_c = _gdn_scan(qkv_c, ab_c, s0, nb, lc, p)
            ox, _ = _gdn_scan(qkv_x, ab_x, s_c, nb, seq, p)
            res["x"] = _gdn_out(lay["xcol"], ox, zx, tail1("x"), p, w, alpha=alpha, name="gdn_out_x")
            if not last:
                res["c"] = _gdn_out(lay["c"], oc, zc, tail1("c"), p, w, alpha=alpha, name="gdn_out_c")

        for s in streams:
            t = dict(h=res[s]["h"], gate=chunk(i, 5, s), ln_g=p["ln_g"][i, 1], ln_b=p["ln_b"][i, 1])
            if not last:
                t.update(scale=chunk(i + 1, 1, s), shift=chunk(i + 1, 0, s))
            if moe:
                out = _ffn_call(lay[s], res[s]["u"], w["moe_w13"][i // 2], w["moe_w2"][i // 2], res[s]["comb"], t,
                                alpha=alpha, name=f"moe_ffn_{s}")
            else:
                out = _ffn_call(lay[s], res[s]["u"], w["ffn_w13"][i // 2][None], w["ffn_w2"][i // 2][None], None, t,
                                alpha=alpha, name=f"dense_ffn_{s}")
            h[s] = out["h"]
            if not last:
                u[s] = out["u"]
    return h["x"].reshape(nb, seq, d).astype(x.dtype)


def kernel(x, c, ctx, c_ctx, mod_w, mod_b, ln_g, ln_b, hy_w_in, hy_b_in, hy_conv_w, hy_conv_b, hy_f_w1, hy_f_b1, hy_f_w2, hy_f_b2, hy_f_freq, hy_f_w3, hy_decay, hy_skip, hy_w_out, hy_b_out, rw_mu, rw_w_rkv, rw_w0, rw_w1, rw_w2, rw_a0, rw_a1, rw_a2, rw_g1, rw_g2, rw_k_k, rw_k_a, rw_r_k, rw_lnx_g, rw_lnx_b, rw_w_out, ssd_w_in, ssd_conv_w, ssd_conv_b, ssd_dt_bias, ssd_A_log, ssd_d_skip, ssd_norm_g, ssd_w_out, gdn_w_in, gdn_conv_w, gdn_dt_bias, gdn_A_log, gdn_norm_g, gdn_w_out, ffn_w13, ffn_w2, moe_router, moe_w13, moe_w2):
    vals = (mod_w, mod_b, ln_g, ln_b, hy_w_in, hy_b_in, hy_conv_w, hy_conv_b, hy_f_w1, hy_f_b1, hy_f_w2, hy_f_b2,
            hy_f_freq, hy_f_w3, hy_decay, hy_skip, hy_w_out, hy_b_out, rw_mu, rw_w_rkv, rw_w0, rw_w1, rw_w2, rw_a0,
            rw_a1, rw_a2, rw_g1, rw_g2, rw_k_k, rw_k_a, rw_r_k, rw_lnx_g, rw_lnx_b, rw_w_out, ssd_w_in, ssd_conv_w,
            ssd_conv_b, ssd_dt_bias, ssd_A_log, ssd_d_skip, ssd_norm_g, ssd_w_out, gdn_w_in, gdn_conv_w,
            gdn_dt_bias, gdn_A_log, gdn_norm_g, gdn_w_out, ffn_w13, ffn_w2, moe_router, moe_w13, moe_w2)
    return _forward(x, c, ctx, c_ctx, dict(zip(_PARAM_NAMES, vals)))
```

```python
import functools
import math

import jax
import jax.numpy as jnp
import numpy as np
from jax import lax
from jax.experimental import pallas as pl
from jax.experimental.pallas import tpu as pltpu

F32 = jnp.float32
BF16 = jnp.bfloat16
HIGHEST = lax.Precision.HIGHEST

GRID_W = 64
LN_EPS = 1e-5
HY_BANDS = 8
RW_HEAD = 64
RW_GN_EPS = 64e-5
SSD_HEAD = 64
SSD_STATE = 128
SSD_GROUPS = 4
GDN_HEAD = 128
CHUNK = 64
TOP_K = 2

V7X_LANES = 128
V7X_SUBLANES = 8
V7X_BF16_ROWS = 16
V7X_VMEM_BUDGET = 56 * 1024 * 1024

NT_DIMS = (((1,), (1,)), ((), ()))
TN_DIMS = (((0,), (0,)), ((), ()))


def _tile(n, pref, mult):
    best = None
    t = mult
    while t <= min(n, pref):
        if n % t == 0:
            best = t
        t += mult
    return best if best is not None else n


def _pad_to(a, axis, size):
    pad = size - a.shape[axis]
    if pad == 0:
        return a
    cfg = [(0, 0)] * a.ndim
    cfg[axis] = (0, pad)
    return jnp.pad(a, cfg)


def _cparams(sem):
    return pltpu.CompilerParams(dimension_semantics=sem, vmem_limit_bytes=V7X_VMEM_BUDGET)


def _silu(x):
    return x * jax.nn.sigmoid(x)


def _softplus(x):
    return jnp.maximum(x, 0.0) + jnp.log1p(jnp.exp(-jnp.abs(x)))


def _bdot(a, b):
    return jnp.dot(a.astype(BF16), b.astype(BF16), preferred_element_type=F32)


def _bdot_nt(a, b):
    return lax.dot_general(a.astype(BF16), b.astype(BF16), NT_DIMS, preferred_element_type=F32)


def _hdot(a, b):
    return jnp.dot(a, b, precision=HIGHEST, preferred_element_type=F32)


def _split_dot(x, m):
    hi = x.astype(BF16)
    lo = (x - hi.astype(F32)).astype(BF16)
    return jnp.dot(hi, m, preferred_element_type=F32) + jnp.dot(lo, m, preferred_element_type=F32)


def _transpose_via_eye(x, eye):
    return lax.dot_general(eye, x, NT_DIMS, precision=HIGHEST, preferred_element_type=F32)


def _block_ones(n, blk):
    i = np.arange(n)
    return jnp.asarray((i[:, None] // blk == i[None, :] // blk).astype(np.float32), BF16)


def _expand_mat(n_in_pad, n_heads, width):
    m = np.zeros((n_in_pad, n_heads * width), np.float32)
    for h in range(n_heads):
        m[h, h * width:(h + 1) * width] = 1.0
    return jnp.asarray(m, BF16)


def _perm_mat(ib, jb):
    n = ib * jb
    q = np.arange(n)
    p = (q % ib) * jb + q // ib
    m = np.zeros((n, n), np.float32)
    m[q, p] = 1.0
    return jnp.asarray(m, BF16)


def _time_masks(d, c):
    ii = lax.broadcasted_iota(jnp.int32, (c, c), 0)
    jj = lax.broadcasted_iota(jnp.int32, (c, c), 1)
    lag = (ii - jj) * (1 - 2 * d)
    return lag >= 0, lag > 0


def _mm_kernel(*refs, nk, pre, act, has_bias, has_perm):
    it = iter(refs)
    x_ref, w_ref = next(it), next(it)
    b_ref = next(it) if has_bias else None
    p_ref = next(it) if has_perm else None
    o_ref = next(it)
    acc_ref = next(it) if nk > 1 else None

    x = x_ref[...]
    x = x.reshape(-1, x.shape[-1])
    if pre is not None:
        x = pre(x.astype(F32))
    x = x.astype(BF16)
    if has_perm:
        x = jnp.dot(p_ref[...], x, preferred_element_type=F32).astype(BF16)
    part = jnp.dot(x, w_ref[...], preferred_element_type=F32)

    def finish(r):
        if has_bias:
            r = r + b_ref[...]
        if act is not None:
            r = act(r)
        o_ref[...] = r.astype(o_ref.dtype).reshape(o_ref.shape)

    if nk == 1:
        finish(part)
    else:
        k = pl.program_id(2)

        @pl.when(k == 0)
        def _():
            acc_ref[...] = jnp.zeros_like(acc_ref)

        acc_ref[...] += part

        @pl.when(k == nk - 1)
        def _():
            finish(acc_ref[...])


def _matmul(x, w, bias=None, *, out_dtype=F32, act=None, pre=None, tm=512, tn=512, tk=None, name="matmul"):
    m, kdim = x.shape
    n = w.shape[1]
    tm = _tile(m, tm, V7X_BF16_ROWS)
    tn = _tile(n, tn, V7X_LANES)
    tk = kdim if tk is None else _tile(kdim, tk, V7X_LANES)
    nk = kdim // tk
    in_specs = [pl.BlockSpec((tm, tk), lambda i, j, k: (i, k)),
                pl.BlockSpec((tk, tn), lambda i, j, k: (k, j))]
    args = [x, w]
    if bias is not None:
        in_specs.append(pl.BlockSpec((1, tn), lambda i, j, k: (0, j)))
        args.append(bias.reshape(1, n).astype(F32))
    return pl.pallas_call(
        functools.partial(_mm_kernel, nk=nk, pre=pre, act=act, has_bias=bias is not None, has_perm=False),
        grid=(m // tm, n // tn, nk),
        in_specs=in_specs,
        out_specs=pl.BlockSpec((tm, tn), lambda i, j, k: (i, j)),
        out_shape=jax.ShapeDtypeStruct((m, n), out_dtype),
        scratch_shapes=[pltpu.VMEM((tm, tn), F32)] if nk > 1 else [],
        compiler_params=_cparams(("parallel", "parallel", "arbitrary")),
        name=name,
    )(*args)


def _col_tiles(gw):
    ib = min(V7X_BF16_ROWS, gw)
    jb = min(32, gw)
    return ib, jb


def _matmul_r2c(u, w, bias, nb, *, out_dtype, tn=512, name="matmul_r2c"):
    gw = GRID_W
    kdim, n = w.shape
    ib, jb = _col_tiles(gw)
    tn = _tile(n, tn, V7X_LANES)
    u4 = u.reshape(nb, gw, gw, kdim)
    in_specs = [pl.BlockSpec((1, ib, jb, kdim), lambda b, i, j, c: (b, i, j, 0)),
                pl.BlockSpec((kdim, tn), lambda b, i, j, c: (0, c))]
    args = [u4, w]
    if bias is not None:
        in_specs.append(pl.BlockSpec((1, tn), lambda b, i, j, c: (0, c)))
        args.append(bias.reshape(1, n).astype(F32))
    in_specs.append(pl.BlockSpec((ib * jb, ib * jb), lambda b, i, j, c: (0, 0)))
    args.append(_perm_mat(ib, jb))
    out = pl.pallas_call(
        functools.partial(_mm_kernel, nk=1, pre=None, act=None, has_bias=bias is not None, has_perm=True),
        grid=(nb, gw // ib, gw // jb, n // tn),
        in_specs=in_specs,
        out_specs=pl.BlockSpec((1, jb, ib, tn), lambda b, i, j, c: (b, j, i, c)),
        out_shape=jax.ShapeDtypeStruct((nb, gw, gw, n), out_dtype),
        compiler_params=_cparams(("parallel", "parallel", "parallel", "arbitrary")),
        name=name,
    )(*args)
    return out.reshape(nb * gw * gw, n)


def _ln_epilogue(h, y, gate, ln_g, ln_b, alpha):
    pre = alpha * h + gate * y
    mu = jnp.mean(pre, axis=-1, keepdims=True)
    xc = pre - mu
    var = jnp.mean(xc * xc, axis=-1, keepdims=True)
    return xc * lax.rsqrt(var + LN_EPS) * ln_g + ln_b


def _route(logits, n_exp):
    lane = lax.broadcasted_iota(jnp.int32, logits.shape, 1)
    neg = jnp.float32(-jnp.inf)
    lg = jnp.where(lane < n_exp, logits, neg)
    big = jnp.int32(logits.shape[1])
    m1 = jnp.max(lg, axis=-1, keepdims=True)
    i1 = jnp.min(jnp.where(lg == m1, lane, big), axis=-1, keepdims=True)
    lg2 = jnp.where(lane == i1, neg, lg)
    m2 = jnp.max(lg2, axis=-1, keepdims=True)
    i2 = jnp.min(jnp.where(lg2 == m2, lane, big), axis=-1, keepdims=True)
    e2 = jnp.exp(m2 - m1)
    den = 1.0 + e2
    return jnp.where(lane == i1, 1.0 / den, jnp.where(lane == i2, e2 / den, 0.0))


def _finish_rows(refs, y, *, alpha, emit_u, n_exp):
    h_ref, gate_ref, lng_ref, lnb_ref = refs["h"], refs["gate"], refs["ln_g"], refs["ln_b"]
    h = h_ref[...].reshape(y.shape)
    hn = _ln_epilogue(h, y, gate_ref[0], lng_ref[...], lnb_ref[...], alpha)
    refs["h_out"][...] = hn.reshape(refs["h_out"].shape)
    if emit_u:
        u = hn * (1.0 + refs["scale"][0]) + refs["shift"][0]
        refs["u_out"][...] = u.astype(BF16).reshape(refs["u_out"].shape)
        if n_exp:
            logits = _hdot(u, refs["router"][...])
            refs["comb_out"][...] = _route(logits, n_exp).reshape(refs["comb_out"].shape)


class _RowLayout:
    def __init__(self, nb, seq, per_batch_mod, mode="rows", tm=512):
        self.nb, self.seq, self.mode, self.per_batch_mod = nb, seq, mode, per_batch_mod
        if mode == "rows":
            self.tm = _tile(seq, tm, V7X_BF16_ROWS)
            self.tps = seq // self.tm
            self.grid = (nb * self.tps,)
            self.rows = self.tm
        else:
            self.ib, self.jb = _col_tiles(GRID_W)
            self.grid = (nb, GRID_W // self.ib, GRID_W // self.jb)
            self.rows = self.ib * self.jb
        self.ngrid = len(self.grid)

    def sem(self, extra=()):
        return ("parallel",) * self.ngrid + tuple(extra)

    def _ix(self, fn):
        n = self.ngrid
        return lambda *g: fn(*g[:n])

    def raster(self, arr, c, cblock=0, lead=None):
        pre_shape = () if lead is None else (arr.shape[0],)
        pre_blk = () if lead is None else (1,)
        pre_ix = () if lead is None else (lead,)
        if self.mode == "rows":
            return arr, pl.BlockSpec(pre_blk + (self.tm, c), self._ix(lambda i: pre_ix + (i, cblock)))
        a4 = arr.reshape(pre_shape + (self.nb, GRID_W, GRID_W, arr.shape[-1]))
        return a4, pl.BlockSpec(pre_blk + (1, self.ib, self.jb, c),
                                self._ix(lambda b, i, j: pre_ix + (b, i, j, cblock)))

    def colmajor(self, arr, c, cblock=0, lead=None):
        assert self.mode == "cols"
        pre_shape = () if lead is None else (arr.shape[0],)
        pre_blk = () if lead is None else (1,)
        pre_ix = () if lead is None else (lead,)
        a4 = arr.reshape(pre_shape + (self.nb, GRID_W, GRID_W, arr.shape[-1]))
        return a4, pl.BlockSpec(pre_blk + (1, self.jb, self.ib, c),
                                self._ix(lambda b, i, j: pre_ix + (b, j, i, cblock)))

    def native(self, arr, c, cblock=0, lead=None):
        if self.mode == "cols":
            return self.colmajor(arr, c, cblock, lead)
        return self.raster(arr, c, cblock, lead)

    def mod(self, arr):
        d = arr.shape[-1]
        if not self.per_batch_mod:
            return arr, pl.BlockSpec((1, 1, d), self._ix(lambda *g: (0, 0, 0)))
        if self.mode == "rows":
            tps = self.tps
            return arr, pl.BlockSpec((1, 1, d), self._ix(lambda i: (i // tps, 0, 0)))
        return arr, pl.BlockSpec((1, 1, d), self._ix(lambda b, i, j: (b, 0, 0)))

    def const(self, arr):
        nd = arr.ndim
        return arr, pl.BlockSpec(arr.shape, self._ix(lambda *g: (0,) * nd))

    def out_raster(self, n_rows, c, dtype):
        if self.mode == "rows":
            return (jax.ShapeDtypeStruct((n_rows, c), dtype),
                    pl.BlockSpec((self.tm, c), self._ix(lambda i: (i, 0))))
        return (jax.ShapeDtypeStruct((self.nb, GRID_W, GRID_W, c), dtype),
                pl.BlockSpec((1, self.ib, self.jb, c), self._ix(lambda b, i, j: (b, i, j, 0))))


def _out_proj_call(lay, prologue, pro_inputs, w_out, bias, tail, *, alpha, name):
    d = w_out.shape[1]
    n_rows = lay.nb * lay.seq
    emit_u = "scale" in tail
    n_exp = tail.get("n_exp", 0)
    names, args, specs = [], [], []

    def add(nm, pair):
        names.append(nm)
        args.append(pair[0])
        specs.append(pair[1])

    for k, pair in enumerate(pro_inputs):
        add(f"p{k}", pair)
    add("w", lay.const(w_out))
    if bias is not None:
        add("bias", lay.const(bias.reshape(1, d).astype(F32)))
    if lay.mode == "cols":
        add("perm", lay.const(_perm_mat(lay.ib, lay.jb).T))
    add("h", lay.raster(tail["h"], d))
    add("gate", lay.mod(tail["gate"]))
    add("ln_g", lay.const(tail["ln_g"].reshape(1, d)))
    add("ln_b", lay.const(tail["ln_b"].reshape(1, d)))
    if emit_u:
        add("scale", lay.mod(tail["scale"]))
        add("shift", lay.mod(tail["shift"]))
    if n_exp:
        add("router", lay.const(tail["router"]))
    out_names, out_shapes, out_specs = [], [], []

    def add_out(nm, pair):
        out_names.append(nm)
        out_shapes.append(pair[0])
        out_specs.append(pair[1])

    add_out("h_out", lay.out_raster(n_rows, d, F32))
    if emit_u:
        add_out("u_out", lay.out_raster(n_rows, d, BF16))
    if n_exp:
        add_out("comb_out", lay.out_raster(n_rows, V7X_LANES, F32))
    n_pro = len(pro_inputs)
    n_in = len(names)

    def kernel(*refs):
        r = dict(zip(names + out_names, refs))
        z = prologue(*[refs[k] for k in range(n_pro)])
        z = z.astype(BF16)
        if lay.mode == "cols":
            z = jnp.dot(r["perm"][...], z, preferred_element_type=F32).astype(BF16)
        y = jnp.dot(z, r["w"][...], preferred_element_type=F32)
        if bias is not None:
            y = y + r["bias"][...]
        _finish_rows(r, y, alpha=alpha, emit_u=emit_u, n_exp=n_exp)

    outs = pl.pallas_call(
        kernel, grid=lay.grid, in_specs=specs, out_specs=out_specs, out_shape=out_shapes,
        compiler_params=_cparams(lay.sem()), name=name)(*args)
    outs = [o.reshape(n_rows, o.shape[-1]) for o in outs]
    res = {"h": outs[0]}
    if emit_u:
        res["u"] = outs[1]
    if n_exp:
        res["comb"] = outs[2]
    return res


def _modulate_kernel(h_ref, scale_ref, shift_ref, u_ref):
    u_ref[...] = (h_ref[...] * (1.0 + scale_ref[0]) + shift_ref[0]).astype(u_ref.dtype)


def _modulate(lay, h, scale, shift):
    n_rows, d = h.shape
    pairs = [lay.raster(h, d), lay.mod(scale), lay.mod(shift)]
    out = lay.out_raster(n_rows, d, BF16)
    return pl.pallas_call(
        _modulate_kernel, grid=lay.grid, in_specs=[pr[1] for pr in pairs], out_specs=out[1], out_shape=out[0],
        compiler_params=_cparams(lay.sem()), name="adaln_modulate")(*[pr[0] for pr in pairs])


def _ffn_kernel(*refs, names, n_steps, alpha, emit_u, use_comb, n_e, n_f):
    r = dict(zip(names, refs))
    e, f = pl.program_id(1), pl.program_id(2)
    step = e * n_f + f

    @pl.when(step == 0)
    def _():
        r["acc"][...] = jnp.zeros_like(r["acc"])

    u = r["u"][...]
    gate = jnp.dot(u, r["w1"][0], preferred_element_type=F32)
    up = jnp.dot(u, r["w3"][0], preferred_element_type=F32)
    hid = _silu(gate) * up
    if use_comb:
        comb = r["comb"][...]
        lane = lax.broadcasted_iota(jnp.int32, comb.shape, 1)
        ce = jnp.sum(jnp.where(lane == e, comb, 0.0), axis=-1, keepdims=True)
        hid = hid * ce
    r["acc"][...] += jnp.dot(hid.astype(BF16), r["w2"][0], preferred_element_type=F32)

    @pl.when(step == n_steps - 1)
    def _():
        _finish_rows(r, r["acc"][...], alpha=alpha, emit_u=emit_u, n_exp=0)


def _ffn_call(lay, u, w13, w2, comb, tail, *, alpha, name):
    n_e, d, f2 = w13.shape
    fdim = f2 // 2
    tf = _tile(fdim, 1536, V7X_LANES)
    n_f = fdim // tf
    n_rows = lay.nb * lay.seq
    emit_u = "scale" in tail
    names, args, specs = [], [], []

    def add(nm, pair):
        names.append(nm)
        args.append(pair[0])
        specs.append(pair[1])

    add("u", lay.raster(u, d))
    add("w1", (w13, pl.BlockSpec((1, d, tf), lambda i, e, f: (e, 0, f))))
    add("w3", (w13, pl.BlockSpec((1, d, tf), lambda i, e, f: (e, 0, n_f + f))))
    add("w2", (w2, pl.BlockSpec((1, tf, d), lambda i, e, f: (e, f, 0))))
    if comb is not None:
        add("comb", lay.raster(comb, V7X_LANES))
    add("h", lay.raster(tail["h"], d))
    add("gate", lay.mod(tail["gate"]))
    add("ln_g", lay.const(tail["ln_g"].reshape(1, d)))
    add("ln_b", lay.const(tail["ln_b"].reshape(1, d)))
    if emit_u:
        add("scale", lay.mod(tail["scale"]))
        add("shift", lay.mod(tail["shift"]))
    out_names = ["h_out"] + (["u_out"] if emit_u else [])
    outs = [lay.out_raster(n_rows, d, F32)] + ([lay.out_raster(n_rows, d, BF16)] if emit_u else [])
    kernel = functools.partial(
        _ffn_kernel, names=names + out_names + ["acc"], n_steps=n_e * n_f, alpha=alpha, emit_u=emit_u,
        use_comb=comb is not None, n_e=n_e, n_f=n_f)
    res = pl.pallas_call(
        kernel, grid=lay.grid + (n_e, n_f), in_specs=specs,
        out_specs=[o[1] for o in outs], out_shape=[o[0] for o in outs],
        scratch_shapes=[pltpu.VMEM((lay.rows, d), F32)],
        compiler_params=_cparams(lay.sem(("arbitrary", "arbitrary"))), name=name)(*args)
    out = {"h": res[0]}
    if emit_u:
        out["u"] = res[1]
    return out


def _shift_rows(cur, prev_row, next_row):
    n = cur.shape[0]
    rows = lax.broadcasted_iota(jnp.int32, cur.shape, 0)
    up = jnp.where(rows == 0, prev_row, pltpu.roll(cur, 1, 0))
    dn = jnp.where(rows == n - 1, next_row, pltpu.roll(cur, n - 1, 0))
    return up, dn


def _halo_rows(i, tps, xp_ref, xn_ref, hb):
    t = i % tps
    prev_row = jnp.where(t == 0, 0.0, xp_ref[hb - 1:hb, :].astype(F32))
    next_row = jnp.where(t == tps - 1, 0.0, xn_ref[0:1, :].astype(F32))
    return prev_row, next_row


def _halo_rows_index(tm, n_rows, hb):
    r = tm // hb
    last = n_rows // hb - 1
    return (lambda i: jnp.maximum(i * r - 1, 0)), (lambda i: jnp.minimum((i + 1) * r, last))


def _conv3_kernel(x_ref, xp_ref, xn_ref, w_ref, b_ref, o_ref, *, tps, act):
    i = pl.program_id(0)
    cur = x_ref[...].astype(F32)
    prev_row, next_row = _halo_rows(i, tps, xp_ref, xn_ref, V7X_SUBLANES)
    up, dn = _shift_rows(cur, prev_row, next_row)
    w = w_ref[...]
    y = w[0:1] * up + w[1:2] * cur + w[2:3] * dn + b_ref[...]
    if act is not None:
        y = act(y)
    o_ref[...] = y.astype(o_ref.dtype).reshape(o_ref.shape)


def _conv3(x, w, b, seq, *, act=None, out_dtype=BF16, time_major_nb=None, name="conv3"):
    n_rows, c = x.shape
    tm = _tile(seq, 512, V7X_SUBLANES)
    tps = seq // tm
    if b is None:
        b = jnp.zeros((c,), F32)
    if time_major_nb is None:
        ct = _tile(c, 1024, V7X_LANES)
        out_shape = jax.ShapeDtypeStruct((n_rows, c), out_dtype)
        out_spec = pl.BlockSpec((tm, ct), lambda i, cc: (i, cc))
    else:
        nb, d = time_major_nb
        ct = d
        out_shape = jax.ShapeDtypeStruct((c // d, seq, nb * d), out_dtype)
        out_spec = pl.BlockSpec((1, tm, d), lambda i, cc: (cc, i % tps, i // tps))
    hb = V7X_SUBLANES
    prev, nxt = _halo_rows_index(tm, n_rows, hb)
    return pl.pallas_call(
        functools.partial(_conv3_kernel, tps=tps, act=act),
        grid=(n_rows // tm, c // ct),
        in_specs=[pl.BlockSpec((tm, ct), lambda i, cc: (i, cc)),
                  pl.BlockSpec((hb, ct), lambda i, cc: (prev(i), cc)),
                  pl.BlockSpec((hb, ct), lambda i, cc: (nxt(i), cc)),
                  pl.BlockSpec((3, ct), lambda i, cc: (0, cc)),
                  pl.BlockSpec((1, ct), lambda i, cc: (0, cc))],
        out_specs=out_spec, out_shape=out_shape,
        compiler_params=_cparams(("parallel", "parallel")), name=name,
    )(x, x, x, w.astype(F32), b.reshape(1, c).astype(F32))


def _hy_filter_kernel(bands_ref, w1_ref, b1_ref, w2_ref, b2_ref, fr_ref, w3_ref, dec_ref, sum_ref, dif_ref,
                      *, seq, tl, d):
    i = pl.program_id(0)
    pos = (lax.broadcasted_iota(jnp.int32, (tl, V7X_LANES), 0) + i * tl).astype(F32)
    lane = lax.broadcasted_iota(jnp.int32, (tl, V7X_LANES), 1)
    t01 = pos / float(max(seq - 1, 1))
    ang = (2.0 * math.pi / seq) * pos * bands_ref[...]
    feats = jnp.where(lane == 0, t01, jnp.where(lane <= HY_BANDS, jnp.cos(ang), -jnp.sin(ang)))
    fr = fr_ref[...]
    h = jnp.sin(fr[0:1] * (_hdot(feats, w1_ref[...]) + b1_ref[...]))
    h = jnp.sin(fr[1:2] * (_hdot(h, w2_ref[...]) + b2_ref[...]))
    k = _hdot(h, w3_ref[...]) * jnp.exp(-t01[:, 0:1] * jnp.abs(dec_ref[...]))
    not_first = (pos[:, 0:1] > 0.0).astype(F32)
    for o in range(2):
        kf = k[:, (2 * o) * d:(2 * o + 1) * d]
        kb = k[:, (2 * o + 1) * d:(2 * o + 2) * d] * not_first
        sum_ref[:, o * d:(o + 1) * d] = (kf + kb).astype(sum_ref.dtype)
        dif_ref[:, o * d:(o + 1) * d] = (kb - kf).astype(dif_ref.dtype)


def _hy_filters(seq, p, d):
    lanes = V7X_LANES
    fw = p["hy_f_w1"].shape[1]
    bands = jnp.linspace(1e-4, HY_BANDS - 1, HY_BANDS, dtype=F32)
    bands_row = _pad_to(jnp.concatenate([jnp.zeros((1,), F32), bands, bands])[None, :], 1, lanes)
    w1 = _pad_to(_pad_to(p["hy_f_w1"].astype(F32), 0, lanes), 1, lanes)
    b1 = _pad_to(p["hy_f_b1"].astype(F32)[None, :], 1, lanes)
    w2 = _pad_to(_pad_to(p["hy_f_w2"].astype(F32), 0, lanes), 1, lanes)
    b2 = _pad_to(p["hy_f_b2"].astype(F32)[None, :], 1, lanes)
    fr = _pad_to(_pad_to(p["hy_f_freq"].astype(F32), 1, lanes), 0, V7X_SUBLANES)
    w3 = _pad_to(p["hy_f_w3"].astype(F32), 0, lanes)
    dec = p["hy_decay"].astype(F32).reshape(1, 4 * d)
    assert fw <= lanes
    tl = _tile(seq, 256, V7X_BF16_ROWS)
    full = lambda a: pl.BlockSpec(a.shape, lambda i: (0,) * a.ndim)
    ins = [bands_row, w1, b1, w2, b2, fr, w3, dec]
    return pl.pallas_call(
        functools.partial(_hy_filter_kernel, seq=seq, tl=tl, d=d),
        grid=(seq // tl,),
        in_specs=[full(a) for a in ins],
        out_specs=[pl.BlockSpec((tl, 2 * d), lambda i: (i, 0))] * 2,
        out_shape=[jax.ShapeDtypeStruct((seq, 2 * d), BF16)] * 2,
        compiler_params=_cparams(("parallel",)), name="hyena_filters",
    )(*ins)


def _dft_kernel(c_ref, s_ref, ct_ref, st_ref, *, seq, tr):
    i = pl.program_id(0)
    row = lax.broadcasted_iota(jnp.int32, (tr, seq), 0) + i * tr
    col = lax.broadcasted_iota(jnp.int32, (tr, seq), 1)
    scale = math.pi / (2 * seq)
    m = ((2 * row + 1) * col) & (4 * seq - 1)
    ang = m.astype(F32) * scale
    c_ref[...] = jnp.cos(ang).astype(BF16)
    s_ref[...] = jnp.sin(ang).astype(BF16)
    mt = ((2 * col + 1) * row) & (4 * seq - 1)
    angt = mt.astype(F32) * scale
    ct_ref[...] = jnp.cos(angt).astype(BF16)
    st_ref[...] = jnp.sin(angt).astype(BF16)


def _dft_mats(seq):
    assert seq & (seq - 1) == 0, "token count must be a power of two"
    tr = _tile(seq, 256, V7X_BF16_ROWS)
    spec = pl.BlockSpec((tr, seq), lambda i: (i, 0))
    return pl.pallas_call(
        functools.partial(_dft_kernel, seq=seq, tr=tr), grid=(seq // tr,), in_specs=[],
        out_specs=[spec] * 4, out_shape=[jax.ShapeDtypeStruct((seq, seq), BF16)] * 4,
        compiler_params=_cparams(("parallel",)), name="dft_matrices")()


def _hy_fwd_kernel(c_ref, s_ref, v_ref, kr_ref, ki_ref, wr_ref, wi_ref):
    v = v_ref[0]
    cv = jnp.dot(c_ref[...], v, preferred_element_type=F32)
    sv = jnp.dot(s_ref[...], v, preferred_element_type=F32)
    kr, ki = kr_ref[...], ki_ref[...]
    wr_ref[...] = (cv * kr + sv * ki).astype(wr_ref.dtype)
    wi_ref[...] = (cv * ki - sv * kr).astype(wi_ref.dtype)


def _hy_inv_kernel(ct_ref, st_ref, wr_ref, wi_ref, v_ref, g_ref, skip_ref, o_ref, *, seq):
    y = (jnp.dot(ct_ref[...], wr_ref[...], preferred_element_type=F32)
         - jnp.dot(st_ref[...], wi_ref[...], preferred_element_type=F32)) * (1.0 / seq)
    y = y + v_ref[0].astype(F32) * skip_ref[...]
    o_ref[0] = (g_ref[0].astype(F32) * y).astype(o_ref.dtype)


def _hy_long_conv(vsrc, v_idx, gsrc, g_idx, mats, kr, ki, order, skip, d):
    c, s, ct, st = mats
    _, seq, cols = vsrc.shape
    tm = _tile(seq, 512, V7X_BF16_ROWS)
    tn = _tile(d, 512, V7X_LANES)
    cpd = d // tn
    kspec = pl.BlockSpec((tm, tn), lambda i, j: (i, order * cpd + j % cpd))
    wr, wi = pl.pallas_call(
        _hy_fwd_kernel, grid=(seq // tm, cols // tn),
        in_specs=[pl.BlockSpec((tm, seq), lambda i, j: (i, 0)), pl.BlockSpec((tm, seq), lambda i, j: (i, 0)),
                  pl.BlockSpec((1, seq, tn), lambda i, j: (v_idx, 0, j)), kspec, kspec],
        out_specs=[pl.BlockSpec((tm, tn), lambda i, j: (i, j))] * 2,
        out_shape=[jax.ShapeDtypeStruct((seq, cols), BF16)] * 2,
        compiler_params=_cparams(("parallel", "parallel")), name="hyena_dft_fwd",
    )(c, s, vsrc, kr, ki)
    return pl.pallas_call(
        functools.partial(_hy_inv_kernel, seq=seq), grid=(seq // tm, cols // tn),
        in_specs=[pl.BlockSpec((tm, seq), lambda i, j: (i, 0)), pl.BlockSpec((tm, seq), lambda i, j: (i, 0)),
                  pl.BlockSpec((seq, tn), lambda i, j: (0, j)), pl.BlockSpec((seq, tn), lambda i, j: (0, j)),
                  pl.BlockSpec((1, tm, tn), lambda i, j: (v_idx, i, j)),
                  pl.BlockSpec((1, tm, tn), lambda i, j: (g_idx, i, j)),
                  pl.BlockSpec((1, tn), lambda i, j: (0, j % cpd))],
        out_specs=pl.BlockSpec((1, tm, tn), lambda i, j: (0, i, j)),
        out_shape=jax.ShapeDtypeStruct((1, seq, cols), BF16),
        compiler_params=_cparams(("parallel", "parallel")), name="hyena_dft_inv",
    )(ct, st, wr, wi, vsrc, gsrc, skip.reshape(1, d).astype(F32))


def _hyena_mixer(u, nb, seq, p, w):
    d = u.shape[1]
    proj = _matmul(u, w["hy_w_in"], p["hy_b_in"], out_dtype=F32, name="hyena_in_proj")
    planes = _conv3(proj, p["hy_conv_w"], p["hy_conv_b"], seq, time_major_nb=(nb, d), name="hyena_short_conv")
    ksum, kdif = _hy_filters(seq, p, d)
    mats = _dft_mats(seq)
    kr = _matmul(mats[0], ksum, out_dtype=F32, name="hyena_filter_spec_re")
    ki = _matmul(mats[1], kdif, out_dtype=F32, name="hyena_filter_spec_im")
    z1 = _hy_long_conv(planes, 0, planes, 1, mats, kr, ki, 0, p["hy_skip"][0], d)
    return _hy_long_conv(z1, 0, planes, 2, mats, kr, ki, 1, p["hy_skip"][1], d)


def _hyena_out(lay_rows, z, tail, p, w, nb, seq, *, alpha, name):
    d = w["hy_w_out"].shape[0]
    tm = lay_rows.tm
    tps = lay_rows.tps
    spec = pl.BlockSpec((1, tm, d), lay_rows._ix(lambda i: (0, i % tps, i // tps)))
    return _out_proj_call(lay_rows, lambda z_ref: z_ref[0], [(z, spec)], w["hy_w_out"], p["hy_b_out"], tail,
                          alpha=alpha, name=name)


def _rw_proj_kernel(*refs, names, tps, hd):
    r = dict(zip(names, refs))
    i = pl.program_id(0)
    cur = r["u"][...].astype(F32)
    prev_row, next_row = _halo_rows(i, tps, r["up"], r["un"], V7X_BF16_ROWS)
    up, dn = _shift_rows(cur, prev_row, next_row)
    xx = 0.5 * (up + dn) - cur
    mu = r["mu"][...]
    mix = lambda j: (cur + xx * mu[j:j + 1]).astype(BF16)
    xr, xw, xk, xv, xa, xg = [mix(j) for j in range(6)]
    qq = r["qq"][...]
    rr = jnp.dot(xr, r["w_rkv"][0], preferred_element_type=F32)
    kk0 = jnp.dot(xk, r["w_rkv"][1], preferred_element_type=F32)
    vv = jnp.dot(xv, r["w_rkv"][2], preferred_element_type=F32)
    gg = _bdot(jax.nn.sigmoid(jnp.dot(xg, r["g1"][...], preferred_element_type=F32)), r["g2"][...])
    kx = kk0 * r["k_k"][...]
    kkn = kx * lax.rsqrt(_split_dot(kx * kx, qq) + 1e-6)
    r["r"][...] = rr
    r["v"][...] = vv
    r["g"][...] = gg.astype(BF16)
    r["na"][...] = -kkn
    kd_sum = jnp.zeros_like(kk0)
    for dd in range(2):
        hw = jnp.tanh(jnp.dot(xw, r["lora_w1"][dd], preferred_element_type=F32))
        wpre = r["bias_w"][dd] + _bdot(hw, r["lora_w2"][dd])
        logw = -_softplus(-wpre) - 0.5
        r["lw"][dd] = -jnp.exp(logw)
        a = jax.nn.sigmoid(
            r["bias_a"][dd] + _bdot(jnp.dot(xa, r["lora_a1"][dd], preferred_element_type=F32), r["lora_a2"][dd]))
        kd = kk0 * (1.0 + (a - 1.0) * r["k_a"][...])
        r["kd"][dd] = kd
        r["b"][dd] = kkn * a
        kd_sum = kd_sum + kd
    r["bonus"][...] = (_split_dot(rr * kd_sum * r["r_k"][...], qq) * vv).astype(BF16)


_RW_OUTS = ["r", "v", "g", "na", "bonus", "lw", "kd", "b"]
_RW_STACKED = ("lw", "kd", "b")


def _rw_project(u, seq, p, w):
    n_rows, d = u.shape
    hb = V7X_BF16_ROWS
    tm = _tile(seq, 256, hb)
    tps = seq // tm
    lanes = V7X_LANES
    names, args, specs = [], [], []

    def add(nm, arr, spec=None):
        names.append(nm)
        args.append(arr)
        nd = arr.ndim
        specs.append(spec if spec is not None else pl.BlockSpec(arr.shape, lambda i: (0,) * nd))

    prev, nxt = _halo_rows_index(tm, n_rows, hb)
    add("u", u, pl.BlockSpec((tm, d), lambda i: (i, 0)))
    add("up", u, pl.BlockSpec((hb, d), lambda i: (prev(i), 0)))
    add("un", u, pl.BlockSpec((hb, d), lambda i: (nxt(i), 0)))
    add("mu", _pad_to(p["rw_mu"].astype(F32), 0, V7X_SUBLANES))
    add("w_rkv", w["rw_w_rkv"])
    add("g1", w["rw_g1"])
    add("g2", w["rw_g2"])
    add("bias_w", p["rw_w0"].astype(F32).reshape(2, 1, d))
    add("lora_w1", w["rw_w1"])
    add("lora_w2", w["rw_w2"])
    add("bias_a", p["rw_a0"].astype(F32).reshape(2, 1, d))
    add("lora_a1", w["rw_a1"])
    add("lora_a2", w["rw_a2"])
    add("k_k", p["rw_k_k"].astype(F32).reshape(1, d))
    add("k_a", p["rw_k_a"].astype(F32).reshape(1, d))
    add("r_k", p["rw_r_k"].astype(F32).reshape(1, d))
    add("qq", _block_ones(d, RW_HEAD))
    out_dt = {nm: (BF16 if nm in ("g", "bonus") else F32) for nm in _RW_OUTS}
    row_spec = pl.BlockSpec((tm, d), lambda i: (i, 0))
    dir_spec = pl.BlockSpec((2, tm, d), lambda i: (0, i, 0))
    outs = pl.pallas_call(
        functools.partial(_rw_proj_kernel, names=names + _RW_OUTS, tps=tps, hd=RW_HEAD),
        grid=(n_rows // tm,), in_specs=specs,
        out_specs=[dir_spec if nm in _RW_STACKED else row_spec for nm in _RW_OUTS],
        out_shape=[jax.ShapeDtypeStruct(((2,) if nm in _RW_STACKED else ()) + (n_rows, d), out_dt[nm])
                   for nm in _RW_OUTS],
        compiler_params=_cparams(("parallel",)), name="rwkv7_projections")(*args)
    return dict(zip(_RW_OUTS, outs))


TRI_BASE = 8


def _unit_tri_inverse(nmat, cl):
    ii = lax.broadcasted_iota(jnp.int32, (cl, cl), 0)
    jj = lax.broadcasted_iota(jnp.int32, (cl, cl), 1)
    zero = jnp.float32(0.0)
    ident = (ii == jj).astype(F32)
    block_gap = lambda k: lax.shift_right_logical(ii ^ jj, k)
    k0 = TRI_BASE.bit_length() - 1
    n0 = [jnp.where(block_gap(k0) == 0, n, zero) for n in nmat]
    minv = [ident + n for n in n0]
    pw = n0
    for _ in range(k0 - 1):
        pw = [_bdot(x, x) for x in pw]
        minv = [m + _bdot(m, x) for m, x in zip(minv, pw)]
    k = k0
    while (1 << k) < cl:
        off = block_gap(k) == 1
        minv = [m + _bdot(_bdot(m, jnp.where(off, n, zero)), m) for m, n in zip(minv, nmat)]
        k += 1
    return minv


def _rw_scan_kernel(r_ref, lw_ref, k_ref, v_ref, a_ref, b_ref, s0_ref, y_ref, sfin_ref, s_ref, *, n_chunks, hd):
    dirn, c = pl.program_id(1), pl.program_id(2)
    cl, d = r_ref.shape[1], r_ref.shape[2]
    pw_ = 2 * hd
    n_pairs = d // pw_

    @pl.when(c == 0)
    def _():
        s_ref[...] = s0_ref[0, 0]

    incl, strict = _time_masks(dirn, cl)
    lw = lw_ref[0, 0]
    cum = _hdot(incl.astype(F32), lw)
    p_tot = jnp.exp(jnp.sum(lw, axis=0, keepdims=True))
    p_inv = jnp.exp(-cum)
    at = a_ref[0] * jnp.exp(cum - lw)
    rt = r_ref[0] * jnp.exp(cum)
    bt = (b_ref[0, 0] * p_inv).astype(BF16)
    kt = (k_ref[0, 0] * p_inv).astype(BF16)
    v = v_ref[0]
    hd_shift = hd.bit_length() - 1
    head_of = lambda shape, axis: lax.shift_right_logical(lax.broadcasted_iota(jnp.int32, shape, axis), hd_shift)
    lane = head_of((cl, pw_), 1)
    lane2 = head_of((2 * cl, pw_), 1)
    blk = head_of((pw_, pw_), 0) == head_of((pw_, pw_), 1)
    zero = jnp.float32(0.0)
    pairs = range(n_pairs)
    halves = [(pr, hf) for pr in pairs for hf in range(2)]
    sl = lambda x, pr: x[:, pr * pw_:(pr + 1) * pw_]
    s_old = [s_ref[pr] for pr in pairs]
    s_bf = [s.astype(BF16) for s in s_old]
    xs = [jnp.concatenate([sl(at, pr), sl(rt, pr)], axis=0) for pr in pairs]
    xs0 = [_bdot_nt(xs[pr], s_bf[pr]) for pr in pairs]
    xm = [jnp.where(lane2 == hf, xs[pr], zero).astype(BF16) for pr, hf in halves]
    gb = [lax.dot_general(xm[i], sl(bt, pr), NT_DIMS, preferred_element_type=F32) for i, (pr, hf) in enumerate(halves)]
    gk = [lax.dot_general(xm[i], sl(kt, pr), NT_DIMS, preferred_element_type=F32) for i, (pr, hf) in enumerate(halves)]
    minv = _unit_tri_inverse([jnp.where(strict, g[:cl], zero) for g in gb], cl)
    vm = [jnp.where(lane == hf, sl(v, pr), zero).astype(BF16) for pr, hf in halves]
    rhs = [jnp.where(lane == hf, xs0[pr][:cl], zero)
           + jnp.dot(jnp.where(strict, gk[i][:cl], zero).astype(BF16), vm[i], preferred_element_type=F32)
           for i, (pr, hf) in enumerate(halves)]
    u = [_bdot(minv[i], rhs[i]) for i in range(len(halves))]
    u_b = [x.astype(BF16) for x in u]
    yh = [jnp.dot(jnp.where(incl, gb[i][cl:], zero).astype(BF16), u_b[i], preferred_element_type=F32)
          + jnp.dot(jnp.where(incl, gk[i][cl:], zero).astype(BF16), vm[i], preferred_element_type=F32)
          for i in range(len(halves))]
    for pr in pairs:
        y_ref[0, 0, :, pr * pw_:(pr + 1) * pw_] = xs0[pr][cl:] + yh[2 * pr] + yh[2 * pr + 1]
    for pr in pairs:
        uv = jnp.concatenate([u[2 * pr] + u[2 * pr + 1], sl(v, pr)], axis=0).astype(BF16)
        bk = jnp.concatenate([sl(bt, pr), sl(kt, pr)], axis=0)
        delta = lax.dot_general(uv, bk, TN_DIMS, preferred_element_type=F32)
        s_ref[pr] = (s_old[pr] + jnp.where(blk, delta, zero)) * sl(p_tot, pr)

    @pl.when(c == n_chunks - 1)
    def _():
        sfin_ref[0, 0] = s_ref[...]


def _rw_scan(q, s0, nb, seq, d):
    hd = RW_HEAD
    pw_ = 2 * hd
    assert pw_ == V7X_LANES and d % pw_ == 0
    n_pairs = d // pw_
    n_chunks = seq // CHUNK
    cidx = lambda dd, c: c + dd * (n_chunks - 1 - 2 * c)
    v3 = lambda a: a.reshape(nb, seq, d)
    v4 = lambda a: a.reshape(2, nb, seq, d)
    tok = pl.BlockSpec((1, CHUNK, d), lambda b, dd, c: (b, cidx(dd, c), 0))
    tok_d = pl.BlockSpec((1, 1, CHUNK, d), lambda b, dd, c: (dd, b, cidx(dd, c), 0))
    st_spec = pl.BlockSpec((1, 1, n_pairs, pw_, pw_), lambda b, dd, c: (b, dd, 0, 0, 0))
    y, sfin = pl.pallas_call(
        functools.partial(_rw_scan_kernel, n_chunks=n_chunks, hd=hd),
        grid=(nb, 2, n_chunks),
        in_specs=[tok, tok_d, tok_d, tok, tok, tok_d, st_spec],
        out_specs=[tok_d, st_spec],
        out_shape=[jax.ShapeDtypeStruct((2, nb, seq, d), F32),
                   jax.ShapeDtypeStruct((nb, 2, n_pairs, pw_, pw_), F32)],
        scratch_shapes=[pltpu.VMEM((n_pairs, pw_, pw_), F32)],
        compiler_params=_cparams(("parallel", "arbitrary", "arbitrary")), name="rwkv7_scan",
    )(v3(q["r"]), v4(q["lw"]), v4(q["kd"]), v3(q["v"]), v3(q["na"]), v4(q["b"]), s0)
    return y, sfin


def _rw_out(lay, y, q, tail, p, w, *, alpha, name):
    d = w["rw_w_out"].shape[0]
    n_rows = lay.nb * lay.seq
    y2 = y.reshape(2, n_rows, d)
    qq = _block_ones(d, RW_HEAD)
    lnx_g = p["rw_lnx_g"].astype(F32).reshape(1, d)
    lnx_b = p["rw_lnx_b"].astype(F32).reshape(1, d)
    yspec = pl.BlockSpec((2, lay.tm, d), lay._ix(lambda i: (0, i, 0)))

    def prologue(y_ref, bonus_ref, g_ref, qq_ref, lg_ref, lb_ref):
        yy = y_ref[0] + y_ref[1]
        inv = 1.0 / RW_HEAD
        mean = _split_dot(yy, qq_ref[...]) * inv
        yc = yy - mean
        var = _split_dot(yc * yc, qq_ref[...]) * inv
        yn = yc * lax.rsqrt(var + RW_GN_EPS) * lg_ref[...] + lb_ref[...]
        return (yn + bonus_ref[...].astype(F32)) * g_ref[...].astype(F32)

    pro = [(y2, yspec), lay.raster(q["bonus"], d), lay.raster(q["g"], d), lay.const(qq), lay.const(lnx_g),
           lay.const(lnx_b)]
    return _out_proj_call(lay, prologue, pro, w["rw_w_out"], None, tail, alpha=alpha, name=name)


def _ssd_scan_kernel(x_ref, b_ref, c_ref, dt_ref, dtb_ref, a_ref, s0_ref, xp_ref, eye_ref, y_ref, sfin_ref, s_ref,
                     *, n_chunks, ng, nr, hp, ns):
    dirn, c = pl.program_id(1), pl.program_id(2)
    cl = x_ref.shape[1]

    @pl.when(c == 0)
    def _():
        s_ref[...] = s0_ref[0, 0]

    incl, _ = _time_masks(dirn, cl)
    tri = incl.astype(F32)
    eye = eye_ref[...]
    xp = xp_ref[...]
    dt = _softplus(dt_ref[0] + dtb_ref[0])
    dta = dt * a_ref[0]
    cum = _hdot(tri, dta)
    tot = jnp.sum(dta, axis=0, keepdims=True)
    cum_t = _transpose_via_eye(cum, eye)
    dt_t = _transpose_via_eye(dt, eye)
    e_in = _split_dot(jnp.exp(cum), xp)
    e_end = _split_dot(jnp.exp(tot - cum) * dt, xp)
    e_tot = _split_dot(jnp.broadcast_to(jnp.exp(tot), (V7X_SUBLANES, tot.shape[1])), xp)[0:1]
    x = x_ref[0]
    xw = (x.astype(F32) * e_end).astype(BF16)
    gw = nr * hp
    pair = 2 * hp
    lane = lax.broadcasted_iota(jnp.int32, (cl, pair), 1)
    neg = jnp.float32(-jnp.inf)
    for g in range(ng):
        bm = b_ref[0, :, g * ns:(g + 1) * ns]
        cm = c_ref[0, :, g * ns:(g + 1) * ns]
        cb = lax.dot_general(cm, bm, NT_DIMS, preferred_element_type=F32)
        s_g = s_ref[g]
        y_inter = jnp.dot(cm, s_g.astype(BF16), preferred_element_type=F32) * e_in[:, g * gw:(g + 1) * gw]
        for rp in range(nr // 2):
            y_pair = jnp.zeros((cl, pair), F32)
            xpair = x[:, g * gw + rp * pair:g * gw + (rp + 1) * pair]
            for half in range(2):
                h = g * nr + rp * 2 + half
                dec = jnp.exp(jnp.where(incl, cum[:, h:h + 1] - cum_t[h:h + 1, :], neg))
                sc = (cb * dec * dt_t[h:h + 1, :]).astype(BF16)
                xh = jnp.where((lane >= half * hp) & (lane < (half + 1) * hp), xpair, jnp.zeros_like(xpair))
                y_pair = y_pair + jnp.dot(sc, xh, preferred_element_type=F32)
            lo = g * gw + rp * pair
            y_ref[0, 0, :, lo:lo + pair] = y_pair + y_inter[:, rp * pair:(rp + 1) * pair]
        bt = lax.dot_general(eye.astype(BF16), bm, NT_DIMS, preferred_element_type=F32).astype(BF16)
        s_ref[g] = s_g * e_tot[:, g * gw:(g + 1) * gw] + jnp.dot(
            bt, xw[:, g * gw:(g + 1) * gw], preferred_element_type=F32)

    @pl.when(c == n_chunks - 1)
    def _():
        sfin_ref[0, 0] = s_ref[...]


def _ssd_scan(xbc, dt_raw, s0, nb, seq, p):
    ng, ns, hp = SSD_GROUPS, SSD_STATE, SSD_HEAD
    inner = xbc.shape[1] - 2 * ng * ns
    nh = inner // hp
    nr = nh // ng
    lanes = V7X_LANES
    assert ns == lanes and nh <= lanes and nr % 2 == 0
    n_chunks = seq // CHUNK
    cidx = lambda dd, c: c + dd * (n_chunks - 1 - 2 * c)
    xbc3 = xbc.reshape(nb, seq, xbc.shape[1])
    dt3 = dt_raw.reshape(nb, seq, 2 * lanes)
    a = -jnp.exp(p["ssd_A_log"].astype(F32))
    a_pad = _pad_to(a, 1, lanes).reshape(2, 1, lanes)
    dtb = _pad_to(p["ssd_dt_bias"].astype(F32), 1, lanes).reshape(2, 1, lanes)
    xp = _expand_mat(lanes, nh, hp)
    eye = jnp.eye(lanes, dtype=F32)
    bw = ng * ns
    y, sfin = pl.pallas_call(
        functools.partial(_ssd_scan_kernel, n_chunks=n_chunks, ng=ng, nr=nr, hp=hp, ns=ns),
        grid=(nb, 2, n_chunks),
        in_specs=[pl.BlockSpec((1, CHUNK, inner), lambda b, dd, c: (b, cidx(dd, c), 0)),
                  pl.BlockSpec((1, CHUNK, bw), lambda b, dd, c: (b, cidx(dd, c), inner // bw)),
                  pl.BlockSpec((1, CHUNK, bw), lambda b, dd, c: (b, cidx(dd, c), inner // bw + 1)),
                  pl.BlockSpec((1, CHUNK, lanes), lambda b, dd, c: (b, cidx(dd, c), dd)),
                  pl.BlockSpec((1, 1, lanes), lambda b, dd, c: (dd, 0, 0)),
                  pl.BlockSpec((1, 1, lanes), lambda b, dd, c: (dd, 0, 0)),
                  pl.BlockSpec((1, 1, ng, ns, nr * hp), lambda b, dd, c: (b, dd, 0, 0, 0)),
                  pl.BlockSpec(xp.shape, lambda b, dd, c: (0, 0)),
                  pl.BlockSpec(eye.shape, lambda b, dd, c: (0, 0))],
        out_specs=[pl.BlockSpec((1, 1, CHUNK, inner), lambda b, dd, c: (dd, b, cidx(dd, c), 0)),
                   pl.BlockSpec((1, 1, ng, ns, nr * hp), lambda b, dd, c: (b, dd, 0, 0, 0))],
        out_shape=[jax.ShapeDtypeStruct((2, nb, seq, inner), F32),
                   jax.ShapeDtypeStruct((nb, 2, ng, ns, nr * hp), F32)],
        scratch_shapes=[pltpu.VMEM((ng, ns, nr * hp), F32)],
        compiler_params=_cparams(("parallel", "arbitrary", "arbitrary")), name="ssd_scan",
    )(xbc3, xbc3, xbc3, dt3, dtb, a_pad, s0, xp, eye)
    return y.reshape(2, nb * seq, inner), sfin


def _ssd_project(u, nb, seq, p, w, colmajor):
    mm = (lambda x, wt, **kw: _matmul_r2c(x, wt, None, nb, **kw)) if colmajor else (
        lambda x, wt, **kw: _matmul(x, wt, None, **kw))
    z = mm(u, w["ssd_w_z"], out_dtype=BF16, name="ssd_in_proj_z")
    xbc_raw = mm(u, w["ssd_w_xbc"], out_dtype=F32, name="ssd_in_proj_xbc")
    dt_raw = mm(u, w["ssd_w_dt"], out_dtype=F32, name="ssd_in_proj_dt")
    xbc = _conv3(xbc_raw, p["ssd_conv_w"], p["ssd_conv_b"], seq, act=_silu, name="ssd_short_conv")
    return z, xbc, dt_raw


def _ssd_out(lay, y, z, xbc, tail, p, w, *, alpha, name):
    inner = w["ssd_w_out"].shape[0]
    ng = SSD_GROUPS
    gwid = inner // ng
    d_row = jnp.repeat(p["ssd_d_skip"].astype(F32), SSD_HEAD).reshape(1, inner)
    ng_row = p["ssd_norm_g"].astype(F32).reshape(1, inner)

    def prologue(y0_ref, y1_ref, xs_ref, z_ref, d_ref, g_ref):
        flat = lambda ref: ref[...].reshape(-1, ref.shape[-1]).astype(F32)
        yy = flat(xs_ref) * d_ref[...] + flat(y0_ref) + flat(y1_ref)
        yy = yy * _silu(flat(z_ref))
        parts = []
        for g in range(ng):
            yg = yy[:, g * gwid:(g + 1) * gwid]
            ms = jnp.mean(yg * yg, axis=-1, keepdims=True)
            parts.append(yg * lax.rsqrt(ms + 1e-6))
        return jnp.concatenate(parts, axis=1) * g_ref[...]

    pro = [lay.native(y, inner, lead=0), lay.native(y, inner, lead=1), lay.native(xbc, inner),
           lay.native(z, inner), lay.const(d_row), lay.const(ng_row)]
    return _out_proj_call(lay, prologue, pro, w["ssd_w_out"], None, tail, alpha=alpha, name=name)


def _gdn_scan_kernel(q_ref, k_ref, v_ref, a_ref, bta_ref, dtb_ref, al_ref, s0_ref, eye_ref, o_ref, sfin_ref, s_ref,
                     *, n_chunks, hk, hv, dh):
    dirn, c = pl.program_id(1), pl.program_id(2)
    cl = q_ref.shape[1]

    @pl.when(c == 0)
    def _():
        s_ref[...] = s0_ref[0, 0]

    incl, strict = _time_masks(dirn, cl)
    tri = incl.astype(F32)
    eye = eye_ref[...]
    eye_b = eye.astype(BF16)
    neg = jnp.float32(-jnp.inf)
    g = al_ref[0] * _softplus(a_ref[0] + dtb_ref[0])
    beta = jax.nn.sigmoid(bta_ref[0])
    gn = _hdot(tri, g)
    gtot = jnp.sum(g, axis=0, keepdims=True)
    gn_t = _transpose_via_eye(gn, eye)
    e_in = jnp.exp(gn)
    e_end = jnp.exp(gtot - gn)
    e_tot = jnp.exp(gtot)
    rep = hv // hk
    heads = range(hv)
    s_old = [s_ref[h] for h in heads]
    s_bf = [s.astype(BF16) for s in s_old]
    qn, kn, kk, qk = [], [], [], []
    for hq in range(hk):
        qh = q_ref[0, :, hq * dh:(hq + 1) * dh].astype(F32)
        kh = k_ref[0, :, hq * dh:(hq + 1) * dh].astype(F32)
        qn.append(qh * lax.rsqrt(jnp.sum(qh * qh, axis=-1, keepdims=True) + 1e-6) * (dh ** -0.5))
        kn.append(kh * lax.rsqrt(jnp.sum(kh * kh, axis=-1, keepdims=True) + 1e-6))
        gram = _bdot_nt(jnp.concatenate([kn[hq], qn[hq]], axis=0), kn[hq])
        kk.append(gram[:cl])
        qk.append(gram[cl:])
    bh = [beta[:, h:h + 1] for h in heads]
    diff = [gn[:, h:h + 1] - gn_t[h:h + 1, :] for h in heads]
    nmat = [-(kk[h // rep] * bh[h] * jnp.exp(jnp.where(strict, diff[h], neg))) for h in heads]
    attn = [(qk[h // rep] * jnp.exp(jnp.where(incl, diff[h], neg))).astype(BF16) for h in heads]
    minv = _unit_tri_inverse(nmat, cl)
    rhs =[jnp.concatenate([v_ref[0, :, h * dh:(h + 1) * dh].astype(F32) * bh[h],
                            kn[h // rep] * (bh[h] * e_in[:, h:h + 1])], axis=1) for h in heads]
    sol = [_bdot(minv[h], rhs[h]) for h in heads]
    u_b = [(sol[h][:, :dh] - _bdot(sol[h][:, dh:], s_bf[h])).astype(BF16) for h in heads]
    out = [_bdot(qn[h // rep] * e_in[:, h:h + 1], s_bf[h]) + jnp.dot(attn[h], u_b[h], preferred_element_type=F32)
           for h in heads]
    s_new = [s_old[h] * e_tot[:, h:h + 1]
             + lax.dot_general((kn[h // rep] * e_end[:, h:h + 1]).astype(BF16), u_b[h], TN_DIMS,
                               preferred_element_type=F32) for h in heads]
    for h in heads:
        o_ref[0, 0, :, h * dh:(h + 1) * dh] = out[h]
    for h in heads:
        s_ref[h] = s_new[h]

    @pl.when(c == n_chunks - 1)
    def _():
        sfin_ref[0, 0] = s_ref[...]


def _gdn_scan(qkv, ab, s0, nb, seq, p):
    dh = GDN_HEAD
    lanes = V7X_LANES
    hv = p["gdn_dt_bias"].shape[1]
    vw = hv * dh
    qk = (qkv.shape[1] - vw) // 2
    hk = qk // dh
    assert dh == lanes and hv <= lanes
    n_chunks = seq // CHUNK
    cidx = lambda dd, c: c + dd * (n_chunks - 1 - 2 * c)
    qkv3 = qkv.reshape(nb, seq, qkv.shape[1])
    ab3 = ab.reshape(nb, seq, 4 * lanes)
    dtb = _pad_to(p["gdn_dt_bias"].astype(F32), 1, lanes).reshape(2, 1, lanes)
    al = _pad_to(-jnp.exp(p["gdn_A_log"].astype(F32)), 1, lanes).reshape(2, 1, lanes)
    eye = jnp.eye(lanes, dtype=F32)
    o, sfin = pl.pallas_call(
        functools.partial(_gdn_scan_kernel, n_chunks=n_chunks, hk=hk, hv=hv, dh=dh),
        grid=(nb, 2, n_chunks),
        in_specs=[pl.BlockSpec((1, CHUNK, qk), lambda b, dd, c: (b, cidx(dd, c), 0)),
                  pl.BlockSpec((1, CHUNK, qk), lambda b, dd, c: (b, cidx(dd, c), 1)),
                  pl.BlockSpec((1, CHUNK, vw), lambda b, dd, c: (b, cidx(dd, c), 2 * qk // vw)),
                  pl.BlockSpec((1, CHUNK, lanes), lambda b, dd, c: (b, cidx(dd, c), 2 * dd)),
                  pl.BlockSpec((1, CHUNK, lanes), lambda b, dd, c: (b, cidx(dd, c), 2 * dd + 1)),
                  pl.BlockSpec((1, 1, lanes), lambda b, dd, c: (dd, 0, 0)),
                  pl.BlockSpec((1, 1, lanes), lambda b, dd, c: (dd, 0, 0)),
                  pl.BlockSpec((1, 1, hv, dh, dh), lambda b, dd, c: (b, dd, 0, 0, 0)),
                  pl.BlockSpec(eye.shape, lambda b, dd, c: (0, 0))],
        out_specs=[pl.BlockSpec((1, 1, CHUNK, vw), lambda b, dd, c: (dd, b, cidx(dd, c), 0)),
                   pl.BlockSpec((1, 1, hv, dh, dh), lambda b, dd, c: (b, dd, 0, 0, 0))],
        out_shape=[jax.ShapeDtypeStruct((2, nb, seq, vw), F32),
                   jax.ShapeDtypeStruct((nb, 2, hv, dh, dh), F32)],
        scratch_shapes=[pltpu.VMEM((hv, dh, dh), F32)],
        compiler_params=_cparams(("parallel", "arbitrary", "arbitrary")), name="gdn_scan",
    )(qkv3, qkv3, qkv3, ab3, ab3, dtb, al, s0, eye)
    return o.reshape(2, nb * seq, vw), sfin


def _gdn_project(u, nb, seq, p, w, colmajor):
    mm = (lambda x, wt, **kw: _matmul_r2c(x, wt, None, nb, **kw)) if colmajor else (
        lambda x, wt, **kw: _matmul(x, wt, None, **kw))
    qkv_raw = mm(u, w["gdn_w_qkv"], out_dtype=F32, name="gdn_in_proj_qkv")
    z = mm(u, w["gdn_w_z"], out_dtype=BF16, name="gdn_in_proj_z")
    ab = mm(u, w["gdn_w_ab"], out_dtype=F32, name="gdn_in_proj_ab")
    qkv = _conv3(qkv_raw, p["gdn_conv_w"], None, seq, act=_silu, name="gdn_short_conv")
    return qkv, z, ab


def _gdn_out(lay, o, z, tail, p, w, *, alpha, name):
    vw = w["gdn_w_out"].shape[0]
    dh = GDN_HEAD
    hv = vw // dh
    g_row = jnp.tile(p["gdn_norm_g"].astype(F32), hv).reshape(1, vw)

    def prologue(o0_ref, o1_ref, z_ref, g_ref):
        flat = lambda ref: ref[...].reshape(-1, ref.shape[-1]).astype(F32)
        oo = flat(o0_ref) + flat(o1_ref)
        parts = []
        for h in range(hv):
            oh = oo[:, h * dh:(h + 1) * dh]
            ms = jnp.mean(oh * oh, axis=-1, keepdims=True)
            parts.append(oh * lax.rsqrt(ms + 1e-6))
        return jnp.concatenate(parts, axis=1) * g_ref[...] * _silu(flat(z_ref))

    pro = [lay.native(o, vw, lead=0), lay.native(o, vw, lead=1), lay.native(z, vw), lay.const(g_row)]
    return _out_proj_call(lay, prologue, pro, w["gdn_w_out"], None, tail, alpha=alpha, name=name)


def _prep_weights(p):
    lanes = V7X_LANES
    bf = lambda a: a.astype(BF16)
    d = p["hy_w_out"].shape[0]
    w = {k: bf(p[k]) for k in ("hy_w_in", "hy_w_out", "rw_w_rkv", "rw_w_out", "ssd_w_out", "gdn_w_out",
                               "ffn_w13", "ffn_w2", "moe_w13", "moe_w2", "mod_w")}
    lora = lambda a, ax: bf(_pad_to(a, ax, lanes * ((a.shape[ax] + lanes - 1) // lanes)))
    w["rw_w1"], w["rw_w2"] = lora(p["rw_w1"], 2), lora(p["rw_w2"], 1)
    w["rw_a1"], w["rw_a2"] = lora(p["rw_a1"], 2), lora(p["rw_a2"], 1)
    w["rw_g1"], w["rw_g2"] = lora(p["rw_g1"], 1), lora(p["rw_g2"], 0)
    inner = p["ssd_w_out"].shape[0]
    nbc = 2 * SSD_GROUPS * SSD_STATE
    nh = inner // SSD_HEAD
    ws = p["ssd_w_in"]
    w["ssd_w_z"] = bf(ws[:, :inner])
    w["ssd_w_xbc"] = bf(ws[:, inner:2 * inner + nbc])
    wdt = ws[:, 2 * inner + nbc:].reshape(d, 2, nh)
    w["ssd_w_dt"] = bf(_pad_to(wdt, 2, lanes).reshape(d, 2 * lanes))
    vw = p["gdn_w_out"].shape[0]
    hv = vw // GDN_HEAD
    qkvw = p["gdn_conv_w"].shape[1]
    wg = p["gdn_w_in"]
    w["gdn_w_qkv"] = bf(wg[:, :qkvw])
    w["gdn_w_z"] = bf(wg[:, qkvw:qkvw + vw])
    wab = wg[:, qkvw + vw:].reshape(d, 2, 2, hv)
    wab = jnp.transpose(wab, (0, 2, 1, 3))
    w["gdn_w_ab"] = bf(_pad_to(wab, 3, lanes).reshape(d, 4 * lanes))
    w["moe_router"] = _pad_to(p["moe_router"].astype(F32), 2, lanes)
    return w


_PARAM_NAMES = (
    "mod_w mod_b ln_g ln_b hy_w_in hy_b_in hy_conv_w hy_conv_b hy_f_w1 hy_f_b1 hy_f_w2 hy_f_b2 hy_f_freq "
    "hy_f_w3 hy_decay hy_skip hy_w_out hy_b_out rw_mu rw_w_rkv rw_w0 rw_w1 rw_w2 rw_a0 rw_a1 rw_a2 rw_g1 rw_g2 "
    "rw_k_k rw_k_a rw_r_k rw_lnx_g rw_lnx_b rw_w_out ssd_w_in ssd_conv_w ssd_conv_b ssd_dt_bias ssd_A_log "
    "ssd_d_skip ssd_norm_g ssd_w_out gdn_w_in gdn_conv_w gdn_dt_bias gdn_A_log gdn_norm_g gdn_w_out ffn_w13 "
    "ffn_w2 moe_router moe_w13 moe_w2").split()


def _forward(x, c, ctx, c_ctx, p):
    nb, seq, d = x.shape
    lc = ctx.shape[1]
    depth = p["mod_w"].shape[0]
    alpha = (2 * depth) ** 0.25
    n_exp = p["moe_router"].shape[2]
    assert seq == GRID_W * GRID_W and seq % CHUNK == 0 and lc % CHUNK == 0
    w = _prep_weights(p)

    cc = _pad_to(jnp.concatenate([c, c_ctx[None, :]], axis=0).astype(F32), 0, V7X_BF16_ROWS)
    mods = [
        _matmul(cc, w["mod_w"][i], p["mod_b"][i], pre=_silu, out_dtype=F32, name="adaln_modulation")
        .reshape(cc.shape[0], 6, d) for i in range(depth)]

    def chunk(i, k, stream):
        m = mods[i][:, k]
        return m[:nb, None, :] if stream == "x" else m[nb:nb + 1, None, :]

    lay = {"x": _RowLayout(nb, seq, True), "c": _RowLayout(nb, lc, False),
           "xcol": _RowLayout(nb, seq, True, mode="cols")}
    seqs = {"x": seq, "c": lc}
    h = {"x": x.reshape(nb * seq, d).astype(F32), "c": ctx.reshape(nb * lc, d).astype(F32)}

    u = {s: _modulate(lay[s], h[s], chunk(0, 1, s), chunk(0, 0, s)) for s in ("x", "c")}

    for i in range(depth):
        last = i == depth - 1
        kind = i % 4
        moe = i % 2 == 1
        streams = ("x",) if last else ("x", "c")

        def tail1(s):
            t = dict(h=h[s], gate=chunk(i, 2, s), ln_g=p["ln_g"][i, 0], ln_b=p["ln_b"][i, 0],
                     scale=chunk(i, 4, s), shift=chunk(i, 3, s))
            if moe:
                t.update(router=w["moe_router"][i // 2], n_exp=n_exp)
            return t

        res = {}
        if kind == 0:
            for s in streams:
                z = _hyena_mixer(u[s], nb, seqs[s], p, w)
                res[s] = _hyena_out(lay[s], z, tail1(s), p, w, nb, seqs[s], alpha=alpha, name=f"hyena_out_{s}")
        elif kind == 1:
            qc = _rw_project(u["c"], lc, p, w)
            qx = _rw_project(u["x"], seq, p, w)
            s0 = jnp.zeros((nb, 2, d // (2 * RW_HEAD), 2 * RW_HEAD, 2 * RW_HEAD), F32)
            yc, s_c = _rw_scan(qc, s0, nb, lc, d)
            yx, _ = _rw_scan(qx, s_c, nb, seq, d)
            ys, qs = {"x": yx, "c": yc}, {"x": qx, "c": qc}
            for s in streams:
                res[s] = _rw_out(lay[s], ys[s], qs[s], tail1(s), p, w, alpha=alpha, name=f"rwkv7_out_{s}")
        elif kind == 2:
            zc, xbc_c, dt_c = _ssd_project(u["c"], nb, lc, p, w, False)
            zx, xbc_x, dt_x = _ssd_project(u["x"], nb, seq, p, w, True)
            inner = w["ssd_w_out"].shape[0]
            s0 = jnp.zeros((nb, 2, SSD_GROUPS, SSD_STATE, inner // SSD_GROUPS), F32)
            yc, s_c = _ssd_scan(xbc_c, dt_c, s0, nb, lc, p)
            yx, _ = _ssd_scan(xbc_x, dt_x, s_c, nb, seq, p)
            res["x"] = _ssd_out(lay["xcol"], yx, zx, xbc_x, tail1("x"), p, w, alpha=alpha, name="ssd_out_x")
            if not last:
                res["c"] = _ssd_out(lay["c"], yc, zc, xbc_c, tail1("c"), p, w, alpha=alpha, name="ssd_out_c")
        else:
            qkv_c, zc, ab_c = _gdn_project(u["c"], nb, lc, p, w, False)
            qkv_x, zx, ab_x = _gdn_project(u["x"], nb, seq, p, w, True)
            vw = w["gdn_w_out"].shape[0]
            s0 = jnp.zeros((nb, 2, vw // GDN_HEAD, GDN_HEAD, GDN_HEAD), F32)
            oc, s_c = _gdn_scan(qkv_c, ab_c, s0, nb, lc, p)
            ox, _ = _gdn_scan(qkv_x, ab_x, s_c, nb, seq, p)
            res["x"] = _gdn_out(lay["xcol"], ox, zx, tail1("x"), p, w, alpha=alpha, name="gdn_out_x")
            if not last:
                res["c"] = _gdn_out(lay["c"], oc, zc, tail1("c"), p, w, alpha=alpha, name="gdn_out_c")

        for s in streams:
            t = dict(h=res[s]["h"], gate=chunk(i, 5, s), ln_g=p["ln_g"][i, 1], ln_b=p["ln_b"][i, 1])
            if not last:
                t.update(scale=chunk(i + 1, 1, s), shift=chunk(i + 1, 0, s))
            if moe:
                out = _ffn_call(lay[s], res[s]["u"], w["moe_w13"][i // 2], w["moe_w2"][i // 2], res[s]["comb"], t,
                                alpha=alpha, name=f"moe_ffn_{s}")
            else:
                out = _ffn_call(lay[s], res[s]["u"], w["ffn_w13"][i // 2][None], w["ffn_w2"][i // 2][None], None, t,
                                alpha=alpha, name=f"dense_ffn_{s}")
            h[s] = out["h"]
            if not last:
                u[s] = out["u"]
    return h["x"].reshape(nb, seq, d).astype(x.dtype)


def kernel(x, c, ctx, c_ctx, mod_w, mod_b, ln_g, ln_b, hy_w_in, hy_b_in, hy_conv_w, hy_conv_b, hy_f_w1, hy_f_b1, hy_f_w2, hy_f_b2, hy_f_freq, hy_f_w3, hy_decay, hy_skip, hy_w_out, hy_b_out, rw_mu, rw_w_rkv, rw_w0, rw_w1, rw_w2, rw_a0, rw_a1, rw_a2, rw_g1, rw_g2, rw_k_k, rw_k_a, rw_r_k, rw_lnx_g, rw_lnx_b, rw_w_out, ssd_w_in, ssd_conv_w, ssd_conv_b, ssd_dt_bias, ssd_A_log, ssd_d_skip, ssd_norm_g, ssd_w_out, gdn_w_in, gdn_conv_w, gdn_dt_bias, gdn_A_log, gdn_norm_g, gdn_w_out, ffn_w13, ffn_w2, moe_router, moe_w13, moe_w2):
    vals = (mod_w, mod_b, ln_g, ln_b, hy_w_in, hy_b_in, hy_conv_w, hy_conv_b, hy_f_w1, hy_f_b1, hy_f_w2, hy_f_b2,
            hy_f_freq, hy_f_w3, hy_decay, hy_skip, hy_w_out, hy_b_out, rw_mu, rw_w_rkv, rw_w0, rw_w1, rw_w2, rw_a0,
            rw_a1, rw_a2, rw_g1, rw_g2, rw_k_k, rw_k_a, rw_r_k, rw_lnx_g, rw_lnx_b, rw_w_out, ssd_w_in, ssd_conv_w,
            ssd_conv_b, ssd_dt_bias, ssd_A_log, ssd_d_skip, ssd_norm_g, ssd_w_out, gdn_w_in, gdn_conv_w,
            gdn_dt_bias, gdn_A_log, gdn_norm_g, gdn_w_out, ffn_w13, ffn_w2, moe_router, moe_w13, moe_w2)
    return _forward(x, c, ctx, c_ctx, dict(zip(_PARAM_NAMES, vals)))
```

```python
import functools
import math

import jax
import jax.numpy as jnp
import numpy as np
from jax import lax
from jax.experimental import pallas as pl
from jax.experimental.pallas import tpu as pltpu

F32 = jnp.float32
BF16 = jnp.bfloat16
HIGHEST = lax.Precision.HIGHEST

GRID_W = 64
LN_EPS = 1e-5
HY_BANDS = 8
RW_HEAD = 64
RW_GN_EPS = 64e-5
SSD_HEAD = 64
SSD_STATE = 128
SSD_GROUPS = 4
GDN_HEAD = 128
CHUNK = 64
TOP_K = 2

V7X_LANES = 128
V7X_SUBLANES = 8
V7X_BF16_ROWS = 16
V7X_VMEM_BUDGET = 56 * 1024 * 1024

NT_DIMS = (((1,), (1,)), ((), ()))
TN_DIMS = (((0,), (0,)), ((), ()))


def _tile(n, pref, mult):
    best = None
    t = mult
    while t <= min(n, pref):
        if n % t == 0:
            best = t
        t += mult
    return best if best is not None else n


def _pad_to(a, axis, size):
    pad = size - a.shape[axis]
    if pad == 0:
        return a
    cfg = [(0, 0)] * a.ndim
    cfg[axis] = (0, pad)
    return jnp.pad(a, cfg)


def _cparams(sem):
    return pltpu.CompilerParams(dimension_semantics=sem, vmem_limit_bytes=V7X_VMEM_BUDGET)


def _silu(x):
    return x * jax.nn.sigmoid(x)


def _softplus(x):
    return jnp.maximum(x, 0.0) + jnp.log1p(jnp.exp(-jnp.abs(x)))


def _bdot(a, b):
    return jnp.dot(a.astype(BF16), b.astype(BF16), preferred_element_type=F32)


def _bdot_nt(a, b):
    return lax.dot_general(a.astype(BF16), b.astype(BF16), NT_DIMS, preferred_element_type=F32)


def _hdot(a, b):
    return jnp.dot(a, b, precision=HIGHEST, preferred_element_type=F32)


def _split_dot(x, m):
    hi = x.astype(BF16)
    lo = (x - hi.astype(F32)).astype(BF16)
    return jnp.dot(hi, m, preferred_element_type=F32) + jnp.dot(lo, m, preferred_element_type=F32)


def _transpose_via_eye(x, eye):
    return lax.dot_general(eye, x, NT_DIMS, precision=HIGHEST, preferred_element_type=F32)


def _block_ones(n, blk):
    i = np.arange(n)
    return jnp.asarray((i[:, None] // blk == i[None, :] // blk).astype(np.float32), BF16)


def _expand_mat(n_in_pad, n_heads, width):
    m = np.zeros((n_in_pad, n_heads * width), np.float32)
    for h in range(n_heads):
        m[h, h * width:(h + 1) * width] = 1.0
    return jnp.asarray(m, BF16)


def _perm_mat(ib, jb):
    n = ib * jb
    q = np.arange(n)
    p = (q % ib) * jb + q // ib
    m = np.zeros((n, n), np.float32)
    m[q, p] = 1.0
    return jnp.asarray(m, BF16)


def _time_masks(d, c):
    ii = lax.broadcasted_iota(jnp.int32, (c, c), 0)
    jj = lax.broadcasted_iota(jnp.int32, (c, c), 1)
    lag = (ii - jj) * (1 - 2 * d)
    return lag >= 0, lag > 0


def _mm_kernel(*refs, nk, pre, act, has_bias, has_perm):
    it = iter(refs)
    x_ref, w_ref = next(it), next(it)
    b_ref = next(it) if has_bias else None
    p_ref = next(it) if has_perm else None
    o_ref = next(it)
    acc_ref = next(it) if nk > 1 else None
    xp_ref = next(it) if has_perm else None

    def load_x():
        x = x_ref[...]
        x = x.reshape(-1, x.shape[-1])
        if pre is not None:
            x = pre(x.astype(F32))
        return x.astype(BF16)

    if has_perm:
        @pl.when(pl.program_id(3) == 0)
        def _():
            xp_ref[...] = jnp.dot(p_ref[...], load_x(), preferred_element_type=F32).astype(BF16)

        x = xp_ref[...]
    else:
        x = load_x()
    part = jnp.dot(x, w_ref[...], preferred_element_type=F32)

    def finish(r):
        if has_bias:
            r = r + b_ref[...]
        if act is not None:
            r = act(r)
        o_ref[...] = r.astype(o_ref.dtype).reshape(o_ref.shape)

    if nk == 1:
        finish(part)
    else:
        k = pl.program_id(2)

        @pl.when(k == 0)
        def _():
            acc_ref[...] = jnp.zeros_like(acc_ref)

        acc_ref[...] += part

        @pl.when(k == nk - 1)
        def _():
            finish(acc_ref[...])


def _matmul(x, w, bias=None, *, out_dtype=F32, act=None, pre=None, tm=512, tn=512, tk=None, name="matmul"):
    m, kdim = x.shape
    n = w.shape[1]
    tm = _tile(m, tm, V7X_BF16_ROWS)
    tn = _tile(n, tn, V7X_LANES)
    tk = kdim if tk is None else _tile(kdim, tk, V7X_LANES)
    nk = kdim // tk
    in_specs = [pl.BlockSpec((tm, tk), lambda i, j, k: (i, k)),
                pl.BlockSpec((tk, tn), lambda i, j, k: (k, j))]
    args = [x, w]
    if bias is not None:
        in_specs.append(pl.BlockSpec((1, tn), lambda i, j, k: (0, j)))
        args.append(bias.reshape(1, n).astype(F32))
    return pl.pallas_call(
        functools.partial(_mm_kernel, nk=nk, pre=pre, act=act, has_bias=bias is not None, has_perm=False),
        grid=(m // tm, n // tn, nk),
        in_specs=in_specs,
        out_specs=pl.BlockSpec((tm, tn), lambda i, j, k: (i, j)),
        out_shape=jax.ShapeDtypeStruct((m, n), out_dtype),
        scratch_shapes=[pltpu.VMEM((tm, tn), F32)] if nk > 1 else [],
        compiler_params=_cparams(("parallel", "parallel", "arbitrary")),
        name=name,
    )(*args)


def _col_tiles(gw):
    ib = min(V7X_BF16_ROWS, gw)
    jb = min(32, gw)
    return ib, jb


def _matmul_r2c(u, w, bias, nb, *, out_dtype, tn=512, name="matmul_r2c"):
    gw = GRID_W
    kdim, n = w.shape
    ib, jb = _col_tiles(gw)
    tn = _tile(n, tn, V7X_LANES)
    u4 = u.reshape(nb, gw, gw, kdim)
    in_specs = [pl.BlockSpec((1, ib, jb, kdim), lambda b, i, j, c: (b, i, j, 0)),
                pl.BlockSpec((kdim, tn), lambda b, i, j, c: (0, c))]
    args = [u4, w]
    if bias is not None:
        in_specs.append(pl.BlockSpec((1, tn), lambda b, i, j, c: (0, c)))
        args.append(bias.reshape(1, n).astype(F32))
    in_specs.append(pl.BlockSpec((ib * jb, ib * jb), lambda b, i, j, c: (0, 0)))
    args.append(_perm_mat(ib, jb))
    out = pl.pallas_call(
        functools.partial(_mm_kernel, nk=1, pre=None, act=None, has_bias=bias is not None, has_perm=True),
        grid=(nb, gw // ib, gw // jb, n // tn),
        in_specs=in_specs,
        out_specs=pl.BlockSpec((1, jb, ib, tn), lambda b, i, j, c: (b, j, i, c)),
        out_shape=jax.ShapeDtypeStruct((nb, gw, gw, n), out_dtype),
        scratch_shapes=[pltpu.VMEM((ib * jb, kdim), BF16)],
        compiler_params=_cparams(("parallel", "parallel", "parallel", "arbitrary")),
        name=name,
    )(*args)
    return out.reshape(nb * gw * gw, n)


def _ln_epilogue(h, y, gate, ln_g, ln_b, alpha):
    pre = alpha * h + gate * y
    mu = jnp.mean(pre, axis=-1, keepdims=True)
    xc = pre - mu
    var = jnp.mean(xc * xc, axis=-1, keepdims=True)
    return xc * lax.rsqrt(var + LN_EPS) * ln_g + ln_b


def _route(logits, n_exp):
    lane = lax.broadcasted_iota(jnp.int32, logits.shape, 1)
    neg = jnp.float32(-jnp.inf)
    lg = jnp.where(lane < n_exp, logits, neg)
    big = jnp.int32(logits.shape[1])
    m1 = jnp.max(lg, axis=-1, keepdims=True)
    i1 = jnp.min(jnp.where(lg == m1, lane, big), axis=-1, keepdims=True)
    lg2 = jnp.where(lane == i1, neg, lg)
    m2 = jnp.max(lg2, axis=-1, keepdims=True)
    i2 = jnp.min(jnp.where(lg2 == m2, lane, big), axis=-1, keepdims=True)
    e2 = jnp.exp(m2 - m1)
    den = 1.0 + e2
    return jnp.where(lane == i1, 1.0 / den, jnp.where(lane == i2, e2 / den, 0.0))


def _finish_rows(refs, y, *, alpha, emit_u, n_exp):
    h_ref, gate_ref, lng_ref, lnb_ref = refs["h"], refs["gate"], refs["ln_g"], refs["ln_b"]
    h = h_ref[...].reshape(y.shape)
    hn = _ln_epilogue(h, y, gate_ref[0], lng_ref[...], lnb_ref[...], alpha)
    refs["h_out"][...] = hn.reshape(refs["h_out"].shape)
    if emit_u:
        u = hn * (1.0 + refs["scale"][0]) + refs["shift"][0]
        refs["u_out"][...] = u.astype(BF16).reshape(refs["u_out"].shape)
        if n_exp:
            logits = _hdot(u, refs["router"][...])
            refs["comb_out"][...] = _route(logits, n_exp).reshape(refs["comb_out"].shape)


class _RowLayout:
    def __init__(self, nb, seq, per_batch_mod, mode="rows", tm=512):
        self.nb, self.seq, self.mode, self.per_batch_mod = nb, seq, mode, per_batch_mod
        if mode == "rows":
            self.tm = _tile(seq, tm, V7X_BF16_ROWS)
            self.tps = seq // self.tm
            self.grid = (nb * self.tps,)
            self.rows = self.tm
        else:
            self.ib, self.jb = _col_tiles(GRID_W)
            self.grid = (nb, GRID_W // self.ib, GRID_W // self.jb)
            self.rows = self.ib * self.jb
        self.ngrid = len(self.grid)

    def sem(self, extra=()):
        return ("parallel",) * self.ngrid + tuple(extra)

    def _ix(self, fn):
        n = self.ngrid
        return lambda *g: fn(*g[:n])

    def raster(self, arr, c, cblock=0, lead=None):
        pre_shape = () if lead is None else (arr.shape[0],)
        pre_blk = () if lead is None else (1,)
        pre_ix = () if lead is None else (lead,)
        if self.mode == "rows":
            return arr, pl.BlockSpec(pre_blk + (self.tm, c), self._ix(lambda i: pre_ix + (i, cblock)))
        a4 = arr.reshape(pre_shape + (self.nb, GRID_W, GRID_W, arr.shape[-1]))
        return a4, pl.BlockSpec(pre_blk + (1, self.ib, self.jb, c),
                                self._ix(lambda b, i, j: pre_ix + (b, i, j, cblock)))

    def colmajor(self, arr, c, cblock=0, lead=None):
        assert self.mode == "cols"
        pre_shape = () if lead is None else (arr.shape[0],)
        pre_blk = () if lead is None else (1,)
        pre_ix = () if lead is None else (lead,)
        a4 = arr.reshape(pre_shape + (self.nb, GRID_W, GRID_W, arr.shape[-1]))
        return a4, pl.BlockSpec(pre_blk + (1, self.jb, self.ib, c),
                                self._ix(lambda b, i, j: pre_ix + (b, j, i, cblock)))

    def native(self, arr, c, cblock=0, lead=None):
        if self.mode == "cols":
            return self.colmajor(arr, c, cblock, lead)
        return self.raster(arr, c, cblock, lead)

    def mod(self, arr):
        d = arr.shape[-1]
        if not self.per_batch_mod:
            return arr, pl.BlockSpec((1, 1, d), self._ix(lambda *g: (0, 0, 0)))
        if self.mode == "rows":
            tps = self.tps
            return arr, pl.BlockSpec((1, 1, d), self._ix(lambda i: (i // tps, 0, 0)))
        return arr, pl.BlockSpec((1, 1, d), self._ix(lambda b, i, j: (b, 0, 0)))

    def const(self, arr):
        nd = arr.ndim
        return arr, pl.BlockSpec(arr.shape, self._ix(lambda *g: (0,) * nd))

    def out_raster(self, n_rows, c, dtype):
        if self.mode == "rows":
            return (jax.ShapeDtypeStruct((n_rows, c), dtype),
                    pl.BlockSpec((self.tm, c), self._ix(lambda i: (i, 0))))
        return (jax.ShapeDtypeStruct((self.nb, GRID_W, GRID_W, c), dtype),
                pl.BlockSpec((1, self.ib, self.jb, c), self._ix(lambda b, i, j: (b, i, j, 0))))


def _out_proj_call(lay, prologue, pro_inputs, w_out, bias, tail, *, alpha, name):
    d = w_out.shape[1]
    n_rows = lay.nb * lay.seq
    emit_u = "scale" in tail
    n_exp = tail.get("n_exp", 0)
    names, args, specs = [], [], []

    def add(nm, pair):
        names.append(nm)
        args.append(pair[0])
        specs.append(pair[1])

    for k, pair in enumerate(pro_inputs):
        add(f"p{k}", pair)
    add("w", lay.const(w_out))
    if bias is not None:
        add("bias", lay.const(bias.reshape(1, d).astype(F32)))
    if lay.mode == "cols":
        add("perm", lay.const(_perm_mat(lay.ib, lay.jb).T))
    add("h", lay.raster(tail["h"], d))
    add("gate", lay.mod(tail["gate"]))
    add("ln_g", lay.const(tail["ln_g"].reshape(1, d)))
    add("ln_b", lay.const(tail["ln_b"].reshape(1, d)))
    if emit_u:
        add("scale", lay.mod(tail["scale"]))
        add("shift", lay.mod(tail["shift"]))
    if n_exp:
        add("router", lay.const(tail["router"]))
    out_names, out_shapes, out_specs = [], [], []

    def add_out(nm, pair):
        out_names.append(nm)
        out_shapes.append(pair[0])
        out_specs.append(pair[1])

    add_out("h_out", lay.out_raster(n_rows, d, F32))
    if emit_u:
        add_out("u_out", lay.out_raster(n_rows, d, BF16))
    if n_exp:
        add_out("comb_out", lay.out_raster(n_rows, V7X_LANES, F32))
    n_pro = len(pro_inputs)
    n_in = len(names)

    def kernel(*refs):
        r = dict(zip(names + out_names, refs))
        z = prologue(*[refs[k] for k in range(n_pro)])
        z = z.astype(BF16)
        if lay.mode == "cols":
            z = jnp.dot(r["perm"][...], z, preferred_element_type=F32).astype(BF16)
        y = jnp.dot(z, r["w"][...], preferred_element_type=F32)
        if bias is not None:
            y = y + r["bias"][...]
        _finish_rows(r, y, alpha=alpha, emit_u=emit_u, n_exp=n_exp)

    outs = pl.pallas_call(
        kernel, grid=lay.grid, in_specs=specs, out_specs=out_specs, out_shape=out_shapes,
        compiler_params=_cparams(lay.sem()), name=name)(*args)
    outs = [o.reshape(n_rows, o.shape[-1]) for o in outs]
    res = {"h": outs[0]}
    if emit_u:
        res["u"] = outs[1]
    if n_exp:
        res["comb"] = outs[2]
    return res


def _modulate_kernel(h_ref, scale_ref, shift_ref, u_ref):
    u_ref[...] = (h_ref[...] * (1.0 + scale_ref[0]) + shift_ref[0]).astype(u_ref.dtype)


def _modulate(lay, h, scale, shift):
    n_rows, d = h.shape
    pairs = [lay.raster(h, d), lay.mod(scale), lay.mod(shift)]
    out = lay.out_raster(n_rows, d, BF16)
    return pl.pallas_call(
        _modulate_kernel, grid=lay.grid, in_specs=[pr[1] for pr in pairs], out_specs=out[1], out_shape=out[0],
        compiler_params=_cparams(lay.sem()), name="adaln_modulate")(*[pr[0] for pr in pairs])


def _ffn_kernel(*refs, names, n_f, alpha, emit_u):
    r = dict(zip(names, refs))
    f = pl.program_id(1)

    @pl.when(f == 0)
    def _():
        r["acc"][...] = jnp.zeros_like(r["acc"])

    u = r["u"][...]
    gate = jnp.dot(u, r["w1"][...], preferred_element_type=F32)
    up = jnp.dot(u, r["w3"][...], preferred_element_type=F32)
    hid = _silu(gate) * up
    r["acc"][...] += jnp.dot(hid.astype(BF16), r["w2"][...], preferred_element_type=F32)

    @pl.when(f == n_f - 1)
    def _():
        _finish_rows(r, r["acc"][...], alpha=alpha, emit_u=emit_u, n_exp=0)


def _ffn_call(lay, u, w13, w2, tail, *, alpha, name):
    d, f2 = w13.shape
    fdim = f2 // 2
    tf = _tile(fdim, 1536, V7X_LANES)
    n_f = fdim // tf
    n_rows = lay.nb * lay.seq
    emit_u = "scale" in tail
    names, args, specs = [], [], []

    def add(nm, pair):
        names.append(nm)
        args.append(pair[0])
        specs.append(pair[1])

    add("u", lay.raster(u, d))
    add("w1", (w13, pl.BlockSpec((d, tf), lambda i, f: (0, f))))
    add("w3", (w13, pl.BlockSpec((d, tf), lambda i, f: (0, n_f + f))))
    add("w2", (w2, pl.BlockSpec((tf, d), lambda i, f: (f, 0))))
    add("h", lay.raster(tail["h"], d))
    add("gate", lay.mod(tail["gate"]))
    add("ln_g", lay.const(tail["ln_g"].reshape(1, d)))
    add("ln_b", lay.const(tail["ln_b"].reshape(1, d)))
    if emit_u:
        add("scale", lay.mod(tail["scale"]))
        add("shift", lay.mod(tail["shift"]))
    out_names = ["h_out"] + (["u_out"] if emit_u else [])
    outs = [lay.out_raster(n_rows, d, F32)] + ([lay.out_raster(n_rows, d, BF16)] if emit_u else [])
    kernel = functools.partial(_ffn_kernel, names=names + out_names + ["acc"], n_f=n_f, alpha=alpha, emit_u=emit_u)
    res = pl.pallas_call(
        kernel, grid=lay.grid + (n_f,), in_specs=specs,
        out_specs=[o[1] for o in outs], out_shape=[o[0] for o in outs],
        scratch_shapes=[pltpu.VMEM((lay.rows, d), F32)],
        compiler_params=_cparams(lay.sem(("arbitrary",))), name=name)(*args)
    out = {"h": res[0]}
    if emit_u:
        out["u"] = res[1]
    return out


MOE_TOKENS = 1024
MOE_ROWS = 256


def _moe_kernel(cnt_ref, u_ref, comb_ref, rank_ref, rankt_ref, w1_ref, w3_ref, w2_ref, out_ref, xe_ref, ye_ref, cw_ref,
                *, n_f, tb, rc):
    blk, e, f = pl.program_id(0), pl.program_id(1), pl.program_id(2)
    n_groups = lax.shift_right_logical(cnt_ref[blk, e] + (rc - 1), rc.bit_length() - 1)
    lane = lax.broadcasted_iota(jnp.int32, (rc, V7X_LANES), 1)

    @pl.when((e == 0) & (f == 0))
    def _():
        out_ref[...] = jnp.zeros_like(out_ref)

    @pl.when(f == 0)
    def _():
        want = rankt_ref[0, pl.ds(e, 1), :]
        comb = comb_ref[...]
        comb_hi = comb.astype(BF16)
        comb_lo = (comb - comb_hi.astype(F32)).astype(BF16)
        slot = lax.broadcasted_iota(jnp.int32, (rc, tb), 0).astype(F32)

        def gather(g, carry):
            rows = pl.ds(pl.multiple_of(g * rc, rc), rc)
            onehot = (want == slot + (g * rc).astype(F32)).astype(BF16)
            xe_ref[rows, :] = jnp.dot(onehot, u_ref[...], preferred_element_type=F32).astype(BF16)
            cw_ref[rows, :] = (jnp.dot(onehot, comb_hi, preferred_element_type=F32)
                               + jnp.dot(onehot, comb_lo, preferred_element_type=F32))
            ye_ref[rows, :] = jnp.zeros((rc, ye_ref.shape[1]), F32)
            return carry

        lax.fori_loop(0, n_groups, gather, 0)

    def expert(g, carry):
        rows = pl.ds(pl.multiple_of(g * rc, rc), rc)
        x = xe_ref[rows, :]
        hid = _silu(jnp.dot(x, w1_ref[0], preferred_element_type=F32)) * jnp.dot(x, w3_ref[0], preferred_element_type=F32)
        cw = jnp.sum(jnp.where(lane == e, cw_ref[rows, :], 0.0), axis=-1, keepdims=True)
        ye_ref[rows, :] += jnp.dot((hid * cw).astype(BF16), w2_ref[0], preferred_element_type=F32)
        return carry

    lax.fori_loop(0, n_groups, expert, 0)

    @pl.when(f == n_f - 1)
    def _():
        lane_t = lax.broadcasted_iota(jnp.int32, (tb, V7X_LANES), 1)
        want_col = jnp.sum(jnp.where(lane_t == e, rank_ref[...], 0.0), axis=-1, keepdims=True)
        slot_t = lax.broadcasted_iota(jnp.int32, (tb, rc), 1).astype(F32)

        def scatter(g, carry):
            rows = pl.ds(pl.multiple_of(g * rc, rc), rc)
            onehot_t = (want_col == slot_t + (g * rc).astype(F32)).astype(BF16)
            out_ref[...] += jnp.dot(onehot_t, ye_ref[rows, :].astype(BF16), preferred_element_type=F32)
            return carry

        lax.fori_loop(0, n_groups, scatter, 0)


def _moe_call(lay, u, w13, w2, comb, tail, *, alpha, name):
    n_e, d, f2 = w13.shape
    fdim = f2 // 2
    tf = _tile(fdim, 1536, V7X_LANES)
    n_f = fdim // tf
    n_rows = u.shape[0]
    rc = min(MOE_ROWS, n_rows)
    tb = _tile(n_rows, MOE_TOKENS, rc)
    n_blk = n_rows // tb
    lanes = V7X_LANES
    assert n_e == V7X_SUBLANES and tb % rc == 0 and rc & (rc - 1) == 0
    routed = (comb[:, :n_e] > 0.0).astype(jnp.int32).reshape(n_blk, tb, n_e)
    rank = jnp.where(routed > 0, jnp.cumsum(routed, axis=1) - routed, -1).astype(F32)
    cnt = jnp.sum(routed, axis=1)
    rank_col = jnp.pad(rank.reshape(n_rows, n_e), ((0, 0), (0, lanes - n_e)), constant_values=-1.0)
    rank_row = jnp.transpose(rank, (0, 2, 1))
    grid_spec = pltpu.PrefetchScalarGridSpec(
        num_scalar_prefetch=1, grid=(n_blk, n_e, n_f),
        in_specs=[pl.BlockSpec((tb, d), lambda i, e, f, c: (i, 0)),
                  pl.BlockSpec((tb, lanes), lambda i, e, f, c: (i, 0)),
                  pl.BlockSpec((tb, lanes), lambda i, e, f, c: (i, 0)),
                  pl.BlockSpec((1, n_e, tb), lambda i, e, f, c: (i, 0, 0)),
                  pl.BlockSpec((1, d, tf), lambda i, e, f, c: (e, 0, f)),
                  pl.BlockSpec((1, d, tf), lambda i, e, f, c: (e, 0, n_f + f)),
                  pl.BlockSpec((1, tf, d), lambda i, e, f, c: (e, f, 0))],
        out_specs=pl.BlockSpec((tb, d), lambda i, e, f, c: (i, 0)),
        scratch_shapes=[pltpu.VMEM((tb, d), BF16), pltpu.VMEM((tb, d), F32), pltpu.VMEM((tb, lanes), F32)])
    y = pl.pallas_call(
        functools.partial(_moe_kernel, n_f=n_f, tb=tb, rc=rc), grid_spec=grid_spec,
        out_shape=jax.ShapeDtypeStruct((n_rows, d), F32),
        compiler_params=_cparams(("parallel", "arbitrary", "arbitrary")), name=name,
    )(cnt, u, comb, rank_col, rank_row, w13, w13, w2)
    return _tail_call(lay, y, tail, alpha=alpha, name=name + "_tail")


def _tail_call(lay, y, tail, *, alpha, name):
    n_rows, d = y.shape
    emit_u = "scale" in tail
    names, args, specs = [], [], []

    def add(nm, pair):
        names.append(nm)
        args.append(pair[0])
        specs.append(pair[1])

    add("y", lay.raster(y, d))
    add("h", lay.raster(tail["h"], d))
    add("gate", lay.mod(tail["gate"]))
    add("ln_g", lay.const(tail["ln_g"].reshape(1, d)))
    add("ln_b", lay.const(tail["ln_b"].reshape(1, d)))
    if emit_u:
        add("scale", lay.mod(tail["scale"]))
        add("shift", lay.mod(tail["shift"]))
    out_names = ["h_out"] + (["u_out"] if emit_u else [])
    outs = [lay.out_raster(n_rows, d, F32)] + ([lay.out_raster(n_rows, d, BF16)] if emit_u else [])

    def kernel(*refs):
        r = dict(zip(names + out_names, refs))
        _finish_rows(r, r["y"][...], alpha=alpha, emit_u=emit_u, n_exp=0)

    res = pl.pallas_call(
        kernel, grid=lay.grid, in_specs=specs, out_specs=[o[1] for o in outs], out_shape=[o[0] for o in outs],
        compiler_params=_cparams(lay.sem()), name=name)(*args)
    out = {"h": res[0]}
    if emit_u:
        out["u"] = res[1]
    return out


def _shift_rows(cur, prev_row, next_row):
    n = cur.shape[0]
    rows = lax.broadcasted_iota(jnp.int32, cur.shape, 0)
    up = jnp.where(rows == 0, prev_row, pltpu.roll(cur, 1, 0))
    dn = jnp.where(rows == n - 1, next_row, pltpu.roll(cur, n - 1, 0))
    return up, dn


def _halo_rows(i, tps, xp_ref, xn_ref, hb):
    t = i % tps
    prev_row = jnp.where(t == 0, 0.0, xp_ref[hb - 1:hb, :].astype(F32))
    next_row = jnp.where(t == tps - 1, 0.0, xn_ref[0:1, :].astype(F32))
    return prev_row, next_row


def _halo_rows_index(tm, n_rows, hb):
    r = tm // hb
    last = n_rows // hb - 1
    return (lambda i: jnp.maximum(i * r - 1, 0)), (lambda i: jnp.minimum((i + 1) * r, last))


def _conv3_kernel(x_ref, xp_ref, xn_ref, w_ref, b_ref, o_ref, *, tps, act):
    i = pl.program_id(0)
    cur = x_ref[...].astype(F32)
    prev_row, next_row = _halo_rows(i, tps, xp_ref, xn_ref, V7X_SUBLANES)
    up, dn = _shift_rows(cur, prev_row, next_row)
    w = w_ref[...]
    y = w[0:1] * up + w[1:2] * cur + w[2:3] * dn + b_ref[...]
    if act is not None:
        y = act(y)
    o_ref[...] = y.astype(o_ref.dtype).reshape(o_ref.shape)


def _conv3(x, w, b, seq, *, act=None, out_dtype=BF16, time_major_nb=None, name="conv3"):
    n_rows, c = x.shape
    tm = _tile(seq, 512, V7X_SUBLANES)
    tps = seq // tm
    if b is None:
        b = jnp.zeros((c,), F32)
    if time_major_nb is None:
        ct = _tile(c, 1024, V7X_LANES)
        out_shape = jax.ShapeDtypeStruct((n_rows, c), out_dtype)
        out_spec = pl.BlockSpec((tm, ct), lambda i, cc: (i, cc))
    else:
        nb, d = time_major_nb
        ct = d
        out_shape = jax.ShapeDtypeStruct((c // d, seq, nb * d), out_dtype)
        out_spec = pl.BlockSpec((1, tm, d), lambda i, cc: (cc, i % tps, i // tps))
    hb = V7X_SUBLANES
    prev, nxt = _halo_rows_index(tm, n_rows, hb)
    return pl.pallas_call(
        functools.partial(_conv3_kernel, tps=tps, act=act),
        grid=(n_rows // tm, c // ct),
        in_specs=[pl.BlockSpec((tm, ct), lambda i, cc: (i, cc)),
                  pl.BlockSpec((hb, ct), lambda i, cc: (prev(i), cc)),
                  pl.BlockSpec((hb, ct), lambda i, cc: (nxt(i), cc)),
                  pl.BlockSpec((3, ct), lambda i, cc: (0, cc)),
                  pl.BlockSpec((1, ct), lambda i, cc: (0, cc))],
        out_specs=out_spec, out_shape=out_shape,
        compiler_params=_cparams(("parallel", "parallel")), name=name,
    )(x, x, x, w.astype(F32), b.reshape(1, c).astype(F32))


def _hy_filter_kernel(bands_ref, w1_ref, b1_ref, w2_ref, b2_ref, fr_ref, w3_ref, dec_ref, sum_ref, dif_ref,
                      *, seq, tl, d):
    i = pl.program_id(0)
    pos = (lax.broadcasted_iota(jnp.int32, (tl, V7X_LANES), 0) + i * tl).astype(F32)
    lane = lax.broadcasted_iota(jnp.int32, (tl, V7X_LANES), 1)
    t01 = pos / float(max(seq - 1, 1))
    ang = (2.0 * math.pi / seq) * pos * bands_ref[...]
    feats = jnp.where(lane == 0, t01, jnp.where(lane <= HY_BANDS, jnp.cos(ang), -jnp.sin(ang)))
    fr = fr_ref[...]
    h = jnp.sin(fr[0:1] * (_hdot(feats, w1_ref[...]) + b1_ref[...]))
    h = jnp.sin(fr[1:2] * (_hdot(h, w2_ref[...]) + b2_ref[...]))
    k = _hdot(h, w3_ref[...]) * jnp.exp(-t01[:, 0:1] * jnp.abs(dec_ref[...]))
    not_first = (pos[:, 0:1] > 0.0).astype(F32)
    for o in range(2):
        kf = k[:, (2 * o) * d:(2 * o + 1) * d]
        kb = k[:, (2 * o + 1) * d:(2 * o + 2) * d] * not_first
        sum_ref[:, o * d:(o + 1) * d] = (kf + kb).astype(sum_ref.dtype)
        dif_ref[:, o * d:(o + 1) * d] = (kb - kf).astype(dif_ref.dtype)


def _hy_filters(seq, p, d):
    lanes = V7X_LANES
    fw = p["hy_f_w1"].shape[1]
    bands = jnp.linspace(1e-4, HY_BANDS - 1, HY_BANDS, dtype=F32)
    bands_row = _pad_to(jnp.concatenate([jnp.zeros((1,), F32), bands, bands])[None, :], 1, lanes)
    w1 = _pad_to(_pad_to(p["hy_f_w1"].astype(F32), 0, lanes), 1, lanes)
    b1 = _pad_to(p["hy_f_b1"].astype(F32)[None, :], 1, lanes)
    w2 = _pad_to(_pad_to(p["hy_f_w2"].astype(F32), 0, lanes), 1, lanes)
    b2 = _pad_to(p["hy_f_b2"].astype(F32)[None, :], 1, lanes)
    fr = _pad_to(_pad_to(p["hy_f_freq"].astype(F32), 1, lanes), 0, V7X_SUBLANES)
    w3 = _pad_to(p["hy_f_w3"].astype(F32), 0, lanes)
    dec = p["hy_decay"].astype(F32).reshape(1, 4 * d)
    assert fw <= lanes
    tl = _tile(seq, 256, V7X_BF16_ROWS)
    full = lambda a: pl.BlockSpec(a.shape, lambda i: (0,) * a.ndim)
    ins = [bands_row, w1, b1, w2, b2, fr, w3, dec]
    return pl.pallas_call(
        functools.partial(_hy_filter_kernel, seq=seq, tl=tl, d=d),
        grid=(seq // tl,),
        in_specs=[full(a) for a in ins],
        out_specs=[pl.BlockSpec((tl, 2 * d), lambda i: (i, 0))] * 2,
        out_shape=[jax.ShapeDtypeStruct((seq, 2 * d), BF16)] * 2,
        compiler_params=_cparams(("parallel",)), name="hyena_filters",
    )(*ins)


def _dft_kernel(c_ref, s_ref, ct_ref, st_ref, *, seq, tr):
    i = pl.program_id(0)
    row = lax.broadcasted_iota(jnp.int32, (tr, seq), 0) + i * tr
    col = lax.broadcasted_iota(jnp.int32, (tr, seq), 1)
    scale = math.pi / (2 * seq)
    m = ((2 * row + 1) * col) & (4 * seq - 1)
    ang = m.astype(F32) * scale
    c_ref[...] = jnp.cos(ang).astype(BF16)
    s_ref[...] = jnp.sin(ang).astype(BF16)
    mt = ((2 * col + 1) * row) & (4 * seq - 1)
    angt = mt.astype(F32) * scale
    ct_ref[...] = jnp.cos(angt).astype(BF16)
    st_ref[...] = jnp.sin(angt).astype(BF16)


def _dft_mats(seq):
    assert seq & (seq - 1) == 0, "token count must be a power of two"
    tr = _tile(seq, 256, V7X_BF16_ROWS)
    spec = pl.BlockSpec((tr, seq), lambda i: (i, 0))
    return pl.pallas_call(
        functools.partial(_dft_kernel, seq=seq, tr=tr), grid=(seq // tr,), in_specs=[],
        out_specs=[spec] * 4, out_shape=[jax.ShapeDtypeStruct((seq, seq), BF16)] * 4,
        compiler_params=_cparams(("parallel",)), name="dft_matrices")()


def _hy_fwd_kernel(c_ref, s_ref, v_ref, kr_ref, ki_ref, wr_ref, wi_ref):
    v = v_ref[0]
    cv = jnp.dot(c_ref[...], v, preferred_element_type=F32)
    sv = jnp.dot(s_ref[...], v, preferred_element_type=F32)
    kr, ki = kr_ref[...], ki_ref[...]
    wr_ref[...] = (cv * kr + sv * ki).astype(wr_ref.dtype)
    wi_ref[...] = (cv * ki - sv * kr).astype(wi_ref.dtype)


def _hy_inv_kernel(ct_ref, st_ref, wr_ref, wi_ref, v_ref, g_ref, skip_ref, o_ref, *, seq):
    y = (jnp.dot(ct_ref[...], wr_ref[...], preferred_element_type=F32)
         - jnp.dot(st_ref[...], wi_ref[...], preferred_element_type=F32)) * (1.0 / seq)
    y = y + v_ref[0].astype(F32) * skip_ref[...]
    o_ref[0] = (g_ref[0].astype(F32) * y).astype(o_ref.dtype)


def _hy_long_conv(vsrc, v_idx, gsrc, g_idx, mats, kr, ki, order, skip, d):
    c, s, ct, st = mats
    _, seq, cols = vsrc.shape
    tm = _tile(seq, 512, V7X_BF16_ROWS)
    tn = _tile(d, 512, V7X_LANES)
    cpd = d // tn
    kspec = pl.BlockSpec((tm, tn), lambda i, j: (i, order * cpd + j % cpd))
    wr, wi = pl.pallas_call(
        _hy_fwd_kernel, grid=(seq // tm, cols // tn),
        in_specs=[pl.BlockSpec((tm, seq), lambda i, j: (i, 0)), pl.BlockSpec((tm, seq), lambda i, j: (i, 0)),
                  pl.BlockSpec((1, seq, tn), lambda i, j: (v_idx, 0, j)), kspec, kspec],
        out_specs=[pl.BlockSpec((tm, tn), lambda i, j: (i, j))] * 2,
        out_shape=[jax.ShapeDtypeStruct((seq, cols), BF16)] * 2,
        compiler_params=_cparams(("parallel", "parallel")), name="hyena_dft_fwd",
    )(c, s, vsrc, kr, ki)
    return pl.pallas_call(
        functools.partial(_hy_inv_kernel, seq=seq), grid=(seq // tm, cols // tn),
        in_specs=[pl.BlockSpec((tm, seq), lambda i, j: (i, 0)), pl.BlockSpec((tm, seq), lambda i, j: (i, 0)),
                  pl.BlockSpec((seq, tn), lambda i, j: (0, j)), pl.BlockSpec((seq, tn), lambda i, j: (0, j)),
                  pl.BlockSpec((1, tm, tn), lambda i, j: (v_idx, i, j)),
                  pl.BlockSpec((1, tm, tn), lambda i, j: (g_idx, i, j)),
                  pl.BlockSpec((1, tn), lambda i, j: (0, j % cpd))],
        out_specs=pl.BlockSpec((1, tm, tn), lambda i, j: (0, i, j)),
        out_shape=jax.ShapeDtypeStruct((1, seq, cols), BF16),
        compiler_params=_cparams(("parallel", "parallel")), name="hyena_dft_inv",
    )(ct, st, wr, wi, vsrc, gsrc, skip.reshape(1, d).astype(F32))


def _hyena_mixer(u, nb, seq, p, w):
    d = u.shape[1]
    proj = _matmul(u, w["hy_w_in"], p["hy_b_in"], out_dtype=F32, name="hyena_in_proj")
    planes = _conv3(proj, p["hy_conv_w"], p["hy_conv_b"], seq, time_major_nb=(nb, d), name="hyena_short_conv")
    ksum, kdif = _hy_filters(seq, p, d)
    mats = _dft_mats(seq)
    kr = _matmul(mats[0], ksum, out_dtype=F32, name="hyena_filter_spec_re")
    ki = _matmul(mats[1], kdif, out_dtype=F32, name="hyena_filter_spec_im")
    z1 = _hy_long_conv(planes, 0, planes, 1, mats, kr, ki, 0, p["hy_skip"][0], d)
    return _hy_long_conv(z1, 0, planes, 2, mats, kr, ki, 1, p["hy_skip"][1], d)


def _hyena_out(lay_rows, z, tail, p, w, nb, seq, *, alpha, name):
    d = w["hy_w_out"].shape[0]
    tm = lay_rows.tm
    tps = lay_rows.tps
    spec = pl.BlockSpec((1, tm, d), lay_rows._ix(lambda i: (0, i % tps, i // tps)))
    return _out_proj_call(lay_rows, lambda z_ref: z_ref[0], [(z, spec)], w["hy_w_out"], p["hy_b_out"], tail,
                          alpha=alpha, name=name)


def _rw_proj_kernel(*refs, names, tps, hd):
    r = dict(zip(names, refs))
    i = pl.program_id(0)
    cur = r["u"][...].astype(F32)
    prev_row, next_row = _halo_rows(i, tps, r["up"], r["un"], V7X_BF16_ROWS)
    up, dn = _shift_rows(cur, prev_row, next_row)
    xx = 0.5 * (up + dn) - cur
    mu = r["mu"][...]
    mix = lambda j: (cur + xx * mu[j:j + 1]).astype(BF16)
    xr, xw, xk, xv, xa, xg = [mix(j) for j in range(6)]
    qq = r["qq"][...]
    rr = jnp.dot(xr, r["w_rkv"][0], preferred_element_type=F32)
    kk0 = jnp.dot(xk, r["w_rkv"][1], preferred_element_type=F32)
    vv = jnp.dot(xv, r["w_rkv"][2], preferred_element_type=F32)
    gg = _bdot(jax.nn.sigmoid(jnp.dot(xg, r["g1"][...], preferred_element_type=F32)), r["g2"][...])
    kx = kk0 * r["k_k"][...]
    kkn = kx * lax.rsqrt(_split_dot(kx * kx, qq) + 1e-6)
    r["r"][...] = rr
    r["v"][...] = vv
    r["g"][...] = gg.astype(BF16)
    r["na"][...] = -kkn
    kd_sum = jnp.zeros_like(kk0)
    for dd in range(2):
        hw = jnp.tanh(jnp.dot(xw, r["lora_w1"][dd], preferred_element_type=F32))
        wpre = r["bias_w"][dd] + _bdot(hw, r["lora_w2"][dd])
        logw = -_softplus(-wpre) - 0.5
        r["lw"][dd] = -jnp.exp(logw)
        a = jax.nn.sigmoid(
            r["bias_a"][dd] + _bdot(jnp.dot(xa, r["lora_a1"][dd], preferred_element_type=F32), r["lora_a2"][dd]))
        kd = kk0 * (1.0 + (a - 1.0) * r["k_a"][...])
        r["kd"][dd] = kd
        r["b"][dd] = kkn * a
        kd_sum = kd_sum + kd
    r["bonus"][...] = (_split_dot(rr * kd_sum * r["r_k"][...], qq) * vv).astype(BF16)


_RW_OUTS = ["r", "v", "g", "na", "bonus", "lw", "kd", "b"]
_RW_STACKED = ("lw", "kd", "b")


def _rw_project(u, seq, p, w):
    n_rows, d = u.shape
    hb = V7X_BF16_ROWS
    tm = _tile(seq, 256, hb)
    tps = seq // tm
    lanes = V7X_LANES
    names, args, specs = [], [], []

    def add(nm, arr, spec=None):
        names.append(nm)
        args.append(arr)
        nd = arr.ndim
        specs.append(spec if spec is not None else pl.BlockSpec(arr.shape, lambda i: (0,) * nd))

    prev, nxt = _halo_rows_index(tm, n_rows, hb)
    add("u", u, pl.BlockSpec((tm, d), lambda i: (i, 0)))
    add("up", u, pl.BlockSpec((hb, d), lambda i: (prev(i), 0)))
    add("un", u, pl.BlockSpec((hb, d), lambda i: (nxt(i), 0)))
    add("mu", _pad_to(p["rw_mu"].astype(F32), 0, V7X_SUBLANES))
    add("w_rkv", w["rw_w_rkv"])
    add("g1", w["rw_g1"])
    add("g2", w["rw_g2"])
    add("bias_w", p["rw_w0"].astype(F32).reshape(2, 1, d))
    add("lora_w1", w["rw_w1"])
    add("lora_w2", w["rw_w2"])
    add("bias_a", p["rw_a0"].astype(F32).reshape(2, 1, d))
    add("lora_a1", w["rw_a1"])
    add("lora_a2", w["rw_a2"])
    add("k_k", p["rw_k_k"].astype(F32).reshape(1, d))
    add("k_a", p["rw_k_a"].astype(F32).reshape(1, d))
    add("r_k", p["rw_r_k"].astype(F32).reshape(1, d))
    add("qq", _block_ones(d, RW_HEAD))
    out_dt = {nm: (BF16 if nm in ("g", "bonus") else F32) for nm in _RW_OUTS}
    row_spec = pl.BlockSpec((tm, d), lambda i: (i, 0))
    dir_spec = pl.BlockSpec((2, tm, d), lambda i: (0, i, 0))
    outs = pl.pallas_call(
        functools.partial(_rw_proj_kernel, names=names + _RW_OUTS, tps=tps, hd=RW_HEAD),
        grid=(n_rows // tm,), in_specs=specs,
        out_specs=[dir_spec if nm in _RW_STACKED else row_spec for nm in _RW_OUTS],
        out_shape=[jax.ShapeDtypeStruct(((2,) if nm in _RW_STACKED else ()) + (n_rows, d), out_dt[nm])
                   for nm in _RW_OUTS],
        compiler_params=_cparams(("parallel",)), name="rwkv7_projections")(*args)
    return dict(zip(_RW_OUTS, outs))


TRI_BASE = 8


def _unit_tri_inverse(nmat, cl):
    ii = lax.broadcasted_iota(jnp.int32, (cl, cl), 0)
    jj = lax.broadcasted_iota(jnp.int32, (cl, cl), 1)
    zero = jnp.float32(0.0)
    ident = (ii == jj).astype(F32)
    block_gap = lambda k: lax.shift_right_logical(ii ^ jj, k)
    k0 = TRI_BASE.bit_length() - 1
    n0 = [jnp.where(block_gap(k0) == 0, n, zero) for n in nmat]
    minv = [ident + n for n in n0]
    pw = n0
    for _ in range(k0 - 1):
        pw = [_bdot(x, x) for x in pw]
        minv = [m + _bdot(m, x) for m, x in zip(minv, pw)]
    k = k0
    while (1 << k) < cl:
        off = block_gap(k) == 1
        minv = [m + _bdot(_bdot(m, jnp.where(off, n, zero)), m) for m, n in zip(minv, nmat)]
        k += 1
    return minv


def _rw_scan_kernel(r_ref, lw_ref, k_ref, v_ref, a_ref, b_ref, s0_ref, y_ref, sfin_ref, s_ref, *, n_chunks, hd):
    dirn, c = pl.program_id(1), pl.program_id(2)
    cl, d = r_ref.shape[1], r_ref.shape[2]
    pw_ = 2 * hd
    n_pairs = d // pw_

    @pl.when(c == 0)
    def _():
        s_ref[...] = s0_ref[0, 0]

    incl, strict = _time_masks(dirn, cl)
    lw = lw_ref[0, 0]
    cum = _hdot(incl.astype(F32), lw)
    p_tot = jnp.exp(jnp.sum(lw, axis=0, keepdims=True))
    p_inv = jnp.exp(-cum)
    at = a_ref[0] * jnp.exp(cum - lw)
    rt = r_ref[0] * jnp.exp(cum)
    bt = (b_ref[0, 0] * p_inv).astype(BF16)
    kt = (k_ref[0, 0] * p_inv).astype(BF16)
    v = v_ref[0]
    hd_shift = hd.bit_length() - 1
    head_of = lambda shape, axis: lax.shift_right_logical(lax.broadcasted_iota(jnp.int32, shape, axis), hd_shift)
    lane = head_of((cl, pw_), 1)
    lane2 = head_of((2 * cl, pw_), 1)
    blk = head_of((pw_, pw_), 0) == head_of((pw_, pw_), 1)
    zero = jnp.float32(0.0)
    pairs = range(n_pairs)
    halves = [(pr, hf) for pr in pairs for hf in range(2)]
    sl = lambda x, pr: x[:, pr * pw_:(pr + 1) * pw_]
    s_old = [s_ref[pr] for pr in pairs]
    s_bf = [s.astype(BF16) for s in s_old]
    xs = [jnp.concatenate([sl(at, pr), sl(rt, pr)], axis=0) for pr in pairs]
    xs0 = [_bdot_nt(xs[pr], s_bf[pr]) for pr in pairs]
    xm = [jnp.where(lane2 == hf, xs[pr], zero).astype(BF16) for pr, hf in halves]
    gb = [lax.dot_general(xm[i], sl(bt, pr), NT_DIMS, preferred_element_type=F32) for i, (pr, hf) in enumerate(halves)]
    gk = [lax.dot_general(xm[i], sl(kt, pr), NT_DIMS, preferred_element_type=F32) for i, (pr, hf) in enumerate(halves)]
    minv = _unit_tri_inverse([jnp.where(strict, g[:cl], zero) for g in gb], cl)
    vm = [jnp.where(lane == hf, sl(v, pr), zero).astype(BF16) for pr, hf in halves]
    rhs = [jnp.where(lane == hf, xs0[pr][:cl], zero)
           + jnp.dot(jnp.where(strict, gk[i][:cl], zero).astype(BF16), vm[i], preferred_element_type=F32)
           for i, (pr, hf) in enumerate(halves)]
    u = [_bdot(minv[i], rhs[i]) for i in range(len(halves))]
    u_b = [x.astype(BF16) for x in u]
    yh = [jnp.dot(jnp.where(incl, gb[i][cl:], zero).astype(BF16), u_b[i], preferred_element_type=F32)
          + jnp.dot(jnp.where(incl, gk[i][cl:], zero).astype(BF16), vm[i], preferred_element_type=F32)
          for i in range(len(halves))]
    for pr in pairs:
        y_ref[0, 0, :, pr * pw_:(pr + 1) * pw_] = xs0[pr][cl:] + yh[2 * pr] + yh[2 * pr + 1]
    for pr in pairs:
        uv = jnp.concatenate([u[2 * pr] + u[2 * pr + 1], sl(v, pr)], axis=0).astype(BF16)
        bk = jnp.concatenate([sl(bt, pr), sl(kt, pr)], axis=0)
        delta = lax.dot_general(uv, bk, TN_DIMS, preferred_element_type=F32)
        s_ref[pr] = (s_old[pr] + jnp.where(blk, delta, zero)) * sl(p_tot, pr)

    @pl.when(c == n_chunks - 1)
    def _():
        sfin_ref[0, 0] = s_ref[...]


def _rw_scan(q, s0, nb, seq, d):
    hd = RW_HEAD
    pw_ = 2 * hd
    assert pw_ == V7X_LANES and d % pw_ == 0
    n_pairs = d // pw_
    n_chunks = seq // CHUNK
    cidx = lambda dd, c: c + dd * (n_chunks - 1 - 2 * c)
    v3 = lambda a: a.reshape(nb, seq, d)
    v4 = lambda a: a.reshape(2, nb, seq, d)
    tok = pl.BlockSpec((1, CHUNK, d), lambda b, dd, c: (b, cidx(dd, c), 0))
    tok_d = pl.BlockSpec((1, 1, CHUNK, d), lambda b, dd, c: (dd, b, cidx(dd, c), 0))
    st_spec = pl.BlockSpec((1, 1, n_pairs, pw_, pw_), lambda b, dd, c: (b, dd, 0, 0, 0))
    y, sfin = pl.pallas_call(
        functools.partial(_rw_scan_kernel, n_chunks=n_chunks, hd=hd),
        grid=(nb, 2, n_chunks),
        in_specs=[tok, tok_d, tok_d, tok, tok, tok_d, st_spec],
        out_specs=[tok_d, st_spec],
        out_shape=[jax.ShapeDtypeStruct((2, nb, seq, d), F32),
                   jax.ShapeDtypeStruct((nb, 2, n_pairs, pw_, pw_), F32)],
        scratch_shapes=[pltpu.VMEM((n_pairs, pw_, pw_), F32)],
        compiler_params=_cparams(("parallel", "arbitrary", "arbitrary")), name="rwkv7_scan",
    )(v3(q["r"]), v4(q["lw"]), v4(q["kd"]), v3(q["v"]), v3(q["na"]), v4(q["b"]), s0)
    return y, sfin


def _rw_out(lay, y, q, tail, p, w, *, alpha, name):
    d = w["rw_w_out"].shape[0]
    n_rows = lay.nb * lay.seq
    y2 = y.reshape(2, n_rows, d)
    qq = _block_ones(d, RW_HEAD)
    lnx_g = p["rw_lnx_g"].astype(F32).reshape(1, d)
    lnx_b = p["rw_lnx_b"].astype(F32).reshape(1, d)
    yspec = pl.BlockSpec((2, lay.tm, d), lay._ix(lambda i: (0, i, 0)))

    def prologue(y_ref, bonus_ref, g_ref, qq_ref, lg_ref, lb_ref):
        yy = y_ref[0] + y_ref[1]
        inv = 1.0 / RW_HEAD
        mean = _split_dot(yy, qq_ref[...]) * inv
        yc = yy - mean
        var = _split_dot(yc * yc, qq_ref[...]) * inv
        yn = yc * lax.rsqrt(var + RW_GN_EPS) * lg_ref[...] + lb_ref[...]
        return (yn + bonus_ref[...].astype(F32)) * g_ref[...].astype(F32)

    pro = [(y2, yspec), lay.raster(q["bonus"], d), lay.raster(q["g"], d), lay.const(qq), lay.const(lnx_g),
           lay.const(lnx_b)]
    return _out_proj_call(lay, prologue, pro, w["rw_w_out"], None, tail, alpha=alpha, name=name)


def _ssd_scan_kernel(x_ref, b_ref, c_ref, dt_ref, dtb_ref, a_ref, s0_ref, xp_ref, eye_ref, y_ref, sfin_ref, s_ref,
                     *, n_chunks, ng, nr, hp, ns):
    dirn, c = pl.program_id(1), pl.program_id(2)
    cl = x_ref.shape[1]

    @pl.when(c == 0)
    def _():
        s_ref[...] = s0_ref[0, 0]

    incl, _ = _time_masks(dirn, cl)
    tri = incl.astype(F32)
    eye = eye_ref[...]
    xp = xp_ref[...]
    dt = _softplus(dt_ref[0] + dtb_ref[0])
    dta = dt * a_ref[0]
    cum = _hdot(tri, dta)
    tot = jnp.sum(dta, axis=0, keepdims=True)
    cum_t = _transpose_via_eye(cum, eye)
    dt_t = _transpose_via_eye(dt, eye)
    e_in = _split_dot(jnp.exp(cum), xp)
    e_end = _split_dot(jnp.exp(tot - cum) * dt, xp)
    e_tot = _split_dot(jnp.broadcast_to(jnp.exp(tot), (V7X_SUBLANES, tot.shape[1])), xp)[0:1]
    x = x_ref[0]
    xw = (x.astype(F32) * e_end).astype(BF16)
    gw = nr * hp
    pair = 2 * hp
    lane = lax.broadcasted_iota(jnp.int32, (cl, pair), 1)
    neg = jnp.float32(-jnp.inf)
    s_all = [s_ref[g] for g in range(ng)]
    y_parts, s_new = [], []
    for g in range(ng):
        bm = b_ref[0, :, g * ns:(g + 1) * ns]
        cm = c_ref[0, :, g * ns:(g + 1) * ns]
        cb = lax.dot_general(cm, bm, NT_DIMS, preferred_element_type=F32)
        s_g = s_all[g]
        y_inter = jnp.dot(cm, s_g.astype(BF16), preferred_element_type=F32) * e_in[:, g * gw:(g + 1) * gw]
        for rp in range(nr // 2):
            y_pair = jnp.zeros((cl, pair), F32)
            xpair = x[:, g * gw + rp * pair:g * gw + (rp + 1) * pair]
            for half in range(2):
                h = g * nr + rp * 2 + half
                dec = jnp.exp(jnp.where(incl, cum[:, h:h + 1] - cum_t[h:h + 1, :], neg))
                sc = (cb * dec * dt_t[h:h + 1, :]).astype(BF16)
                xh = jnp.where((lane >= half * hp) & (lane < (half + 1) * hp), xpair, jnp.zeros_like(xpair))
                y_pair = y_pair + jnp.dot(sc, xh, preferred_element_type=F32)
            y_parts.append((g * gw + rp * pair, y_pair + y_inter[:, rp * pair:(rp + 1) * pair]))
        s_new.append(s_g * e_tot[:, g * gw:(g + 1) * gw] + lax.dot_general(
            bm, xw[:, g * gw:(g + 1) * gw], TN_DIMS, preferred_element_type=F32))
    for lo, val in y_parts:
        y_ref[0, 0, :, lo:lo + pair] = val
    for g in range(ng):
        s_ref[g] = s_new[g]

    @pl.when(c == n_chunks - 1)
    def _():
        sfin_ref[0, 0] = s_ref[...]


def _ssd_scan(xbc, dt_raw, s0, nb, seq, p):
    ng, ns, hp = SSD_GROUPS, SSD_STATE, SSD_HEAD
    inner = xbc.shape[1] - 2 * ng * ns
    nh = inner // hp
    nr = nh // ng
    lanes = V7X_LANES
    assert ns == lanes and nh <= lanes and nr % 2 == 0
    n_chunks = seq // CHUNK
    cidx = lambda dd, c: c + dd * (n_chunks - 1 - 2 * c)
    xbc3 = xbc.reshape(nb, seq, xbc.shape[1])
    dt3 = dt_raw.reshape(nb, seq, 2 * lanes)
    a = -jnp.exp(p["ssd_A_log"].astype(F32))
    a_pad = _pad_to(a, 1, lanes).reshape(2, 1, lanes)
    dtb = _pad_to(p["ssd_dt_bias"].astype(F32), 1, lanes).reshape(2, 1, lanes)
    xp = _expand_mat(lanes, nh, hp)
    eye = jnp.eye(lanes, dtype=F32)
    bw = ng * ns
    y, sfin = pl.pallas_call(
        functools.partial(_ssd_scan_kernel, n_chunks=n_chunks, ng=ng, nr=nr, hp=hp, ns=ns),
        grid=(nb, 2, n_chunks),
        in_specs=[pl.BlockSpec((1, CHUNK, inner), lambda b, dd, c: (b, cidx(dd, c), 0)),
                  pl.BlockSpec((1, CHUNK, bw), lambda b, dd, c: (b, cidx(dd, c), inner // bw)),
                  pl.BlockSpec((1, CHUNK, bw), lambda b, dd, c: (b, cidx(dd, c), inner // bw + 1)),
                  pl.BlockSpec((1, CHUNK, lanes), lambda b, dd, c: (b, cidx(dd, c), dd)),
                  pl.BlockSpec((1, 1, lanes), lambda b, dd, c: (dd, 0, 0)),
                  pl.BlockSpec((1, 1, lanes), lambda b, dd, c: (dd, 0, 0)),
                  pl.BlockSpec((1, 1, ng, ns, nr * hp), lambda b, dd, c: (b, dd, 0, 0, 0)),
                  pl.BlockSpec(xp.shape, lambda b, dd, c: (0, 0)),
                  pl.BlockSpec(eye.shape, lambda b, dd, c: (0, 0))],
        out_specs=[pl.BlockSpec((1, 1, CHUNK, inner), lambda b, dd, c: (dd, b, cidx(dd, c), 0)),
                   pl.BlockSpec((1, 1, ng, ns, nr * hp), lambda b, dd, c: (b, dd, 0, 0, 0))],
        out_shape=[jax.ShapeDtypeStruct((2, nb, seq, inner), F32),
                   jax.ShapeDtypeStruct((nb, 2, ng, ns, nr * hp), F32)],
        scratch_shapes=[pltpu.VMEM((ng, ns, nr * hp), F32)],
        compiler_params=_cparams(("parallel", "arbitrary", "arbitrary")), name="ssd_scan",
    )(xbc3, xbc3, xbc3, dt3, dtb, a_pad, s0, xp, eye)
    return y.reshape(2, nb * seq, inner), sfin


def _ssd_project(u, nb, seq, p, w, colmajor):
    mm = (lambda x, wt, **kw: _matmul_r2c(x, wt, None, nb, **kw)) if colmajor else (
        lambda x, wt, **kw: _matmul(x, wt, None, **kw))
    z = mm(u, w["ssd_w_z"], out_dtype=BF16, name="ssd_in_proj_z")
    xbc_raw = mm(u, w["ssd_w_xbc"], out_dtype=F32, name="ssd_in_proj_xbc")
    dt_raw = mm(u, w["ssd_w_dt"], out_dtype=F32, name="ssd_in_proj_dt")
    xbc = _conv3(xbc_raw, p["ssd_conv_w"], p["ssd_conv_b"], seq, act=_silu, name="ssd_short_conv")
    return z, xbc, dt_raw


def _ssd_out(lay, y, z, xbc, tail, p, w, *, alpha, name):
    inner = w["ssd_w_out"].shape[0]
    ng = SSD_GROUPS
    gwid = inner // ng
    d_row = jnp.repeat(p["ssd_d_skip"].astype(F32), SSD_HEAD).reshape(1, inner)
    ng_row = p["ssd_norm_g"].astype(F32).reshape(1, inner)

    def prologue(y0_ref, y1_ref, xs_ref, z_ref, d_ref, g_ref):
        flat = lambda ref: ref[...].reshape(-1, ref.shape[-1]).astype(F32)
        yy = flat(xs_ref) * d_ref[...] + flat(y0_ref) + flat(y1_ref)
        yy = yy * _silu(flat(z_ref))
        parts = []
        for g in range(ng):
            yg = yy[:, g * gwid:(g + 1) * gwid]
            ms = jnp.mean(yg * yg, axis=-1, keepdims=True)
            parts.append(yg * lax.rsqrt(ms + 1e-6))
        return jnp.concatenate(parts, axis=1) * g_ref[...]

    pro = [lay.native(y, inner, lead=0), lay.native(y, inner, lead=1), lay.native(xbc, inner),
           lay.native(z, inner), lay.const(d_row), lay.const(ng_row)]
    return _out_proj_call(lay, prologue, pro, w["ssd_w_out"], None, tail, alpha=alpha, name=name)


def _gdn_scan_kernel(q_ref, k_ref, v_ref, a_ref, bta_ref, dtb_ref, al_ref, s0_ref, eye_ref, o_ref, sfin_ref, s_ref,
                     *, n_chunks, hk, hv, dh):
    dirn, c = pl.program_id(1), pl.program_id(2)
    cl = q_ref.shape[1]

    @pl.when(c == 0)
    def _():
        s_ref[...] = s0_ref[0, 0]

    incl, strict = _time_masks(dirn, cl)
    tri = incl.astype(F32)
    eye = eye_ref[...]
    eye_b = eye.astype(BF16)
    neg = jnp.float32(-jnp.inf)
    g = al_ref[0] * _softplus(a_ref[0] + dtb_ref[0])
    beta = jax.nn.sigmoid(bta_ref[0])
    gn = _hdot(tri, g)
    gtot = jnp.sum(g, axis=0, keepdims=True)
    gn_t = _transpose_via_eye(gn, eye)
    e_in = jnp.exp(gn)
    e_end = jnp.exp(gtot - gn)
    e_tot = jnp.exp(gtot)
    rep = hv // hk
    heads = range(hv)
    s_old = [s_ref[h] for h in heads]
    s_bf = [s.astype(BF16) for s in s_old]
    qn, kn, kk, qk = [], [], [], []
    for hq in range(hk):
        qh = q_ref[0, :, hq * dh:(hq + 1) * dh].astype(F32)
        kh = k_ref[0, :, hq * dh:(hq + 1) * dh].astype(F32)
        qn.append(qh * lax.rsqrt(jnp.sum(qh * qh, axis=-1, keepdims=True) + 1e-6) * (dh ** -0.5))
        kn.append(kh * lax.rsqrt(jnp.sum(kh * kh, axis=-1, keepdims=True) + 1e-6))
        gram = _bdot_nt(jnp.concatenate([kn[hq], qn[hq]], axis=0), kn[hq])
        kk.append(gram[:cl])
        qk.append(gram[cl:])
    bh = [beta[:, h:h + 1] for h in heads]
    diff = [gn[:, h:h + 1] - gn_t[h:h + 1, :] for h in heads]
    nmat = [-(kk[h // rep] * bh[h] * jnp.exp(jnp.where(strict, diff[h], neg))) for h in heads]
    attn = [(qk[h // rep] * jnp.exp(jnp.where(incl, diff[h], neg))).astype(BF16) for h in heads]
    minv = _unit_tri_inverse(nmat, cl)
    rhs =[jnp.concatenate([v_ref[0, :, h * dh:(h + 1) * dh].astype(F32) * bh[h],
                            kn[h // rep] * (bh[h] * e_in[:, h:h + 1])], axis=1) for h in heads]
    sol = [_bdot(minv[h], rhs[h]) for h in heads]
    u_b = [(sol[h][:, :dh] - _bdot(sol[h][:, dh:], s_bf[h])).astype(BF16) for h in heads]
    out = [_bdot(qn[h // rep] * e_in[:, h:h + 1], s_bf[h]) + jnp.dot(attn[h], u_b[h], preferred_element_type=F32)
           for h in heads]
    s_new = [s_old[h] * e_tot[:, h:h + 1]
             + lax.dot_general((kn[h // rep] * e_end[:, h:h + 1]).astype(BF16), u_b[h], TN_DIMS,
                               preferred_element_type=F32) for h in heads]
    for h in heads:
        o_ref[0, 0, :, h * dh:(h + 1) * dh] = out[h]
    for h in heads:
        s_ref[h] = s_new[h]

    @pl.when(c == n_chunks - 1)
    def _():
        sfin_ref[0, 0] = s_ref[...]


def _gdn_scan(qkv, ab, s0, nb, seq, p):
    dh = GDN_HEAD
    lanes = V7X_LANES
    hv = p["gdn_dt_bias"].shape[1]
    vw = hv * dh
    qk = (qkv.shape[1] - vw) // 2
    hk = qk // dh
    assert dh == lanes and hv <= lanes
    n_chunks = seq // CHUNK
    cidx = lambda dd, c: c + dd * (n_chunks - 1 - 2 * c)
    qkv3 = qkv.reshape(nb, seq, qkv.shape[1])
    ab3 = ab.reshape(nb, seq, 4 * lanes)
    dtb = _pad_to(p["gdn_dt_bias"].astype(F32), 1, lanes).reshape(2, 1, lanes)
    al = _pad_to(-jnp.exp(p["gdn_A_log"].astype(F32)), 1, lanes).reshape(2, 1, lanes)
    eye = jnp.eye(lanes, dtype=F32)
    o, sfin = pl.pallas_call(
        functools.partial(_gdn_scan_kernel, n_chunks=n_chunks, hk=hk, hv=hv, dh=dh),
        grid=(nb, 2, n_chunks),
        in_specs=[pl.BlockSpec((1, CHUNK, qk), lambda b, dd, c: (b, cidx(dd, c), 0)),
                  pl.BlockSpec((1, CHUNK, qk), lambda b, dd, c: (b, cidx(dd, c), 1)),
                  pl.BlockSpec((1, CHUNK, vw), lambda b, dd, c: (b, cidx(dd, c), 2 * qk // vw)),
                  pl.BlockSpec((1, CHUNK, lanes), lambda b, dd, c: (b, cidx(dd, c), 2 * dd)),
                  pl.BlockSpec((1, CHUNK, lanes), lambda b, dd, c: (b, cidx(dd, c), 2 * dd + 1)),
                  pl.BlockSpec((1, 1, lanes), lambda b, dd, c: (dd, 0, 0)),
                  pl.BlockSpec((1, 1, lanes), lambda b, dd, c: (dd, 0, 0)),
                  pl.BlockSpec((1, 1, hv, dh, dh), lambda b, dd, c: (b, dd, 0, 0, 0)),
                  pl.BlockSpec(eye.shape, lambda b, dd, c: (0, 0))],
        out_specs=[pl.BlockSpec((1, 1, CHUNK, vw), lambda b, dd, c: (dd, b, cidx(dd, c), 0)),
                   pl.BlockSpec((1, 1, hv, dh, dh), lambda b, dd, c: (b, dd, 0, 0, 0))],
        out_shape=[jax.ShapeDtypeStruct((2, nb, seq, vw), F32),
                   jax.ShapeDtypeStruct((nb, 2, hv, dh, dh), F32)],
        scratch_shapes=[pltpu.VMEM((hv, dh, dh), F32)],
        compiler_params=_cparams(("parallel", "arbitrary", "arbitrary")), name="gdn_scan",
    )(qkv3, qkv3, qkv3, ab3, ab3, dtb, al, s0, eye)
    return o.reshape(2, nb * seq, vw), sfin


def _gdn_project(u, nb, seq, p, w, colmajor):
    mm = (lambda x, wt, **kw: _matmul_r2c(x, wt, None, nb, **kw)) if colmajor else (
        lambda x, wt, **kw: _matmul(x, wt, None, **kw))
    qkv_raw = mm(u, w["gdn_w_qkv"], out_dtype=F32, name="gdn_in_proj_qkv")
    z = mm(u, w["gdn_w_z"], out_dtype=BF16, name="gdn_in_proj_z")
    ab = mm(u, w["gdn_w_ab"], out_dtype=F32, name="gdn_in_proj_ab")
    qkv = _conv3(qkv_raw, p["gdn_conv_w"], None, seq, act=_silu, name="gdn_short_conv")
    return qkv, z, ab


def _gdn_out(lay, o, z, tail, p, w, *, alpha, name):
    vw = w["gdn_w_out"].shape[0]
    dh = GDN_HEAD
    hv = vw // dh
    g_row = jnp.tile(p["gdn_norm_g"].astype(F32), hv).reshape(1, vw)

    def prologue(o0_ref, o1_ref, z_ref, g_ref):
        flat = lambda ref: ref[...].reshape(-1, ref.shape[-1]).astype(F32)
        oo = flat(o0_ref) + flat(o1_ref)
        parts = []
        for h in range(hv):
            oh = oo[:, h * dh:(h + 1) * dh]
            ms = jnp.mean(oh * oh, axis=-1, keepdims=True)
            parts.append(oh * lax.rsqrt(ms + 1e-6))
        return jnp.concatenate(parts, axis=1) * g_ref[...] * _silu(flat(z_ref))

    pro = [lay.native(o, vw, lead=0), lay.native(o, vw, lead=1), lay.native(z, vw), lay.const(g_row)]
    return _out_proj_call(lay, prologue, pro, w["gdn_w_out"], None, tail, alpha=alpha, name=name)


def _prep_weights(p):
    lanes = V7X_LANES
    bf = lambda a: a.astype(BF16)
    d = p["hy_w_out"].shape[0]
    w = {k: bf(p[k]) for k in ("hy_w_in", "hy_w_out", "rw_w_rkv", "rw_w_out", "ssd_w_out", "gdn_w_out",
                               "ffn_w13", "ffn_w2", "moe_w13", "moe_w2", "mod_w")}
    lora = lambda a, ax: bf(_pad_to(a, ax, lanes * ((a.shape[ax] + lanes - 1) // lanes)))
    w["rw_w1"], w["rw_w2"] = lora(p["rw_w1"], 2), lora(p["rw_w2"], 1)
    w["rw_a1"], w["rw_a2"] = lora(p["rw_a1"], 2), lora(p["rw_a2"], 1)
    w["rw_g1"], w["rw_g2"] = lora(p["rw_g1"], 1), lora(p["rw_g2"], 0)
    inner = p["ssd_w_out"].shape[0]
    nbc = 2 * SSD_GROUPS * SSD_STATE
    nh = inner // SSD_HEAD
    ws = p["ssd_w_in"]
    w["ssd_w_z"] = bf(ws[:, :inner])
    w["ssd_w_xbc"] = bf(ws[:, inner:2 * inner + nbc])
    wdt = ws[:, 2 * inner + nbc:].reshape(d, 2, nh)
    w["ssd_w_dt"] = bf(_pad_to(wdt, 2, lanes).reshape(d, 2 * lanes))
    vw = p["gdn_w_out"].shape[0]
    hv = vw // GDN_HEAD
    qkvw = p["gdn_conv_w"].shape[1]
    wg = p["gdn_w_in"]
    w["gdn_w_qkv"] = bf(wg[:, :qkvw])
    w["gdn_w_z"] = bf(wg[:, qkvw:qkvw + vw])
    wab = wg[:, qkvw + vw:].reshape(d, 2, 2, hv)
    wab = jnp.transpose(wab, (0, 2, 1, 3))
    w["gdn_w_ab"] = bf(_pad_to(wab, 3, lanes).reshape(d, 4 * lanes))
    w["moe_router"] = _pad_to(p["moe_router"].astype(F32), 2, lanes)
    return w


_PARAM_NAMES = (
    "mod_w mod_b ln_g ln_b hy_w_in hy_b_in hy_conv_w hy_conv_b hy_f_w1 hy_f_b1 hy_f_w2 hy_f_b2 hy_f_freq "
    "hy_f_w3 hy_decay hy_skip hy_w_out hy_b_out rw_mu rw_w_rkv rw_w0 rw_w1 rw_w2 rw_a0 rw_a1 rw_a2 rw_g1 rw_g2 "
    "rw_k_k rw_k_a rw_r_k rw_lnx_g rw_lnx_b rw_w_out ssd_w_in ssd_conv_w ssd_conv_b ssd_dt_bias ssd_A_log "
    "ssd_d_skip ssd_norm_g ssd_w_out gdn_w_in gdn_conv_w gdn_dt_bias gdn_A_log gdn_norm_g gdn_w_out ffn_w13 "
    "ffn_w2 moe_router moe_w13 moe_w2").split()


def _forward(x, c, ctx, c_ctx, p):
    nb, seq, d = x.shape
    lc = ctx.shape[1]
    depth = p["mod_w"].shape[0]
    alpha = (2 * depth) ** 0.25
    n_exp = p["moe_router"].shape[2]
    assert seq == GRID_W * GRID_W and seq % CHUNK == 0 and lc % CHUNK == 0
    w = _prep_weights(p)

    cc = _pad_to(jnp.concatenate([c, c_ctx[None, :]], axis=0).astype(F32), 0, V7X_BF16_ROWS)
    mods = [
        _matmul(cc, w["mod_w"][i], p["mod_b"][i], pre=_silu, out_dtype=F32, name="adaln_modulation")
        .reshape(cc.shape[0], 6, d) for i in range(depth)]

    def chunk(i, k, stream):
        m = mods[i][:, k]
        return m[:nb, None, :] if stream == "x" else m[nb:nb + 1, None, :]

    lay = {"x": _RowLayout(nb, seq, True), "c": _RowLayout(nb, lc, False),
           "xcol": _RowLayout(nb, seq, True, mode="cols")}
    seqs = {"x": seq, "c": lc}
    h = {"x": x.reshape(nb * seq, d).astype(F32), "c": ctx.reshape(nb * lc, d).astype(F32)}

    u = {s: _modulate(lay[s], h[s], chunk(0, 1, s), chunk(0, 0, s)) for s in ("x", "c")}

    for i in range(depth):
        last = i == depth - 1
        kind = i % 4
        moe = i % 2 == 1
        streams = ("x",) if last else ("x", "c")

        def tail1(s):
            t = dict(h=h[s], gate=chunk(i, 2, s), ln_g=p["ln_g"][i, 0], ln_b=p["ln_b"][i, 0],
                     scale=chunk(i, 4, s), shift=chunk(i, 3, s))
            if moe:
                t.update(router=w["moe_router"][i // 2], n_exp=n_exp)
            return t

        res = {}
        if kind == 0:
            for s in streams:
                z = _hyena_mixer(u[s], nb, seqs[s], p, w)
                res[s] = _hyena_out(lay[s], z, tail1(s), p, w, nb, seqs[s], alpha=alpha, name=f"hyena_out_{s}")
        elif kind == 1:
            qc = _rw_project(u["c"], lc, p, w)
            qx = _rw_project(u["x"], seq, p, w)
            s0 = jnp.zeros((nb, 2, d // (2 * RW_HEAD), 2 * RW_HEAD, 2 * RW_HEAD), F32)
            yc, s_c = _rw_scan(qc, s0, nb, lc, d)
            yx, _ = _rw_scan(qx, s_c, nb, seq, d)
            ys, qs = {"x": yx, "c": yc}, {"x": qx, "c": qc}
            for s in streams:
                res[s] = _rw_out(lay[s], ys[s], qs[s], tail1(s), p, w, alpha=alpha, name=f"rwkv7_out_{s}")
        elif kind == 2:
            zc, xbc_c, dt_c = _ssd_project(u["c"], nb, lc, p, w, False)
            zx, xbc_x, dt_x = _ssd_project(u["x"], nb, seq, p, w, True)
            inner = w["ssd_w_out"].shape[0]
            s0 = jnp.zeros((nb, 2, SSD_GROUPS, SSD_STATE, inner // SSD_GROUPS), F32)
            yc, s_c = _ssd_scan(xbc_c, dt_c, s0, nb, lc, p)
            yx, _ = _ssd_scan(xbc_x, dt_x, s_c, nb, seq, p)
            res["x"] = _ssd_out(lay["xcol"], yx, zx, xbc_x, tail1("x"), p, w, alpha=alpha, name="ssd_out_x")
            if not last:
                res["c"] = _ssd_out(lay["c"], yc, zc, xbc_c, tail1("c"), p, w, alpha=alpha, name="ssd_out_c")
        else:
            qkv_c, zc, ab_c = _gdn_project(u["c"], nb, lc, p, w, False)
            qkv_x, zx, ab_x = _gdn_project(u["x"], nb, seq, p, w, True)
            vw = w["gdn_w_out"].shape[0]
            s0 = jnp.zeros((nb, 2, vw // GDN_HEAD, GDN_HEAD, GDN_HEAD), F32)
            oc, s_c = _gdn_scan(qkv_c, ab_c, s0, nb, lc, p)
            ox, _ = _gdn_scan(qkv_x, ab_x, s_c, nb, seq, p)
            res["x"] = _gdn_out(lay["xcol"], ox, zx, tail1("x"), p, w, alpha=alpha, name="gdn_out_x")
            if not last:
                res["c"] = _gdn_out(lay["c"], oc, zc, tail1("c"), p, w, alpha=alpha, name="gdn_out_c")

        for s in streams:
            t = dict(h=res[s]["h"], gate=chunk(i, 5, s), ln_g=p["ln_g"][i, 1], ln_b=p["ln_b"][i, 1])
            if not last:
                t.update(scale=chunk(i + 1, 1, s), shift=chunk(i + 1, 0, s))
            if moe:
                out = _moe_call(lay[s], res[s]["u"], w["moe_w13"][i // 2], w["moe_w2"][i // 2], res[s]["comb"], t,
                                alpha=alpha, name=f"moe_ffn_{s}")
            else:
                out = _ffn_call(lay[s], res[s]["u"], w["ffn_w13"][i // 2], w["ffn_w2"][i // 2], t,
                                alpha=alpha, name=f"dense_ffn_{s}")
            h[s] = out["h"]
            if not last:
                u[s] = out["u"]
    return h["x"].reshape(nb, seq, d).astype(x.dtype)


def kernel(x, c, ctx, c_ctx, mod_w, mod_b, ln_g, ln_b, hy_w_in, hy_b_in, hy_conv_w, hy_conv_b, hy_f_w1, hy_f_b1, hy_f_w2, hy_f_b2, hy_f_freq, hy_f_w3, hy_decay, hy_skip, hy_w_out, hy_b_out, rw_mu, rw_w_rkv, rw_w0, rw_w1, rw_w2, rw_a0, rw_a1, rw_a2, rw_g1, rw_g2, rw_k_k, rw_k_a, rw_r_k, rw_lnx_g, rw_lnx_b, rw_w_out, ssd_w_in, ssd_conv_w, ssd_conv_b, ssd_dt_bias, ssd_A_log, ssd_d_skip, ssd_norm_g, ssd_w_out, gdn_w_in, gdn_conv_w, gdn_dt_bias, gdn_A_log, gdn_norm_g, gdn_w_out, ffn_w13, ffn_w2, moe_router, moe_w13, moe_w2):
    vals = (mod_w, mod_b, ln_g, ln_b, hy_w_in, hy_b_in, hy_conv_w, hy_conv_b, hy_f_w1, hy_f_b1, hy_f_w2, hy_f_b2,
            hy_f_freq, hy_f_w3, hy_decay, hy_skip, hy_w_out, hy_b_out, rw_mu, rw_w_rkv, rw_w0, rw_w1, rw_w2, rw_a0,
            rw_a1, rw_a2, rw_g1, rw_g2, rw_k_k, rw_k_a, rw_r_k, rw_lnx_g, rw_lnx_b, rw_w_out, ssd_w_in, ssd_conv_w,
            ssd_conv_b, ssd_dt_bias, ssd_A_log, ssd_d_skip, ssd_norm_g, ssd_w_out, gdn_w_in, gdn_conv_w,
            gdn_dt_bias, gdn_A_log, gdn_norm_g, gdn_w_out, ffn_w13, ffn_w2, moe_router, moe_w13, moe_w2)
    return _forward(x, c, ctx, c_ctx, dict(zip(_PARAM_NAMES, vals)))
```

```python
import functools
import math

import jax
import jax.numpy as jnp
import numpy as np
from jax import lax
from jax.experimental import pallas as pl
from jax.experimental.pallas import tpu as pltpu

F32 = jnp.float32
BF16 = jnp.bfloat16
HIGHEST = lax.Precision.HIGHEST

GRID_W = 64
LN_EPS = 1e-5
HY_BANDS = 8
RW_HEAD = 64
RW_GN_EPS = 64e-5
SSD_HEAD = 64
SSD_STATE = 128
SSD_GROUPS = 4
GDN_HEAD = 128
CHUNK = 64
TOP_K = 2

V7X_LANES = 128
V7X_SUBLANES = 8
V7X_BF16_ROWS = 16
V7X_VMEM_BUDGET = 56 * 1024 * 1024

NT_DIMS = (((1,), (1,)), ((), ()))
TN_DIMS = (((0,), (0,)), ((), ()))


def _tile(n, pref, mult):
    best = None
    t = mult
    while t <= min(n, pref):
        if n % t == 0:
            best = t
        t += mult
    return best if best is not None else n


def _pad_to(a, axis, size):
    pad = size - a.shape[axis]
    if pad == 0:
        return a
    cfg = [(0, 0)] * a.ndim
    cfg[axis] = (0, pad)
    return jnp.pad(a, cfg)


def _cparams(sem):
    return pltpu.CompilerParams(dimension_semantics=sem, vmem_limit_bytes=V7X_VMEM_BUDGET)


def _silu(x):
    return x * jax.nn.sigmoid(x)


def _softplus(x):
    return jnp.maximum(x, 0.0) + jnp.log1p(jnp.exp(-jnp.abs(x)))


def _bdot(a, b):
    return jnp.dot(a.astype(BF16), b.astype(BF16), preferred_element_type=F32)


def _bdot_nt(a, b):
    return lax.dot_general(a.astype(BF16), b.astype(BF16), NT_DIMS, preferred_element_type=F32)


def _hdot(a, b):
    return jnp.dot(a, b, precision=HIGHEST, preferred_element_type=F32)


def _split_dot(x, m):
    hi = x.astype(BF16)
    lo = (x - hi.astype(F32)).astype(BF16)
    return jnp.dot(hi, m, preferred_element_type=F32) + jnp.dot(lo, m, preferred_element_type=F32)


def _transpose_via_eye(x, eye):
    return lax.dot_general(eye, x, NT_DIMS, precision=HIGHEST, preferred_element_type=F32)


def _head_sum(x, expand):
    hi = x.astype(BF16)
    lo = (x - hi.astype(F32)).astype(BF16)
    s = (lax.dot_general(hi, expand, NT_DIMS, preferred_element_type=F32)
         + lax.dot_general(lo, expand, NT_DIMS, preferred_element_type=F32))
    return _split_dot(s, expand)


def _expand_mat(n_in_pad, n_heads, width):
    m = np.zeros((n_in_pad, n_heads * width), np.float32)
    for h in range(n_heads):
        m[h, h * width:(h + 1) * width] = 1.0
    return jnp.asarray(m, BF16)


def _perm_mat(ib, jb):
    n = ib * jb
    q = np.arange(n)
    p = (q % ib) * jb + q // ib
    m = np.zeros((n, n), np.float32)
    m[q, p] = 1.0
    return jnp.asarray(m, BF16)


def _time_masks(d, c):
    ii = lax.broadcasted_iota(jnp.int32, (c, c), 0)
    jj = lax.broadcasted_iota(jnp.int32, (c, c), 1)
    lag = (ii - jj) * (1 - 2 * d)
    return lag >= 0, lag > 0


def _mm_kernel(*refs, nk, pre, act, has_bias, has_perm):
    it = iter(refs)
    x_ref, w_ref = next(it), next(it)
    b_ref = next(it) if has_bias else None
    p_ref = next(it) if has_perm else None
    o_ref = next(it)
    acc_ref = next(it) if nk > 1 else None
    xp_ref = next(it) if has_perm else None

    def load_x():
        x = x_ref[...]
        x = x.reshape(-1, x.shape[-1])
        if pre is not None:
            x = pre(x.astype(F32))
        return x.astype(BF16)

    if has_perm:
        @pl.when(pl.program_id(3) == 0)
        def _():
            xp_ref[...] = jnp.dot(p_ref[...], load_x(), preferred_element_type=F32).astype(BF16)

        x = xp_ref[...]
    else:
        x = load_x()
    part = jnp.dot(x, w_ref[...], preferred_element_type=F32)

    def finish(r):
        if has_bias:
            r = r + b_ref[...]
        if act is not None:
            r = act(r)
        o_ref[...] = r.astype(o_ref.dtype).reshape(o_ref.shape)

    if nk == 1:
        finish(part)
    else:
        k = pl.program_id(2)

        @pl.when(k == 0)
        def _():
            acc_ref[...] = jnp.zeros_like(acc_ref)

        acc_ref[...] += part

        @pl.when(k == nk - 1)
        def _():
            finish(acc_ref[...])


def _matmul(x, w, bias=None, *, out_dtype=F32, act=None, pre=None, tm=512, tn=2048, tk=None, name="matmul"):
    m, kdim = x.shape
    n = w.shape[1]
    tm = _tile(m, tm, V7X_BF16_ROWS)
    tn = _tile(n, tn, V7X_LANES)
    tk = kdim if tk is None else _tile(kdim, tk, V7X_LANES)
    nk = kdim // tk
    in_specs = [pl.BlockSpec((tm, tk), lambda i, j, k: (i, k)),
                pl.BlockSpec((tk, tn), lambda i, j, k: (k, j))]
    args = [x, w]
    if bias is not None:
        in_specs.append(pl.BlockSpec((1, tn), lambda i, j, k: (0, j)))
        args.append(bias.reshape(1, n).astype(F32))
    return pl.pallas_call(
        functools.partial(_mm_kernel, nk=nk, pre=pre, act=act, has_bias=bias is not None, has_perm=False),
        grid=(m // tm, n // tn, nk),
        in_specs=in_specs,
        out_specs=pl.BlockSpec((tm, tn), lambda i, j, k: (i, j)),
        out_shape=jax.ShapeDtypeStruct((m, n), out_dtype),
        scratch_shapes=[pltpu.VMEM((tm, tn), F32)] if nk > 1 else [],
        compiler_params=_cparams(("parallel", "parallel", "arbitrary")),
        name=name,
    )(*args)


def _col_tiles(gw):
    ib = min(V7X_BF16_ROWS, gw)
    jb = min(32, gw)
    return ib, jb


def _matmul_r2c(u, w, bias, nb, *, out_dtype, tn=2048, name="matmul_r2c"):
    gw = GRID_W
    kdim, n = w.shape
    ib, jb = _col_tiles(gw)
    tn = _tile(n, tn, V7X_LANES)
    u4 = u.reshape(nb, gw, gw, kdim)
    in_specs = [pl.BlockSpec((1, ib, jb, kdim), lambda b, i, j, c: (b, i, j, 0)),
                pl.BlockSpec((kdim, tn), lambda b, i, j, c: (0, c))]
    args = [u4, w]
    if bias is not None:
        in_specs.append(pl.BlockSpec((1, tn), lambda b, i, j, c: (0, c)))
        args.append(bias.reshape(1, n).astype(F32))
    in_specs.append(pl.BlockSpec((ib * jb, ib * jb), lambda b, i, j, c: (0, 0)))
    args.append(_perm_mat(ib, jb))
    out = pl.pallas_call(
        functools.partial(_mm_kernel, nk=1, pre=None, act=None, has_bias=bias is not None, has_perm=True),
        grid=(nb, gw // ib, gw // jb, n // tn),
        in_specs=in_specs,
        out_specs=pl.BlockSpec((1, jb, ib, tn), lambda b, i, j, c: (b, j, i, c)),
        out_shape=jax.ShapeDtypeStruct((nb, gw, gw, n), out_dtype),
        scratch_shapes=[pltpu.VMEM((ib * jb, kdim), BF16)],
        compiler_params=_cparams(("parallel", "parallel", "parallel", "arbitrary")),
        name=name,
    )(*args)
    return out.reshape(nb * gw * gw, n)


def _ln_epilogue(h, y, gate, ln_g, ln_b, alpha):
    pre = alpha * h + gate * y
    mu = jnp.mean(pre, axis=-1, keepdims=True)
    xc = pre - mu
    var = jnp.mean(xc * xc, axis=-1, keepdims=True)
    return xc * lax.rsqrt(var + LN_EPS) * ln_g + ln_b


def _route(logits, n_exp):
    lane = lax.broadcasted_iota(jnp.int32, logits.shape, 1)
    neg = jnp.float32(-jnp.inf)
    lg = jnp.where(lane < n_exp, logits, neg)
    big = jnp.int32(logits.shape[1])
    m1 = jnp.max(lg, axis=-1, keepdims=True)
    i1 = jnp.min(jnp.where(lg == m1, lane, big), axis=-1, keepdims=True)
    lg2 = jnp.where(lane == i1, neg, lg)
    m2 = jnp.max(lg2, axis=-1, keepdims=True)
    i2 = jnp.min(jnp.where(lg2 == m2, lane, big), axis=-1, keepdims=True)
    e2 = jnp.exp(m2 - m1)
    den = 1.0 + e2
    return jnp.where(lane == i1, 1.0 / den, jnp.where(lane == i2, e2 / den, 0.0))


def _finish_rows(refs, y, *, alpha, emit_u, n_exp):
    h_ref, gate_ref, lng_ref, lnb_ref = refs["h"], refs["gate"], refs["ln_g"], refs["ln_b"]
    h = h_ref[...].reshape(y.shape)
    hn = _ln_epilogue(h, y, gate_ref[0], lng_ref[...], lnb_ref[...], alpha)
    refs["h_out"][...] = hn.reshape(refs["h_out"].shape)
    if emit_u:
        u = hn * (1.0 + refs["scale"][0]) + refs["shift"][0]
        refs["u_out"][...] = u.astype(BF16).reshape(refs["u_out"].shape)
        if n_exp:
            logits = _hdot(u, refs["router"][...])
            refs["comb_out"][...] = _route(logits, n_exp).reshape(refs["comb_out"].shape)


class _RowLayout:
    def __init__(self, nb, seq, per_batch_mod, mode="rows", tm=512):
        self.nb, self.seq, self.mode, self.per_batch_mod = nb, seq, mode, per_batch_mod
        if mode == "rows":
            self.tm = _tile(seq, tm, V7X_BF16_ROWS)
            self.tps = seq // self.tm
            self.grid = (nb * self.tps,)
            self.rows = self.tm
        else:
            self.ib, self.jb = _col_tiles(GRID_W)
            self.grid = (nb, GRID_W // self.ib, GRID_W // self.jb)
            self.rows = self.ib * self.jb
        self.ngrid = len(self.grid)

    def sem(self, extra=()):
        return ("parallel",) * self.ngrid + tuple(extra)

    def _ix(self, fn):
        n = self.ngrid
        return lambda *g: fn(*g[:n])

    def raster(self, arr, c, cblock=0, lead=None):
        pre_shape = () if lead is None else (arr.shape[0],)
        pre_blk = () if lead is None else (1,)
        pre_ix = () if lead is None else (lead,)
        if self.mode == "rows":
            return arr, pl.BlockSpec(pre_blk + (self.tm, c), self._ix(lambda i: pre_ix + (i, cblock)))
        a4 = arr.reshape(pre_shape + (self.nb, GRID_W, GRID_W, arr.shape[-1]))
        return a4, pl.BlockSpec(pre_blk + (1, self.ib, self.jb, c),
                                self._ix(lambda b, i, j: pre_ix + (b, i, j, cblock)))

    def colmajor(self, arr, c, cblock=0, lead=None):
        assert self.mode == "cols"
        pre_shape = () if lead is None else (arr.shape[0],)
        pre_blk = () if lead is None else (1,)
        pre_ix = () if lead is None else (lead,)
        a4 = arr.reshape(pre_shape + (self.nb, GRID_W, GRID_W, arr.shape[-1]))
        return a4, pl.BlockSpec(pre_blk + (1, self.jb, self.ib, c),
                                self._ix(lambda b, i, j: pre_ix + (b, j, i, cblock)))

    def native(self, arr, c, cblock=0, lead=None):
        if self.mode == "cols":
            return self.colmajor(arr, c, cblock, lead)
        return self.raster(arr, c, cblock, lead)

    def mod(self, arr):
        d = arr.shape[-1]
        if not self.per_batch_mod:
            return arr, pl.BlockSpec((1, 1, d), self._ix(lambda *g: (0, 0, 0)))
        if self.mode == "rows":
            tps = self.tps
            return arr, pl.BlockSpec((1, 1, d), self._ix(lambda i: (i // tps, 0, 0)))
        return arr, pl.BlockSpec((1, 1, d), self._ix(lambda b, i, j: (b, 0, 0)))

    def const(self, arr):
        nd = arr.ndim
        return arr, pl.BlockSpec(arr.shape, self._ix(lambda *g: (0,) * nd))

    def out_raster(self, n_rows, c, dtype):
        if self.mode == "rows":
            return (jax.ShapeDtypeStruct((n_rows, c), dtype),
                    pl.BlockSpec((self.tm, c), self._ix(lambda i: (i, 0))))
        return (jax.ShapeDtypeStruct((self.nb, GRID_W, GRID_W, c), dtype),
                pl.BlockSpec((1, self.ib, self.jb, c), self._ix(lambda b, i, j: (b, i, j, 0))))


def _out_proj_call(lay, prologue, pro_inputs, w_out, bias, tail, *, alpha, name):
    d = w_out.shape[1]
    n_rows = lay.nb * lay.seq
    emit_u = "scale" in tail
    n_exp = tail.get("n_exp", 0)
    names, args, specs = [], [], []

    def add(nm, pair):
        names.append(nm)
        args.append(pair[0])
        specs.append(pair[1])

    for k, pair in enumerate(pro_inputs):
        add(f"p{k}", pair)
    add("w", lay.const(w_out))
    if bias is not None:
        add("bias", lay.const(bias.reshape(1, d).astype(F32)))
    if lay.mode == "cols":
        add("perm", lay.const(_perm_mat(lay.ib, lay.jb).T))
    add("h", lay.raster(tail["h"], d))
    add("gate", lay.mod(tail["gate"]))
    add("ln_g", lay.const(tail["ln_g"].reshape(1, d)))
    add("ln_b", lay.const(tail["ln_b"].reshape(1, d)))
    if emit_u:
        add("scale", lay.mod(tail["scale"]))
        add("shift", lay.mod(tail["shift"]))
    if n_exp:
        add("router", lay.const(tail["router"]))
    out_names, out_shapes, out_specs = [], [], []

    def add_out(nm, pair):
        out_names.append(nm)
        out_shapes.append(pair[0])
        out_specs.append(pair[1])

    add_out("h_out", lay.out_raster(n_rows, d, F32))
    if emit_u:
        add_out("u_out", lay.out_raster(n_rows, d, BF16))
    if n_exp:
        add_out("comb_out", lay.out_raster(n_rows, V7X_LANES, F32))
    n_pro = len(pro_inputs)
    n_in = len(names)

    def kernel(*refs):
        r = dict(zip(names + out_names, refs))
        z = prologue(*[refs[k] for k in range(n_pro)])
        z = z.astype(BF16)
        if lay.mode == "cols":
            z = jnp.dot(r["perm"][...], z, preferred_element_type=F32).astype(BF16)
        y = jnp.dot(z, r["w"][...], preferred_element_type=F32)
        if bias is not None:
            y = y + r["bias"][...]
        _finish_rows(r, y, alpha=alpha, emit_u=emit_u, n_exp=n_exp)

    outs = pl.pallas_call(
        kernel, grid=lay.grid, in_specs=specs, out_specs=out_specs, out_shape=out_shapes,
        compiler_params=_cparams(lay.sem()), name=name)(*args)
    outs = [o.reshape(n_rows, o.shape[-1]) for o in outs]
    res = {"h": outs[0]}
    if emit_u:
        res["u"] = outs[1]
    if n_exp:
        res["comb"] = outs[2]
    return res


def _modulate_kernel(h_ref, scale_ref, shift_ref, u_ref):
    u_ref[...] = (h_ref[...] * (1.0 + scale_ref[0]) + shift_ref[0]).astype(u_ref.dtype)


def _modulate(lay, h, scale, shift):
    n_rows, d = h.shape
    pairs = [lay.raster(h, d), lay.mod(scale), lay.mod(shift)]
    out = lay.out_raster(n_rows, d, BF16)
    return pl.pallas_call(
        _modulate_kernel, grid=lay.grid, in_specs=[pr[1] for pr in pairs], out_specs=out[1], out_shape=out[0],
        compiler_params=_cparams(lay.sem()), name="adaln_modulate")(*[pr[0] for pr in pairs])


def _ffn_kernel(*refs, names, n_f, alpha, emit_u):
    r = dict(zip(names, refs))
    f = pl.program_id(1)

    @pl.when(f == 0)
    def _():
        r["acc"][...] = jnp.zeros_like(r["acc"])

    u = r["u"][...]
    gate = jnp.dot(u, r["w1"][...], preferred_element_type=F32)
    up = jnp.dot(u, r["w3"][...], preferred_element_type=F32)
    hid = _silu(gate) * up
    r["acc"][...] += jnp.dot(hid.astype(BF16), r["w2"][...], preferred_element_type=F32)

    @pl.when(f == n_f - 1)
    def _():
        _finish_rows(r, r["acc"][...], alpha=alpha, emit_u=emit_u, n_exp=0)


def _ffn_call(lay, u, w13, w2, tail, *, alpha, name):
    d, f2 = w13.shape
    fdim = f2 // 2
    tf = _tile(fdim, 1536, V7X_LANES)
    n_f = fdim // tf
    n_rows = lay.nb * lay.seq
    emit_u = "scale" in tail
    names, args, specs = [], [], []

    def add(nm, pair):
        names.append(nm)
        args.append(pair[0])
        specs.append(pair[1])

    add("u", lay.raster(u, d))
    add("w1", (w13, pl.BlockSpec((d, tf), lambda i, f: (0, f))))
    add("w3", (w13, pl.BlockSpec((d, tf), lambda i, f: (0, n_f + f))))
    add("w2", (w2, pl.BlockSpec((tf, d), lambda i, f: (f, 0))))
    add("h", lay.raster(tail["h"], d))
    add("gate", lay.mod(tail["gate"]))
    add("ln_g", lay.const(tail["ln_g"].reshape(1, d)))
    add("ln_b", lay.const(tail["ln_b"].reshape(1, d)))
    if emit_u:
        add("scale", lay.mod(tail["scale"]))
        add("shift", lay.mod(tail["shift"]))
    out_names = ["h_out"] + (["u_out"] if emit_u else [])
    outs = [lay.out_raster(n_rows, d, F32)] + ([lay.out_raster(n_rows, d, BF16)] if emit_u else [])
    kernel = functools.partial(_ffn_kernel, names=names + out_names + ["acc"], n_f=n_f, alpha=alpha, emit_u=emit_u)
    res = pl.pallas_call(
        kernel, grid=lay.grid + (n_f,), in_specs=specs,
        out_specs=[o[1] for o in outs], out_shape=[o[0] for o in outs],
        scratch_shapes=[pltpu.VMEM((lay.rows, d), F32)],
        compiler_params=_cparams(lay.sem(("arbitrary",))), name=name)(*args)
    out = {"h": res[0]}
    if emit_u:
        out["u"] = res[1]
    return out


MOE_TOKENS = 1024
MOE_ROWS = 256


def _moe_kernel(cnt_ref, u_ref, comb_ref, rank_ref, rankt_ref, w1_ref, w3_ref, w2_ref, out_ref, xe_ref, ye_ref, cw_ref,
                *, n_f, tb, rc):
    blk, e, f = pl.program_id(0), pl.program_id(1), pl.program_id(2)
    n_groups = lax.shift_right_logical(cnt_ref[blk, e] + (rc - 1), rc.bit_length() - 1)
    lane = lax.broadcasted_iota(jnp.int32, (rc, V7X_LANES), 1)

    @pl.when((e == 0) & (f == 0))
    def _():
        out_ref[...] = jnp.zeros_like(out_ref)

    @pl.when(f == 0)
    def _():
        want = rankt_ref[0, pl.ds(e, 1), :]
        comb = comb_ref[...]
        comb_hi = comb.astype(BF16)
        comb_lo = (comb - comb_hi.astype(F32)).astype(BF16)
        slot = lax.broadcasted_iota(jnp.int32, (rc, tb), 0).astype(F32)

        def gather(g, carry):
            rows = pl.ds(pl.multiple_of(g * rc, rc), rc)
            onehot = (want == slot + (g * rc).astype(F32)).astype(BF16)
            xe_ref[rows, :] = jnp.dot(onehot, u_ref[...], preferred_element_type=F32).astype(BF16)
            cw_ref[rows, :] = (jnp.dot(onehot, comb_hi, preferred_element_type=F32)
                               + jnp.dot(onehot, comb_lo, preferred_element_type=F32))
            ye_ref[rows, :] = jnp.zeros((rc, ye_ref.shape[1]), F32)
            return carry

        lax.fori_loop(0, n_groups, gather, 0)

    def expert(g, carry):
        rows = pl.ds(pl.multiple_of(g * rc, rc), rc)
        x = xe_ref[rows, :]
        hid = _silu(jnp.dot(x, w1_ref[0], preferred_element_type=F32)) * jnp.dot(x, w3_ref[0], preferred_element_type=F32)
        cw = jnp.sum(jnp.where(lane == e, cw_ref[rows, :], 0.0), axis=-1, keepdims=True)
        ye_ref[rows, :] += jnp.dot((hid * cw).astype(BF16), w2_ref[0], preferred_element_type=F32)
        return carry

    lax.fori_loop(0, n_groups, expert, 0)

    @pl.when(f == n_f - 1)
    def _():
        lane_t = lax.broadcasted_iota(jnp.int32, (tb, V7X_LANES), 1)
        want_col = jnp.sum(jnp.where(lane_t == e, rank_ref[...], 0.0), axis=-1, keepdims=True)
        slot_t = lax.broadcasted_iota(jnp.int32, (tb, rc), 1).astype(F32)

        def scatter(g, carry):
            rows = pl.ds(pl.multiple_of(g * rc, rc), rc)
            onehot_t = (want_col == slot_t + (g * rc).astype(F32)).astype(BF16)
            out_ref[...] += jnp.dot(onehot_t, ye_ref[rows, :].astype(BF16), preferred_element_type=F32)
            return carry

        lax.fori_loop(0, n_groups, scatter, 0)


def _moe_call(lay, u, w13, w2, comb, tail, *, alpha, name):
    n_e, d, f2 = w13.shape
    fdim = f2 // 2
    tf = _tile(fdim, 1536, V7X_LANES)
    n_f = fdim // tf
    n_rows = u.shape[0]
    rc = min(MOE_ROWS, n_rows)
    tb = _tile(n_rows, MOE_TOKENS, rc)
    n_blk = n_rows // tb
    lanes = V7X_LANES
    assert n_e == V7X_SUBLANES and tb % rc == 0 and rc & (rc - 1) == 0
    routed = (comb[:, :n_e] > 0.0).astype(jnp.int32).reshape(n_blk, tb, n_e)
    rank = jnp.where(routed > 0, jnp.cumsum(routed, axis=1) - routed, -1).astype(F32)
    cnt = jnp.sum(routed, axis=1)
    rank_col = jnp.pad(rank.reshape(n_rows, n_e), ((0, 0), (0, lanes - n_e)), constant_values=-1.0)
    rank_row = jnp.transpose(rank, (0, 2, 1))
    grid_spec = pltpu.PrefetchScalarGridSpec(
        num_scalar_prefetch=1, grid=(n_blk, n_e, n_f),
        in_specs=[pl.BlockSpec((tb, d), lambda i, e, f, c: (i, 0)),
                  pl.BlockSpec((tb, lanes), lambda i, e, f, c: (i, 0)),
                  pl.BlockSpec((tb, lanes), lambda i, e, f, c: (i, 0)),
                  pl.BlockSpec((1, n_e, tb), lambda i, e, f, c: (i, 0, 0)),
                  pl.BlockSpec((1, d, tf), lambda i, e, f, c: (e, 0, f)),
                  pl.BlockSpec((1, d, tf), lambda i, e, f, c: (e, 0, n_f + f)),
                  pl.BlockSpec((1, tf, d), lambda i, e, f, c: (e, f, 0))],
        out_specs=pl.BlockSpec((tb, d), lambda i, e, f, c: (i, 0)),
        scratch_shapes=[pltpu.VMEM((tb, d), BF16), pltpu.VMEM((tb, d), F32), pltpu.VMEM((tb, lanes), F32)])
    y = pl.pallas_call(
        functools.partial(_moe_kernel, n_f=n_f, tb=tb, rc=rc), grid_spec=grid_spec,
        out_shape=jax.ShapeDtypeStruct((n_rows, d), F32),
        compiler_params=_cparams(("parallel", "arbitrary", "arbitrary")), name=name,
    )(cnt, u, comb, rank_col, rank_row, w13, w13, w2)
    return _tail_call(lay, y, tail, alpha=alpha, name=name + "_tail")


def _tail_call(lay, y, tail, *, alpha, name):
    n_rows, d = y.shape
    emit_u = "scale" in tail
    names, args, specs = [], [], []

    def add(nm, pair):
        names.append(nm)
        args.append(pair[0])
        specs.append(pair[1])

    add("y", lay.raster(y, d))
    add("h", lay.raster(tail["h"], d))
    add("gate", lay.mod(tail["gate"]))
    add("ln_g", lay.const(tail["ln_g"].reshape(1, d)))
    add("ln_b", lay.const(tail["ln_b"].reshape(1, d)))
    if emit_u:
        add("scale", lay.mod(tail["scale"]))
        add("shift", lay.mod(tail["shift"]))
    out_names = ["h_out"] + (["u_out"] if emit_u else [])
    outs = [lay.out_raster(n_rows, d, F32)] + ([lay.out_raster(n_rows, d, BF16)] if emit_u else [])

    def kernel(*refs):
        r = dict(zip(names + out_names, refs))
        _finish_rows(r, r["y"][...], alpha=alpha, emit_u=emit_u, n_exp=0)

    res = pl.pallas_call(
        kernel, grid=lay.grid, in_specs=specs, out_specs=[o[1] for o in outs], out_shape=[o[0] for o in outs],
        compiler_params=_cparams(lay.sem()), name=name)(*args)
    out = {"h": res[0]}
    if emit_u:
        out["u"] = res[1]
    return out


def _shift_rows(cur, prev_row, next_row):
    n = cur.shape[0]
    rows = lax.broadcasted_iota(jnp.int32, cur.shape, 0)
    up = jnp.where(rows == 0, prev_row, pltpu.roll(cur, 1, 0))
    dn = jnp.where(rows == n - 1, next_row, pltpu.roll(cur, n - 1, 0))
    return up, dn


def _halo_rows(i, tps, xp_ref, xn_ref, hb):
    t = i % tps
    prev_row = jnp.where(t == 0, 0.0, xp_ref[hb - 1:hb, :].astype(F32))
    next_row = jnp.where(t == tps - 1, 0.0, xn_ref[0:1, :].astype(F32))
    return prev_row, next_row


def _halo_rows_index(tm, n_rows, hb):
    r = tm // hb
    last = n_rows // hb - 1
    return (lambda i: jnp.maximum(i * r - 1, 0)), (lambda i: jnp.minimum((i + 1) * r, last))


def _conv3_kernel(x_ref, xp_ref, xn_ref, w_ref, b_ref, o_ref, *, tps, act):
    i = pl.program_id(0)
    cur = x_ref[...].astype(F32)
    prev_row, next_row = _halo_rows(i, tps, xp_ref, xn_ref, xp_ref.shape[0])
    up, dn = _shift_rows(cur, prev_row, next_row)
    w = w_ref[...]
    y = w[0:1] * up + w[1:2] * cur + w[2:3] * dn + b_ref[...]
    if act is not None:
        y = act(y)
    o_ref[...] = y.astype(o_ref.dtype).reshape(o_ref.shape)


def _conv3(x, w, b, seq, *, act=None, out_dtype=BF16, time_major_nb=None, name="conv3"):
    n_rows, c = x.shape
    hb = V7X_BF16_ROWS if x.dtype == BF16 else V7X_SUBLANES
    tm = _tile(seq, 512, hb)
    tps = seq // tm
    if b is None:
        b = jnp.zeros((c,), F32)
    if time_major_nb is None:
        ct = _tile(c, 1024, V7X_LANES)
        out_shape = jax.ShapeDtypeStruct((n_rows, c), out_dtype)
        out_spec = pl.BlockSpec((tm, ct), lambda i, cc: (i, cc))
    else:
        nb, d = time_major_nb
        ct = d
        out_shape = jax.ShapeDtypeStruct((c // d, seq, nb * d), out_dtype)
        out_spec = pl.BlockSpec((1, tm, d), lambda i, cc: (cc, i % tps, i // tps))
    prev, nxt = _halo_rows_index(tm, n_rows, hb)
    return pl.pallas_call(
        functools.partial(_conv3_kernel, tps=tps, act=act),
        grid=(n_rows // tm, c // ct),
        in_specs=[pl.BlockSpec((tm, ct), lambda i, cc: (i, cc)),
                  pl.BlockSpec((hb, ct), lambda i, cc: (prev(i), cc)),
                  pl.BlockSpec((hb, ct), lambda i, cc: (nxt(i), cc)),
                  pl.BlockSpec((3, ct), lambda i, cc: (0, cc)),
                  pl.BlockSpec((1, ct), lambda i, cc: (0, cc))],
        out_specs=out_spec, out_shape=out_shape,
        compiler_params=_cparams(("parallel", "parallel")), name=name,
    )(x, x, x, w.astype(F32), b.reshape(1, c).astype(F32))


def _hy_filter_kernel(bands_ref, w1_ref, b1_ref, w2_ref, b2_ref, fr_ref, w3_ref, dec_ref, sum_ref, dif_ref,
                      *, seq, tl, d):
    i = pl.program_id(0)
    pos = (lax.broadcasted_iota(jnp.int32, (tl, V7X_LANES), 0) + i * tl).astype(F32)
    lane = lax.broadcasted_iota(jnp.int32, (tl, V7X_LANES), 1)
    t01 = pos / float(max(seq - 1, 1))
    ang = (2.0 * math.pi / seq) * pos * bands_ref[...]
    feats = jnp.where(lane == 0, t01, jnp.where(lane <= HY_BANDS, jnp.cos(ang), -jnp.sin(ang)))
    fr = fr_ref[...]
    h = jnp.sin(fr[0:1] * (_hdot(feats, w1_ref[...]) + b1_ref[...]))
    h = jnp.sin(fr[1:2] * (_hdot(h, w2_ref[...]) + b2_ref[...]))
    k = _hdot(h, w3_ref[...]) * jnp.exp(-t01[:, 0:1] * jnp.abs(dec_ref[...]))
    not_first = (pos[:, 0:1] > 0.0).astype(F32)
    for o in range(2):
        kf = k[:, (2 * o) * d:(2 * o + 1) * d]
        kb = k[:, (2 * o + 1) * d:(2 * o + 2) * d] * not_first
        sum_ref[:, o * d:(o + 1) * d] = (kf + kb).astype(sum_ref.dtype)
        dif_ref[:, o * d:(o + 1) * d] = (kb - kf).astype(dif_ref.dtype)


def _hy_filters(seq, p, d):
    lanes = V7X_LANES
    fw = p["hy_f_w1"].shape[1]
    bands = jnp.linspace(1e-4, HY_BANDS - 1, HY_BANDS, dtype=F32)
    bands_row = _pad_to(jnp.concatenate([jnp.zeros((1,), F32), bands, bands])[None, :], 1, lanes)
    w1 = _pad_to(_pad_to(p["hy_f_w1"].astype(F32), 0, lanes), 1, lanes)
    b1 = _pad_to(p["hy_f_b1"].astype(F32)[None, :], 1, lanes)
    w2 = _pad_to(_pad_to(p["hy_f_w2"].astype(F32), 0, lanes), 1, lanes)
    b2 = _pad_to(p["hy_f_b2"].astype(F32)[None, :], 1, lanes)
    fr = _pad_to(_pad_to(p["hy_f_freq"].astype(F32), 1, lanes), 0, V7X_SUBLANES)
    w3 = _pad_to(p["hy_f_w3"].astype(F32), 0, lanes)
    dec = p["hy_decay"].astype(F32).reshape(1, 4 * d)
    assert fw <= lanes
    tl = _tile(seq, 256, V7X_BF16_ROWS)
    full = lambda a: pl.BlockSpec(a.shape, lambda i: (0,) * a.ndim)
    ins = [bands_row, w1, b1, w2, b2, fr, w3, dec]
    return pl.pallas_call(
        functools.partial(_hy_filter_kernel, seq=seq, tl=tl, d=d),
        grid=(seq // tl,),
        in_specs=[full(a) for a in ins],
        out_specs=[pl.BlockSpec((tl, 2 * d), lambda i: (i, 0))] * 2,
        out_shape=[jax.ShapeDtypeStruct((seq, 2 * d), BF16)] * 2,
        compiler_params=_cparams(("parallel",)), name="hyena_filters",
    )(*ins)


def _dft_kernel(c_ref, s_ref, ct_ref, st_ref, *, seq, tr):
    i = pl.program_id(0)
    row = lax.broadcasted_iota(jnp.int32, (tr, seq), 0) + i * tr
    col = lax.broadcasted_iota(jnp.int32, (tr, seq), 1)
    scale = math.pi / (2 * seq)
    m = ((2 * row + 1) * col) & (4 * seq - 1)
    ang = m.astype(F32) * scale
    c_ref[...] = jnp.cos(ang).astype(BF16)
    s_ref[...] = jnp.sin(ang).astype(BF16)
    mt = ((2 * col + 1) * row) & (4 * seq - 1)
    angt = mt.astype(F32) * scale
    ct_ref[...] = jnp.cos(angt).astype(BF16)
    st_ref[...] = jnp.sin(angt).astype(BF16)


def _dft_mats(seq):
    assert seq & (seq - 1) == 0, "token count must be a power of two"
    tr = _tile(seq, 256, V7X_BF16_ROWS)
    spec = pl.BlockSpec((tr, seq), lambda i: (i, 0))
    return pl.pallas_call(
        functools.partial(_dft_kernel, seq=seq, tr=tr), grid=(seq // tr,), in_specs=[],
        out_specs=[spec] * 4, out_shape=[jax.ShapeDtypeStruct((seq, seq), BF16)] * 4,
        compiler_params=_cparams(("parallel",)), name="dft_matrices")()


def _hy_fwd_kernel(c_ref, s_ref, v_ref, kr_ref, ki_ref, wr_ref, wi_ref):
    v = v_ref[0]
    cv = jnp.dot(c_ref[...], v, preferred_element_type=F32)
    sv = jnp.dot(s_ref[...], v, preferred_element_type=F32)
    kr, ki = kr_ref[...], ki_ref[...]
    wr_ref[...] = (cv * kr + sv * ki).astype(wr_ref.dtype)
    wi_ref[...] = (cv * ki - sv * kr).astype(wi_ref.dtype)


def _hy_inv_kernel(ct_ref, st_ref, wr_ref, wi_ref, v_ref, g_ref, skip_ref, o_ref, *, seq):
    y = (jnp.dot(ct_ref[...], wr_ref[...], preferred_element_type=F32)
         - jnp.dot(st_ref[...], wi_ref[...], preferred_element_type=F32)) * (1.0 / seq)
    y = y + v_ref[0].astype(F32) * skip_ref[...]
    o_ref[0] = (g_ref[0].astype(F32) * y).astype(o_ref.dtype)


def _hy_long_conv(vsrc, v_idx, gsrc, g_idx, mats, kr, ki, order, skip, d):
    c, s, ct, st = mats
    _, seq, cols = vsrc.shape
    tm = _tile(seq, 512, V7X_BF16_ROWS)
    tn = _tile(d, 512, V7X_LANES)
    cpd = d // tn
    kspec = pl.BlockSpec((tm, tn), lambda i, j: (i, order * cpd + j % cpd))
    wr, wi = pl.pallas_call(
        _hy_fwd_kernel, grid=(seq // tm, cols // tn),
        in_specs=[pl.BlockSpec((tm, seq), lambda i, j: (i, 0)), pl.BlockSpec((tm, seq), lambda i, j: (i, 0)),
                  pl.BlockSpec((1, seq, tn), lambda i, j: (v_idx, 0, j)), kspec, kspec],
        out_specs=[pl.BlockSpec((tm, tn), lambda i, j: (i, j))] * 2,
        out_shape=[jax.ShapeDtypeStruct((seq, cols), BF16)] * 2,
        compiler_params=_cparams(("parallel", "parallel")), name="hyena_dft_fwd",
    )(c, s, vsrc, kr, ki)
    return pl.pallas_call(
        functools.partial(_hy_inv_kernel, seq=seq), grid=(seq // tm, cols // tn),
        in_specs=[pl.BlockSpec((tm, seq), lambda i, j: (i, 0)), pl.BlockSpec((tm, seq), lambda i, j: (i, 0)),
                  pl.BlockSpec((seq, tn), lambda i, j: (0, j)), pl.BlockSpec((seq, tn), lambda i, j: (0, j)),
                  pl.BlockSpec((1, tm, tn), lambda i, j: (v_idx, i, j)),
                  pl.BlockSpec((1, tm, tn), lambda i, j: (g_idx, i, j)),
                  pl.BlockSpec((1, tn), lambda i, j: (0, j % cpd))],
        out_specs=pl.BlockSpec((1, tm, tn), lambda i, j: (0, i, j)),
        out_shape=jax.ShapeDtypeStruct((1, seq, cols), BF16),
        compiler_params=_cparams(("parallel", "parallel")), name="hyena_dft_inv",
    )(ct, st, wr, wi, vsrc, gsrc, skip.reshape(1, d).astype(F32))


def _hyena_mixer(u, nb, seq, p, w):
    d = u.shape[1]
    proj = _matmul(u, w["hy_w_in"], p["hy_b_in"], out_dtype=BF16, name="hyena_in_proj")
    planes = _conv3(proj, p["hy_conv_w"], p["hy_conv_b"], seq, time_major_nb=(nb, d), name="hyena_short_conv")
    ksum, kdif = _hy_filters(seq, p, d)
    mats = _dft_mats(seq)
    kr = _matmul(mats[0], ksum, out_dtype=F32, tn=512, name="hyena_filter_spec_re")
    ki = _matmul(mats[1], kdif, out_dtype=F32, tn=512, name="hyena_filter_spec_im")
    z1 = _hy_long_conv(planes, 0, planes, 1, mats, kr, ki, 0, p["hy_skip"][0], d)
    return _hy_long_conv(z1, 0, planes, 2, mats, kr, ki, 1, p["hy_skip"][1], d)


def _hyena_out(lay_rows, z, tail, p, w, nb, seq, *, alpha, name):
    d = w["hy_w_out"].shape[0]
    tm = lay_rows.tm
    tps = lay_rows.tps
    spec = pl.BlockSpec((1, tm, d), lay_rows._ix(lambda i: (0, i % tps, i // tps)))
    return _out_proj_call(lay_rows, lambda z_ref: z_ref[0], [(z, spec)], w["hy_w_out"], p["hy_b_out"], tail,
                          alpha=alpha, name=name)


def _rw_proj_kernel(*refs, names, tps, hd):
    r = dict(zip(names, refs))
    i = pl.program_id(0)
    cur = r["u"][...].astype(F32)
    prev_row, next_row = _halo_rows(i, tps, r["up"], r["un"], V7X_BF16_ROWS)
    up, dn = _shift_rows(cur, prev_row, next_row)
    xx = 0.5 * (up + dn) - cur
    mu = r["mu"][...]
    mix = lambda j: (cur + xx * mu[j:j + 1]).astype(BF16)
    xr, xw, xk, xv, xa, xg = [mix(j) for j in range(6)]
    qq = r["qq"][...]
    rr = jnp.dot(xr, r["w_rkv"][0], preferred_element_type=F32)
    kk0 = jnp.dot(xk, r["w_rkv"][1], preferred_element_type=F32)
    vv = jnp.dot(xv, r["w_rkv"][2], preferred_element_type=F32)
    gg = _bdot(jax.nn.sigmoid(jnp.dot(xg, r["g1"][...], preferred_element_type=F32)), r["g2"][...])
    kx = kk0 * r["k_k"][...]
    kkn = kx * lax.rsqrt(_head_sum(kx * kx, qq) + 1e-6)
    r["r"][...] = rr.astype(BF16)
    r["v"][...] = vv.astype(BF16)
    r["g"][...] = gg.astype(BF16)
    r["na"][...] = (-kkn).astype(BF16)
    kd_sum = jnp.zeros_like(kk0)
    for dd in range(2):
        hw = jnp.tanh(jnp.dot(xw, r["lora_w1"][dd], preferred_element_type=F32))
        wpre = r["bias_w"][dd] + _bdot(hw, r["lora_w2"][dd])
        logw = -_softplus(-wpre) - 0.5
        r["lw"][dd] = -jnp.exp(logw)
        a = jax.nn.sigmoid(
            r["bias_a"][dd] + _bdot(jnp.dot(xa, r["lora_a1"][dd], preferred_element_type=F32), r["lora_a2"][dd]))
        kd = kk0 * (1.0 + (a - 1.0) * r["k_a"][...])
        r["kd"][dd] = kd.astype(BF16)
        r["b"][dd] = (kkn * a).astype(BF16)
        kd_sum = kd_sum + kd
    r["bonus"][...] = (_head_sum(rr * kd_sum * r["r_k"][...], qq) * vv).astype(BF16)


_RW_OUTS = ["r", "v", "g", "na", "bonus", "lw", "kd", "b"]
_RW_STACKED = ("lw", "kd", "b")


def _rw_project(u, seq, p, w):
    n_rows, d = u.shape
    hb = V7X_BF16_ROWS
    tm = _tile(seq, 256, hb)
    tps = seq // tm
    lanes = V7X_LANES
    names, args, specs = [], [], []

    def add(nm, arr, spec=None):
        names.append(nm)
        args.append(arr)
        nd = arr.ndim
        specs.append(spec if spec is not None else pl.BlockSpec(arr.shape, lambda i: (0,) * nd))

    prev, nxt = _halo_rows_index(tm, n_rows, hb)
    add("u", u, pl.BlockSpec((tm, d), lambda i: (i, 0)))
    add("up", u, pl.BlockSpec((hb, d), lambda i: (prev(i), 0)))
    add("un", u, pl.BlockSpec((hb, d), lambda i: (nxt(i), 0)))
    add("mu", _pad_to(p["rw_mu"].astype(F32), 0, V7X_SUBLANES))
    add("w_rkv", w["rw_w_rkv"])
    add("g1", w["rw_g1"])
    add("g2", w["rw_g2"])
    add("bias_w", p["rw_w0"].astype(F32).reshape(2, 1, d))
    add("lora_w1", w["rw_w1"])
    add("lora_w2", w["rw_w2"])
    add("bias_a", p["rw_a0"].astype(F32).reshape(2, 1, d))
    add("lora_a1", w["rw_a1"])
    add("lora_a2", w["rw_a2"])
    add("k_k", p["rw_k_k"].astype(F32).reshape(1, d))
    add("k_a", p["rw_k_a"].astype(F32).reshape(1, d))
    add("r_k", p["rw_r_k"].astype(F32).reshape(1, d))
    add("qq", _expand_mat(V7X_LANES, d // RW_HEAD, RW_HEAD))
    out_dt = {nm: (F32 if nm == "lw" else BF16) for nm in _RW_OUTS}
    row_spec = pl.BlockSpec((tm, d), lambda i: (i, 0))
    dir_spec = pl.BlockSpec((2, tm, d), lambda i: (0, i, 0))
    outs = pl.pallas_call(
        functools.partial(_rw_proj_kernel, names=names + _RW_OUTS, tps=tps, hd=RW_HEAD),
        grid=(n_rows // tm,), in_specs=specs,
        out_specs=[dir_spec if nm in _RW_STACKED else row_spec for nm in _RW_OUTS],
        out_shape=[jax.ShapeDtypeStruct(((2,) if nm in _RW_STACKED else ()) + (n_rows, d), out_dt[nm])
                   for nm in _RW_OUTS],
        compiler_params=_cparams(("parallel",)), name="rwkv7_projections")(*args)
    return dict(zip(_RW_OUTS, outs))


TRI_BASE = 8


def _unit_tri_inverse(nmat, cl):
    ii = lax.broadcasted_iota(jnp.int32, (cl, cl), 0)
    jj = lax.broadcasted_iota(jnp.int32, (cl, cl), 1)
    zero = jnp.float32(0.0)
    ident = (ii == jj).astype(F32)
    block_gap = lambda k: lax.shift_right_logical(ii ^ jj, k)
    k0 = TRI_BASE.bit_length() - 1
    n0 = [jnp.where(block_gap(k0) == 0, n, zero) for n in nmat]
    minv = [ident + n for n in n0]
    pw = n0
    for _ in range(k0 - 1):
        pw = [_bdot(x, x) for x in pw]
        minv = [m + _bdot(m, x) for m, x in zip(minv, pw)]
    k = k0
    while (1 << k) < cl:
        off = block_gap(k) == 1
        minv = [m + _bdot(_bdot(m, jnp.where(off, n, zero)), m) for m, n in zip(minv, nmat)]
        k += 1
    return minv


def _rw_scan_kernel(r_ref, lw_ref, k_ref, v_ref, a_ref, b_ref, s0_ref, y_ref, sfin_ref, s_ref, *, n_chunks, hd):
    dirn, c = pl.program_id(1), pl.program_id(2)
    cl, d = r_ref.shape[1], r_ref.shape[2]
    pw_ = 2 * hd
    n_pairs = d // pw_

    @pl.when(c == 0)
    def _():
        s_ref[...] = s0_ref[0, 0]

    incl, strict = _time_masks(dirn, cl)
    lw = lw_ref[0, 0]
    cum = _hdot(incl.astype(F32), lw)
    p_tot = jnp.exp(jnp.sum(lw, axis=0, keepdims=True))
    p_inv = jnp.exp(-cum)
    at = a_ref[0] * jnp.exp(cum - lw)
    rt = r_ref[0] * jnp.exp(cum)
    bt = (b_ref[0, 0] * p_inv).astype(BF16)
    kt = (k_ref[0, 0] * p_inv).astype(BF16)
    v = v_ref[0]
    hd_shift = hd.bit_length() - 1
    head_of = lambda shape, axis: lax.shift_right_logical(lax.broadcasted_iota(jnp.int32, shape, axis), hd_shift)
    lane = head_of((cl, pw_), 1)
    lane2 = head_of((2 * cl, pw_), 1)
    blk = head_of((pw_, pw_), 0) == head_of((pw_, pw_), 1)
    zero = jnp.float32(0.0)
    pairs = range(n_pairs)
    halves = [(pr, hf) for pr in pairs for hf in range(2)]
    sl = lambda x, pr: x[:, pr * pw_:(pr + 1) * pw_]
    s_old = [s_ref[pr] for pr in pairs]
    s_bf = [s.astype(BF16) for s in s_old]
    xs = [jnp.concatenate([sl(at, pr), sl(rt, pr)], axis=0) for pr in pairs]
    xs0 = [_bdot_nt(xs[pr], s_bf[pr]) for pr in pairs]
    xm = [jnp.where(lane2 == hf, xs[pr], zero).astype(BF16) for pr, hf in halves]
    gb = [lax.dot_general(xm[i], sl(bt, pr), NT_DIMS, preferred_element_type=F32) for i, (pr, hf) in enumerate(halves)]
    gk = [lax.dot_general(xm[i], sl(kt, pr), NT_DIMS, preferred_element_type=F32) for i, (pr, hf) in enumerate(halves)]
    minv = _unit_tri_inverse([jnp.where(strict, g[:cl], zero) for g in gb], cl)
    vm = [jnp.where(lane == hf, sl(v, pr), zero).astype(BF16) for pr, hf in halves]
    rhs = [jnp.where(lane == hf, xs0[pr][:cl], zero)
           + jnp.dot(jnp.where(strict, gk[i][:cl], zero).astype(BF16), vm[i], preferred_element_type=F32)
           for i, (pr, hf) in enumerate(halves)]
    u = [_bdot(minv[i], rhs[i]) for i in range(len(halves))]
    u_b = [x.astype(BF16) for x in u]
    yh = [jnp.dot(jnp.where(incl, gb[i][cl:], zero).astype(BF16), u_b[i], preferred_element_type=F32)
          + jnp.dot(jnp.where(incl, gk[i][cl:], zero).astype(BF16), vm[i], preferred_element_type=F32)
          for i in range(len(halves))]
    for pr in pairs:
        y_ref[0, 0, :, pr * pw_:(pr + 1) * pw_] = (xs0[pr][cl:] + yh[2 * pr] + yh[2 * pr + 1]).astype(y_ref.dtype)
    for pr in pairs:
        uv = jnp.concatenate([u[2 * pr] + u[2 * pr + 1], sl(v, pr)], axis=0).astype(BF16)
        bk = jnp.concatenate([sl(bt, pr), sl(kt, pr)], axis=0)
        delta = lax.dot_general(uv, bk, TN_DIMS, preferred_element_type=F32)
        s_ref[pr] = (s_old[pr] + jnp.where(blk, delta, zero)) * sl(p_tot, pr)

    @pl.when(c == n_chunks - 1)
    def _():
        sfin_ref[0, 0] = s_ref[...]


def _rw_scan(q, s0, nb, seq, d):
    hd = RW_HEAD
    pw_ = 2 * hd
    assert pw_ == V7X_LANES and d % pw_ == 0
    n_pairs = d // pw_
    n_chunks = seq // CHUNK
    cidx = lambda dd, c: c + dd * (n_chunks - 1 - 2 * c)
    v3 = lambda a: a.reshape(nb, seq, d)
    v4 = lambda a: a.reshape(2, nb, seq, d)
    tok = pl.BlockSpec((1, CHUNK, d), lambda b, dd, c: (b, cidx(dd, c), 0))
    tok_d = pl.BlockSpec((1, 1, CHUNK, d), lambda b, dd, c: (dd, b, cidx(dd, c), 0))
    st_spec = pl.BlockSpec((1, 1, n_pairs, pw_, pw_), lambda b, dd, c: (b, dd, 0, 0, 0))
    y, sfin = pl.pallas_call(
        functools.partial(_rw_scan_kernel, n_chunks=n_chunks, hd=hd),
        grid=(nb, 2, n_chunks),
        in_specs=[tok, tok_d, tok_d, tok, tok, tok_d, st_spec],
        out_specs=[tok_d, st_spec],
        out_shape=[jax.ShapeDtypeStruct((2, nb, seq, d), BF16),
                   jax.ShapeDtypeStruct((nb, 2, n_pairs, pw_, pw_), F32)],
        scratch_shapes=[pltpu.VMEM((n_pairs, pw_, pw_), F32)],
        compiler_params=_cparams(("parallel", "arbitrary", "arbitrary")), name="rwkv7_scan",
    )(v3(q["r"]), v4(q["lw"]), v4(q["kd"]), v3(q["v"]), v3(q["na"]), v4(q["b"]), s0)
    return y, sfin


def _rw_out(lay, y, q, tail, p, w, *, alpha, name):
    d = w["rw_w_out"].shape[0]
    n_rows = lay.nb * lay.seq
    y2 = y.reshape(2, n_rows, d)
    qq = _expand_mat(V7X_LANES, d // RW_HEAD, RW_HEAD)
    lnx_g = p["rw_lnx_g"].astype(F32).reshape(1, d)
    lnx_b = p["rw_lnx_b"].astype(F32).reshape(1, d)
    yspec = pl.BlockSpec((2, lay.tm, d), lay._ix(lambda i: (0, i, 0)))

    def prologue(y_ref, bonus_ref, g_ref, qq_ref, lg_ref, lb_ref):
        yy = y_ref[0].astype(F32) + y_ref[1].astype(F32)
        inv = 1.0 / RW_HEAD
        mean = _head_sum(yy, qq_ref[...]) * inv
        yc = yy - mean
        var = _head_sum(yc * yc, qq_ref[...]) * inv
        yn = yc * lax.rsqrt(var + RW_GN_EPS) * lg_ref[...] + lb_ref[...]
        return (yn + bonus_ref[...].astype(F32)) * g_ref[...].astype(F32)

    pro = [(y2, yspec), lay.raster(q["bonus"], d), lay.raster(q["g"], d), lay.const(qq), lay.const(lnx_g),
           lay.const(lnx_b)]
    return _out_proj_call(lay, prologue, pro, w["rw_w_out"], None, tail, alpha=alpha, name=name)


def _ssd_scan_kernel(x_ref, b_ref, c_ref, dt_ref, dtb_ref, a_ref, s0_ref, xp_ref, eye_ref, y_ref, sfin_ref, s_ref,
                     *, n_chunks, ng, nr, hp, ns):
    dirn, c = pl.program_id(1), pl.program_id(2)
    cl = x_ref.shape[1]

    @pl.when(c == 0)
    def _():
        s_ref[...] = s0_ref[0, 0]

    incl, _ = _time_masks(dirn, cl)
    tri = incl.astype(F32)
    eye = eye_ref[...]
    xp = xp_ref[...]
    dt = _softplus(dt_ref[0] + dtb_ref[0])
    dta = dt * a_ref[0]
    cum = _hdot(tri, dta)
    tot = jnp.sum(dta, axis=0, keepdims=True)
    cum_t = _transpose_via_eye(cum, eye)
    dt_t = _transpose_via_eye(dt, eye)
    e_in = _split_dot(jnp.exp(cum), xp)
    e_end = _split_dot(jnp.exp(tot - cum) * dt, xp)
    e_tot = _split_dot(jnp.broadcast_to(jnp.exp(tot), (V7X_SUBLANES, tot.shape[1])), xp)[0:1]
    x = x_ref[0]
    xw = (x.astype(F32) * e_end).astype(BF16)
    gw = nr * hp
    pair = 2 * hp
    lane = lax.broadcasted_iota(jnp.int32, (cl, pair), 1)
    neg = jnp.float32(-jnp.inf)
    s_all = [s_ref[g] for g in range(ng)]
    y_parts, s_new = [], []
    for g in range(ng):
        bm = b_ref[0, :, g * ns:(g + 1) * ns]
        cm = c_ref[0, :, g * ns:(g + 1) * ns]
        cb = lax.dot_general(cm, bm, NT_DIMS, preferred_element_type=F32)
        s_g = s_all[g]
        y_inter = jnp.dot(cm, s_g.astype(BF16), preferred_element_type=F32) * e_in[:, g * gw:(g + 1) * gw]
        for rp in range(nr // 2):
            y_pair = jnp.zeros((cl, pair), F32)
            xpair = x[:, g * gw + rp * pair:g * gw + (rp + 1) * pair]
            for half in range(2):
                h = g * nr + rp * 2 + half
                dec = jnp.exp(jnp.where(incl, cum[:, h:h + 1] - cum_t[h:h + 1, :], neg))
                sc = (cb * dec * dt_t[h:h + 1, :]).astype(BF16)
                xh = jnp.where((lane >= half * hp) & (lane < (half + 1) * hp), xpair, jnp.zeros_like(xpair))
                y_pair = y_pair + jnp.dot(sc, xh, preferred_element_type=F32)
            y_parts.append((g * gw + rp * pair, y_pair + y_inter[:, rp * pair:(rp + 1) * pair]))
        s_new.append(s_g * e_tot[:, g * gw:(g + 1) * gw] + lax.dot_general(
            bm, xw[:, g * gw:(g + 1) * gw], TN_DIMS, preferred_element_type=F32))
    for lo, val in y_parts:
        y_ref[0, 0, :, lo:lo + pair] = val.astype(y_ref.dtype)
    for g in range(ng):
        s_ref[g] = s_new[g]

    @pl.when(c == n_chunks - 1)
    def _():
        sfin_ref[0, 0] = s_ref[...]


def _ssd_scan(xbc, dt_raw, s0, nb, seq, p):
    ng, ns, hp = SSD_GROUPS, SSD_STATE, SSD_HEAD
    inner = xbc.shape[1] - 2 * ng * ns
    nh = inner // hp
    nr = nh // ng
    lanes = V7X_LANES
    assert ns == lanes and nh <= lanes and nr % 2 == 0
    n_chunks = seq // CHUNK
    cidx = lambda dd, c: c + dd * (n_chunks - 1 - 2 * c)
    xbc3 = xbc.reshape(nb, seq, xbc.shape[1])
    dt3 = dt_raw.reshape(nb, seq, 2 * lanes)
    a = -jnp.exp(p["ssd_A_log"].astype(F32))
    a_pad = _pad_to(a, 1, lanes).reshape(2, 1, lanes)
    dtb = _pad_to(p["ssd_dt_bias"].astype(F32), 1, lanes).reshape(2, 1, lanes)
    xp = _expand_mat(lanes, nh, hp)
    eye = jnp.eye(lanes, dtype=F32)
    bw = ng * ns
    y, sfin = pl.pallas_call(
        functools.partial(_ssd_scan_kernel, n_chunks=n_chunks, ng=ng, nr=nr, hp=hp, ns=ns),
        grid=(nb, 2, n_chunks),
        in_specs=[pl.BlockSpec((1, CHUNK, inner), lambda b, dd, c: (b, cidx(dd, c), 0)),
                  pl.BlockSpec((1, CHUNK, bw), lambda b, dd, c: (b, cidx(dd, c), inner // bw)),
                  pl.BlockSpec((1, CHUNK, bw), lambda b, dd, c: (b, cidx(dd, c), inner // bw + 1)),
                  pl.BlockSpec((1, CHUNK, lanes), lambda b, dd, c: (b, cidx(dd, c), dd)),
                  pl.BlockSpec((1, 1, lanes), lambda b, dd, c: (dd, 0, 0)),
                  pl.BlockSpec((1, 1, lanes), lambda b, dd, c: (dd, 0, 0)),
                  pl.BlockSpec((1, 1, ng, ns, nr * hp), lambda b, dd, c: (b, dd, 0, 0, 0)),
                  pl.BlockSpec(xp.shape, lambda b, dd, c: (0, 0)),
                  pl.BlockSpec(eye.shape, lambda b, dd, c: (0, 0))],
        out_specs=[pl.BlockSpec((1, 1, CHUNK, inner), lambda b, dd, c: (dd, b, cidx(dd, c), 0)),
                   pl.BlockSpec((1, 1, ng, ns, nr * hp), lambda b, dd, c: (b, dd, 0, 0, 0))],
        out_shape=[jax.ShapeDtypeStruct((2, nb, seq, inner), BF16),
                   jax.ShapeDtypeStruct((nb, 2, ng, ns, nr * hp), F32)],
        scratch_shapes=[pltpu.VMEM((ng, ns, nr * hp), F32)],
        compiler_params=_cparams(("parallel", "arbitrary", "arbitrary")), name="ssd_scan",
    )(xbc3, xbc3, xbc3, dt3, dtb, a_pad, s0, xp, eye)
    return y.reshape(2, nb * seq, inner), sfin


def _ssd_project(u, nb, seq, p, w, colmajor):
    mm = (lambda x, wt, **kw: _matmul_r2c(x, wt, None, nb, **kw)) if colmajor else (
        lambda x, wt, **kw: _matmul(x, wt, None, **kw))
    z = mm(u, w["ssd_w_z"], out_dtype=BF16, name="ssd_in_proj_z")
    xbc_raw = mm(u, w["ssd_w_xbc"], out_dtype=BF16, name="ssd_in_proj_xbc")
    dt_raw = mm(u, w["ssd_w_dt"], out_dtype=F32, name="ssd_in_proj_dt")
    xbc = _conv3(xbc_raw, p["ssd_conv_w"], p["ssd_conv_b"], seq, act=_silu, name="ssd_short_conv")
    return z, xbc, dt_raw


def _ssd_out(lay, y, z, xbc, tail, p, w, *, alpha, name):
    inner = w["ssd_w_out"].shape[0]
    ng = SSD_GROUPS
    gwid = inner // ng
    d_row = jnp.repeat(p["ssd_d_skip"].astype(F32), SSD_HEAD).reshape(1, inner)
    ng_row = p["ssd_norm_g"].astype(F32).reshape(1, inner)

    def prologue(y0_ref, y1_ref, xs_ref, z_ref, d_ref, g_ref):
        flat = lambda ref: ref[...].reshape(-1, ref.shape[-1]).astype(F32)
        yy = flat(xs_ref) * d_ref[...] + flat(y0_ref) + flat(y1_ref)
        yy = yy * _silu(flat(z_ref))
        parts = []
        for g in range(ng):
            yg = yy[:, g * gwid:(g + 1) * gwid]
            ms = jnp.mean(yg * yg, axis=-1, keepdims=True)
            parts.append(yg * lax.rsqrt(ms + 1e-6))
        return jnp.concatenate(parts, axis=1) * g_ref[...]

    pro = [lay.native(y, inner, lead=0), lay.native(y, inner, lead=1), lay.native(xbc, inner),
           lay.native(z, inner), lay.const(d_row), lay.const(ng_row)]
    return _out_proj_call(lay, prologue, pro, w["ssd_w_out"], None, tail, alpha=alpha, name=name)


def _gdn_scan_kernel(q_ref, k_ref, v_ref, a_ref, bta_ref, dtb_ref, al_ref, s0_ref, eye_ref, o_ref, sfin_ref, s_ref,
                     *, n_chunks, hk, hv, dh):
    dirn, c = pl.program_id(1), pl.program_id(2)
    cl = q_ref.shape[1]

    @pl.when(c == 0)
    def _():
        s_ref[...] = s0_ref[0, 0]

    incl, strict = _time_masks(dirn, cl)
    tri = incl.astype(F32)
    eye = eye_ref[...]
    neg = jnp.float32(-jnp.inf)
    g = al_ref[0] * _softplus(a_ref[0] + dtb_ref[0])
    beta = jax.nn.sigmoid(bta_ref[0])
    gn = _hdot(tri, g)
    gtot = jnp.sum(g, axis=0, keepdims=True)
    gn_t = _transpose_via_eye(gn, eye)
    e_in = jnp.exp(gn)
    e_end = jnp.exp(gtot - gn)
    e_tot = jnp.exp(gtot)
    rep = hv // hk
    heads = range(hv)
    s_old = [s_ref[h] for h in heads]
    s_bf = [s.astype(BF16) for s in s_old]
    qn, kn, kk, qk = [], [], [], []
    for hq in range(hk):
        qh = q_ref[0, :, hq * dh:(hq + 1) * dh].astype(F32)
        kh = k_ref[0, :, hq * dh:(hq + 1) * dh].astype(F32)
        qn.append(qh * lax.rsqrt(jnp.sum(qh * qh, axis=-1, keepdims=True) + 1e-6) * (dh ** -0.5))
        kn.append(kh * lax.rsqrt(jnp.sum(kh * kh, axis=-1, keepdims=True) + 1e-6))
        gram = _bdot_nt(jnp.concatenate([kn[hq], qn[hq]], axis=0), kn[hq])
        kk.append(gram[:cl])
        qk.append(gram[cl:])
    bh = [beta[:, h:h + 1] for h in heads]
    diff = [gn[:, h:h + 1] - gn_t[h:h + 1, :] for h in heads]
    nmat = [-(kk[h // rep] * bh[h] * jnp.exp(jnp.where(strict, diff[h], neg))) for h in heads]
    attn = [(qk[h // rep] * jnp.exp(jnp.where(incl, diff[h], neg))).astype(BF16) for h in heads]
    minv = _unit_tri_inverse(nmat, cl)
    rhs =[jnp.concatenate([v_ref[0, :, h * dh:(h + 1) * dh].astype(F32) * bh[h],
                            kn[h // rep] * (bh[h] * e_in[:, h:h + 1])], axis=1) for h in heads]
    sol = [_bdot(minv[h], rhs[h]) for h in heads]
    u_b = [(sol[h][:, :dh] - _bdot(sol[h][:, dh:], s_bf[h])).astype(BF16) for h in heads]
    out = [_bdot(qn[h // rep] * e_in[:, h:h + 1], s_bf[h]) + jnp.dot(attn[h], u_b[h], preferred_element_type=F32)
           for h in heads]
    s_new = [s_old[h] * e_tot[:, h:h + 1]
             + lax.dot_general((kn[h // rep] * e_end[:, h:h + 1]).astype(BF16), u_b[h], TN_DIMS,
                               preferred_element_type=F32) for h in heads]
    for h in heads:
        o_ref[0, 0, :, h * dh:(h + 1) * dh] = out[h].astype(o_ref.dtype)
    for h in heads:
        s_ref[h] = s_new[h]

    @pl.when(c == n_chunks - 1)
    def _():
        sfin_ref[0, 0] = s_ref[...]


def _gdn_scan(qkv, ab, s0, nb, seq, p):
    dh = GDN_HEAD
    lanes = V7X_LANES
    hv = p["gdn_dt_bias"].shape[1]
    vw = hv * dh
    qk = (qkv.shape[1] - vw) // 2
    hk = qk // dh
    assert dh == lanes and hv <= lanes
    n_chunks = seq // CHUNK
    cidx = lambda dd, c: c + dd * (n_chunks - 1 - 2 * c)
    qkv3 = qkv.reshape(nb, seq, qkv.shape[1])
    ab3 = ab.reshape(nb, seq, 4 * lanes)
    dtb = _pad_to(p["gdn_dt_bias"].astype(F32), 1, lanes).reshape(2, 1, lanes)
    al = _pad_to(-jnp.exp(p["gdn_A_log"].astype(F32)), 1, lanes).reshape(2, 1, lanes)
    eye = jnp.eye(lanes, dtype=F32)
    o, sfin = pl.pallas_call(
        functools.partial(_gdn_scan_kernel, n_chunks=n_chunks, hk=hk, hv=hv, dh=dh),
        grid=(nb, 2, n_chunks),
        in_specs=[pl.BlockSpec((1, CHUNK, qk), lambda b, dd, c: (b, cidx(dd, c), 0)),
                  pl.BlockSpec((1, CHUNK, qk), lambda b, dd, c: (b, cidx(dd, c), 1)),
                  pl.BlockSpec((1, CHUNK, vw), lambda b, dd, c: (b, cidx(dd, c), 2 * qk // vw)),
                  pl.BlockSpec((1, CHUNK, lanes), lambda b, dd, c: (b, cidx(dd, c), 2 * dd)),
                  pl.BlockSpec((1, CHUNK, lanes), lambda b, dd, c: (b, cidx(dd, c), 2 * dd + 1)),
                  pl.BlockSpec((1, 1, lanes), lambda b, dd, c: (dd, 0, 0)),
                  pl.BlockSpec((1, 1, lanes), lambda b, dd, c: (dd, 0, 0)),
                  pl.BlockSpec((1, 1, hv, dh, dh), lambda b, dd, c: (b, dd, 0, 0, 0)),
                  pl.BlockSpec(eye.shape, lambda b, dd, c: (0, 0))],
        out_specs=[pl.BlockSpec((1, 1, CHUNK, vw), lambda b, dd, c: (dd, b, cidx(dd, c), 0)),
                   pl.BlockSpec((1, 1, hv, dh, dh), lambda b, dd, c: (b, dd, 0, 0, 0))],
        out_shape=[jax.ShapeDtypeStruct((2, nb, seq, vw), BF16),
                   jax.ShapeDtypeStruct((nb, 2, hv, dh, dh), F32)],
        scratch_shapes=[pltpu.VMEM((hv, dh, dh), F32)],
        compiler_params=_cparams(("parallel", "arbitrary", "arbitrary")), name="gdn_scan",
    )(qkv3, qkv3, qkv3, ab3, ab3, dtb, al, s0, eye)
    return o.reshape(2, nb * seq, vw), sfin


def _gdn_project(u, nb, seq, p, w, colmajor):
    mm = (lambda x, wt, **kw: _matmul_r2c(x, wt, None, nb, **kw)) if colmajor else (
        lambda x, wt, **kw: _matmul(x, wt, None, **kw))
    qkv_raw = mm(u, w["gdn_w_qkv"], out_dtype=BF16, name="gdn_in_proj_qkv")
    z = mm(u, w["gdn_w_z"], out_dtype=BF16, name="gdn_in_proj_z")
    ab = mm(u, w["gdn_w_ab"], out_dtype=F32, name="gdn_in_proj_ab")
    qkv = _conv3(qkv_raw, p["gdn_conv_w"], None, seq, act=_silu, name="gdn_short_conv")
    return qkv, z, ab


def _gdn_out(lay, o, z, tail, p, w, *, alpha, name):
    vw = w["gdn_w_out"].shape[0]
    dh = GDN_HEAD
    hv = vw // dh
    g_row = jnp.tile(p["gdn_norm_g"].astype(F32), hv).reshape(1, vw)

    def prologue(o0_ref, o1_ref, z_ref, g_ref):
        flat = lambda ref: ref[...].reshape(-1, ref.shape[-1]).astype(F32)
        oo = flat(o0_ref) + flat(o1_ref)
        parts = []
        for h in range(hv):
            oh = oo[:, h * dh:(h + 1) * dh]
            ms = jnp.mean(oh * oh, axis=-1, keepdims=True)
            parts.append(oh * lax.rsqrt(ms + 1e-6))
        return jnp.concatenate(parts, axis=1) * g_ref[...] * _silu(flat(z_ref))

    pro = [lay.native(o, vw, lead=0), lay.native(o, vw, lead=1), lay.native(z, vw), lay.const(g_row)]
    return _out_proj_call(lay, prologue, pro, w["gdn_w_out"], None, tail, alpha=alpha, name=name)


def _prep_weights(p):
    lanes = V7X_LANES
    bf = lambda a: a.astype(BF16)
    d = p["hy_w_out"].shape[0]
    w = {k: bf(p[k]) for k in ("hy_w_in", "hy_w_out", "rw_w_rkv", "rw_w_out", "ssd_w_out", "gdn_w_out",
                               "ffn_w13", "ffn_w2", "moe_w13", "moe_w2", "mod_w")}
    lora = lambda a, ax: bf(_pad_to(a, ax, lanes * ((a.shape[ax] + lanes - 1) // lanes)))
    w["rw_w1"], w["rw_w2"] = lora(p["rw_w1"], 2), lora(p["rw_w2"], 1)
    w["rw_a1"], w["rw_a2"] = lora(p["rw_a1"], 2), lora(p["rw_a2"], 1)
    w["rw_g1"], w["rw_g2"] = lora(p["rw_g1"], 1), lora(p["rw_g2"], 0)
    inner = p["ssd_w_out"].shape[0]
    nbc = 2 * SSD_GROUPS * SSD_STATE
    nh = inner // SSD_HEAD
    ws = p["ssd_w_in"]
    w["ssd_w_z"] = bf(ws[:, :inner])
    w["ssd_w_xbc"] = bf(ws[:, inner:2 * inner + nbc])
    wdt = ws[:, 2 * inner + nbc:].reshape(d, 2, nh)
    w["ssd_w_dt"] = bf(_pad_to(wdt, 2, lanes).reshape(d, 2 * lanes))
    vw = p["gdn_w_out"].shape[0]
    hv = vw // GDN_HEAD
    qkvw = p["gdn_conv_w"].shape[1]
    wg = p["gdn_w_in"]
    w["gdn_w_qkv"] = bf(wg[:, :qkvw])
    w["gdn_w_z"] = bf(wg[:, qkvw:qkvw + vw])
    wab = wg[:, qkvw + vw:].reshape(d, 2, 2, hv)
    wab = jnp.transpose(wab, (0, 2, 1, 3))
    w["gdn_w_ab"] = bf(_pad_to(wab, 3, lanes).reshape(d, 4 * lanes))
    w["moe_router"] = _pad_to(p["moe_router"].astype(F32), 2, lanes)
    return w


_PARAM_NAMES = (
    "mod_w mod_b ln_g ln_b hy_w_in hy_b_in hy_conv_w hy_conv_b hy_f_w1 hy_f_b1 hy_f_w2 hy_f_b2 hy_f_freq "
    "hy_f_w3 hy_decay hy_skip hy_w_out hy_b_out rw_mu rw_w_rkv rw_w0 rw_w1 rw_w2 rw_a0 rw_a1 rw_a2 rw_g1 rw_g2 "
    "rw_k_k rw_k_a rw_r_k rw_lnx_g rw_lnx_b rw_w_out ssd_w_in ssd_conv_w ssd_conv_b ssd_dt_bias ssd_A_log "
    "ssd_d_skip ssd_norm_g ssd_w_out gdn_w_in gdn_conv_w gdn_dt_bias gdn_A_log gdn_norm_g gdn_w_out ffn_w13 "
    "ffn_w2 moe_router moe_w13 moe_w2").split()


def _forward(x, c, ctx, c_ctx, p):
    nb, seq, d = x.shape
    lc = ctx.shape[1]
    depth = p["mod_w"].shape[0]
    alpha = (2 * depth) ** 0.25
    n_exp = p["moe_router"].shape[2]
    assert seq == GRID_W * GRID_W and seq % CHUNK == 0 and lc % CHUNK == 0
    w = _prep_weights(p)

    cc = _pad_to(jnp.concatenate([c, c_ctx[None, :]], axis=0).astype(F32), 0, V7X_BF16_ROWS)
    mods = [
        _matmul(cc, w["mod_w"][i], p["mod_b"][i], pre=_silu, out_dtype=F32, name="adaln_modulation")
        .reshape(cc.shape[0], 6, d) for i in range(depth)]

    def chunk(i, k, stream):
        m = mods[i][:, k]
        return m[:nb, None, :] if stream == "x" else m[nb:nb + 1, None, :]

    lay = {"x": _RowLayout(nb, seq, True), "c": _RowLayout(nb, lc, False),
           "xcol": _RowLayout(nb, seq, True, mode="cols")}
    seqs = {"x": seq, "c": lc}
    h = {"x": x.reshape(nb * seq, d).astype(F32), "c": ctx.reshape(nb * lc, d).astype(F32)}

    u = {s: _modulate(lay[s], h[s], chunk(0, 1, s), chunk(0, 0, s)) for s in ("x", "c")}

    for i in range(depth):
        last = i == depth - 1
        kind = i % 4
        moe = i % 2 == 1
        streams = ("x",) if last else ("x", "c")

        def tail1(s):
            t = dict(h=h[s], gate=chunk(i, 2, s), ln_g=p["ln_g"][i, 0], ln_b=p["ln_b"][i, 0],
                     scale=chunk(i, 4, s), shift=chunk(i, 3, s))
            if moe:
                t.update(router=w["moe_router"][i // 2], n_exp=n_exp)
            return t

        res = {}
        if kind == 0:
            for s in streams:
                z = _hyena_mixer(u[s], nb, seqs[s], p, w)
                res[s] = _hyena_out(lay[s], z, tail1(s), p, w, nb, seqs[s], alpha=alpha, name=f"hyena_out_{s}")
        elif kind == 1:
            qc = _rw_project(u["c"], lc, p, w)
            qx = _rw_project(u["x"], seq, p, w)
            s0 = jnp.zeros((nb, 2, d // (2 * RW_HEAD), 2 * RW_HEAD, 2 * RW_HEAD), F32)
            yc, s_c = _rw_scan(qc, s0, nb, lc, d)
            yx, _ = _rw_scan(qx, s_c, nb, seq, d)
            ys, qs = {"x": yx, "c": yc}, {"x": qx, "c": qc}
            for s in streams:
                res[s] = _rw_out(lay[s], ys[s], qs[s], tail1(s), p, w, alpha=alpha, name=f"rwkv7_out_{s}")
        elif kind == 2:
            zc, xbc_c, dt_c = _ssd_project(u["c"], nb, lc, p, w, False)
            zx, xbc_x, dt_x = _ssd_project(u["x"], nb, seq, p, w, True)
            inner = w["ssd_w_out"].shape[0]
            s0 = jnp.zeros((nb, 2, SSD_GROUPS, SSD_STATE, inner // SSD_GROUPS), F32)
            yc, s_c = _ssd_scan(xbc_c, dt_c, s0, nb, lc, p)
            yx, _ = _ssd_scan(xbc_x, dt_x, s_c, nb, seq, p)
            res["x"] = _ssd_out(lay["xcol"], yx, zx, xbc_x, tail1("x"), p, w, alpha=alpha, name="ssd_out_x")
            if not last:
                res["c"] = _ssd_out(lay["c"], yc, zc, xbc_c, tail1("c"), p, w, alpha=alpha, name="ssd_out_c")
        else:
            qkv_c, zc, ab_c = _gdn_project(u["c"], nb, lc, p, w, False)
            qkv_x, zx, ab_x = _gdn_project(u["x"], nb, seq, p, w, True)
            vw = w["gdn_w_out"].shape[0]
            s0 = jnp.zeros((nb, 2, vw // GDN_HEAD, GDN_HEAD, GDN_HEAD), F32)
            oc, s_c = _gdn_scan(qkv_c, ab_c, s0, nb, lc, p)
            ox, _ = _gdn_scan(qkv_x, ab_x, s_c, nb, seq, p)
            res["x"] = _gdn_out(lay["xcol"], ox, zx, tail1("x"), p, w, alpha=alpha, name="gdn_out_x")
            if not last:
                res["c"] = _gdn_out(lay["c"], oc, zc, tail1("c"), p, w, alpha=alpha, name="gdn_out_c")

        for s in streams:
            t = dict(h=res[s]["h"], gate=chunk(i, 5, s), ln_g=p["ln_g"][i, 1], ln_b=p["ln_b"][i, 1])
            if not last:
                t.update(scale=chunk(i + 1, 1, s), shift=chunk(i + 1, 0, s))
            if moe:
                out = _moe_call(lay[s], res[s]["u"], w["moe_w13"][i // 2], w["moe_w2"][i // 2], res[s]["comb"], t,
                                alpha=alpha, name=f"moe_ffn_{s}")
            else:
                out = _ffn_call(lay[s], res[s]["u"], w["ffn_w13"][i // 2], w["ffn_w2"][i // 2], t,
                                alpha=alpha, name=f"dense_ffn_{s}")
            h[s] = out["h"]
            if not last:
                u[s] = out["u"]
    return h["x"].reshape(nb, seq, d).astype(x.dtype)


def kernel(x, c, ctx, c_ctx, mod_w, mod_b, ln_g, ln_b, hy_w_in, hy_b_in, hy_conv_w, hy_conv_b, hy_f_w1, hy_f_b1, hy_f_w2, hy_f_b2, hy_f_freq, hy_f_w3, hy_decay, hy_skip, hy_w_out, hy_b_out, rw_mu, rw_w_rkv, rw_w0, rw_w1, rw_w2, rw_a0, rw_a1, rw_a2, rw_g1, rw_g2, rw_k_k, rw_k_a, rw_r_k, rw_lnx_g, rw_lnx_b, rw_w_out, ssd_w_in, ssd_conv_w, ssd_conv_b, ssd_dt_bias, ssd_A_log, ssd_d_skip, ssd_norm_g, ssd_w_out, gdn_w_in, gdn_conv_w, gdn_dt_bias, gdn_A_log, gdn_norm_g, gdn_w_out, ffn_w13, ffn_w2, moe_router, moe_w13, moe_w2):
    vals = (mod_w, mod_b, ln_g, ln_b, hy_w_in, hy_b_in, hy_conv_w, hy_conv_b, hy_f_w1, hy_f_b1, hy_f_w2, hy_f_b2,
            hy_f_freq, hy_f_w3, hy_decay, hy_skip, hy_w_out, hy_b_out, rw_mu, rw_w_rkv, rw_w0, rw_w1, rw_w2, rw_a0,
            rw_a1, rw_a2, rw_g1, rw_g2, rw_k_k, rw_k_a, rw_r_k, rw_lnx_g, rw_lnx_b, rw_w_out, ssd_w_in, ssd_conv_w,
            ssd_conv_b, ssd_dt_bias, ssd_A_log, ssd_d_skip, ssd_norm_g, ssd_w_out, gdn_w_in, gdn_conv_w,
            gdn_dt_bias, gdn_A_log, gdn_norm_g, gdn_w_out, ffn_w13, ffn_w2, moe_router, moe_w13, moe_w2)
    return _forward(x, c, ctx, c_ctx, dict(zip(_PARAM_NAMES, vals)))
```

```python
import functools
import math

import jax
import jax.numpy as jnp
import numpy as np
from jax import lax
from jax.experimental import pallas as pl
from jax.experimental.pallas import tpu as pltpu

F32 = jnp.float32
BF16 = jnp.bfloat16
HIGHEST = lax.Precision.HIGHEST

GRID_W = 64
LN_EPS = 1e-5
HY_BANDS = 8
RW_HEAD = 64
RW_GN_EPS = 64e-5
SSD_HEAD = 64
SSD_STATE = 128
SSD_GROUPS = 4
GDN_HEAD = 128
CHUNK = 64
TOP_K = 2

V7X_LANES = 128
V7X_SUBLANES = 8
V7X_BF16_ROWS = 16
V7X_VMEM_BUDGET = 56 * 1024 * 1024

NT_DIMS = (((1,), (1,)), ((), ()))
TN_DIMS = (((0,), (0,)), ((), ()))


def _tile(n, pref, mult):
    best = None
    t = mult
    while t <= min(n, pref):
        if n % t == 0:
            best = t
        t += mult
    return best if best is not None else n


def _pad_to(a, axis, size):
    pad = size - a.shape[axis]
    if pad == 0:
        return a
    cfg = [(0, 0)] * a.ndim
    cfg[axis] = (0, pad)
    return jnp.pad(a, cfg)


def _cparams(sem):
    return pltpu.CompilerParams(dimension_semantics=sem, vmem_limit_bytes=V7X_VMEM_BUDGET)


def _silu(x):
    return x * jax.nn.sigmoid(x)


def _softplus(x):
    return jnp.maximum(x, 0.0) + jnp.log1p(jnp.exp(-jnp.abs(x)))


def _bdot(a, b):
    return jnp.dot(a.astype(BF16), b.astype(BF16), preferred_element_type=F32)


def _bdot_nt(a, b):
    return lax.dot_general(a.astype(BF16), b.astype(BF16), NT_DIMS, preferred_element_type=F32)


def _hdot(a, b):
    return jnp.dot(a, b, precision=HIGHEST, preferred_element_type=F32)


def _split_dot(x, m):
    hi = x.astype(BF16)
    lo = (x - hi.astype(F32)).astype(BF16)
    return jnp.dot(hi, m, preferred_element_type=F32) + jnp.dot(lo, m, preferred_element_type=F32)


def _transpose_via_eye(x, eye):
    return lax.dot_general(eye, x, NT_DIMS, precision=HIGHEST, preferred_element_type=F32)


def _head_sum(x, expand):
    hi = x.astype(BF16)
    lo = (x - hi.astype(F32)).astype(BF16)
    s = (lax.dot_general(hi, expand, NT_DIMS, preferred_element_type=F32)
         + lax.dot_general(lo, expand, NT_DIMS, preferred_element_type=F32))
    return _split_dot(s, expand)


def _expand_mat(n_in_pad, n_heads, width):
    m = np.zeros((n_in_pad, n_heads * width), np.float32)
    for h in range(n_heads):
        m[h, h * width:(h + 1) * width] = 1.0
    return jnp.asarray(m, BF16)


def _perm_mat(ib, jb):
    n = ib * jb
    q = np.arange(n)
    p = (q % ib) * jb + q // ib
    m = np.zeros((n, n), np.float32)
    m[q, p] = 1.0
    return jnp.asarray(m, BF16)


def _time_masks(d, c):
    ii = lax.broadcasted_iota(jnp.int32, (c, c), 0)
    jj = lax.broadcasted_iota(jnp.int32, (c, c), 1)
    lag = (ii - jj) * (1 - 2 * d)
    return lag >= 0, lag > 0


def _mm_kernel(*refs, nk, pre, act, has_bias, has_perm):
    it = iter(refs)
    x_ref, w_ref = next(it), next(it)
    b_ref = next(it) if has_bias else None
    p_ref = next(it) if has_perm else None
    o_ref = next(it)
    acc_ref = next(it) if nk > 1 else None
    xp_ref = next(it) if has_perm else None

    def load_x():
        x = x_ref[...]
        x = x.reshape(-1, x.shape[-1])
        if pre is not None:
            x = pre(x.astype(F32))
        return x.astype(BF16)

    if has_perm:
        @pl.when(pl.program_id(3) == 0)
        def _():
            xp_ref[...] = jnp.dot(p_ref[...], load_x(), preferred_element_type=F32).astype(BF16)

        x = xp_ref[...]
    else:
        x = load_x()
    part = jnp.dot(x, w_ref[...], preferred_element_type=F32)

    def finish(r):
        if has_bias:
            r = r + b_ref[...]
        if act is not None:
            r = act(r)
        o_ref[...] = r.astype(o_ref.dtype).reshape(o_ref.shape)

    if nk == 1:
        finish(part)
    else:
        k = pl.program_id(2)

        @pl.when(k == 0)
        def _():
            acc_ref[...] = jnp.zeros_like(acc_ref)

        acc_ref[...] += part

        @pl.when(k == nk - 1)
        def _():
            finish(acc_ref[...])


def _matmul(x, w, bias=None, *, out_dtype=F32, act=None, pre=None, tm=512, tn=2048, tk=None, name="matmul"):
    m, kdim = x.shape
    n = w.shape[1]
    tm = _tile(m, tm, V7X_BF16_ROWS)
    tn = _tile(n, tn, V7X_LANES)
    tk = kdim if tk is None else _tile(kdim, tk, V7X_LANES)
    nk = kdim // tk
    in_specs = [pl.BlockSpec((tm, tk), lambda i, j, k: (i, k)),
                pl.BlockSpec((tk, tn), lambda i, j, k: (k, j))]
    args = [x, w]
    if bias is not None:
        in_specs.append(pl.BlockSpec((1, tn), lambda i, j, k: (0, j)))
        args.append(bias.reshape(1, n).astype(F32))
    return pl.pallas_call(
        functools.partial(_mm_kernel, nk=nk, pre=pre, act=act, has_bias=bias is not None, has_perm=False),
        grid=(m // tm, n // tn, nk),
        in_specs=in_specs,
        out_specs=pl.BlockSpec((tm, tn), lambda i, j, k: (i, j)),
        out_shape=jax.ShapeDtypeStruct((m, n), out_dtype),
        scratch_shapes=[pltpu.VMEM((tm, tn), F32)] if nk > 1 else [],
        compiler_params=_cparams(("parallel", "parallel", "arbitrary")),
        name=name,
    )(*args)


def _col_tiles(gw):
    ib = min(V7X_BF16_ROWS, gw)
    jb = min(32, gw)
    return ib, jb


def _matmul_r2c(u, w, bias, nb, *, out_dtype, tn=2048, name="matmul_r2c"):
    gw = GRID_W
    kdim, n = w.shape
    ib, jb = _col_tiles(gw)
    tn = _tile(n, tn, V7X_LANES)
    u4 = u.reshape(nb, gw, gw, kdim)
    in_specs = [pl.BlockSpec((1, ib, jb, kdim), lambda b, i, j, c: (b, i, j, 0)),
                pl.BlockSpec((kdim, tn), lambda b, i, j, c: (0, c))]
    args = [u4, w]
    if bias is not None:
        in_specs.append(pl.BlockSpec((1, tn), lambda b, i, j, c: (0, c)))
        args.append(bias.reshape(1, n).astype(F32))
    in_specs.append(pl.BlockSpec((ib * jb, ib * jb), lambda b, i, j, c: (0, 0)))
    args.append(_perm_mat(ib, jb))
    out = pl.pallas_call(
        functools.partial(_mm_kernel, nk=1, pre=None, act=None, has_bias=bias is not None, has_perm=True),
        grid=(nb, gw // ib, gw // jb, n // tn),
        in_specs=in_specs,
        out_specs=pl.BlockSpec((1, jb, ib, tn), lambda b, i, j, c: (b, j, i, c)),
        out_shape=jax.ShapeDtypeStruct((nb, gw, gw, n), out_dtype),
        scratch_shapes=[pltpu.VMEM((ib * jb, kdim), BF16)],
        compiler_params=_cparams(("parallel", "parallel", "parallel", "arbitrary")),
        name=name,
    )(*args)
    return out.reshape(nb * gw * gw, n)


def _ln_epilogue(h, y, gate, ln_g, ln_b, alpha):
    pre = alpha * h + gate * y
    mu = jnp.mean(pre, axis=-1, keepdims=True)
    xc = pre - mu
    var = jnp.mean(xc * xc, axis=-1, keepdims=True)
    return xc * lax.rsqrt(var + LN_EPS) * ln_g + ln_b


def _route(logits, n_exp):
    lane = lax.broadcasted_iota(jnp.int32, logits.shape, 1)
    neg = jnp.float32(-jnp.inf)
    lg = jnp.where(lane < n_exp, logits, neg)
    big = jnp.int32(logits.shape[1])
    m1 = jnp.max(lg, axis=-1, keepdims=True)
    i1 = jnp.min(jnp.where(lg == m1, lane, big), axis=-1, keepdims=True)
    lg2 = jnp.where(lane == i1, neg, lg)
    m2 = jnp.max(lg2, axis=-1, keepdims=True)
    i2 = jnp.min(jnp.where(lg2 == m2, lane, big), axis=-1, keepdims=True)
    e2 = jnp.exp(m2 - m1)
    den = 1.0 + e2
    return jnp.where(lane == i1, 1.0 / den, jnp.where(lane == i2, e2 / den, 0.0))


def _finish_rows(refs, y, *, alpha, emit_u, n_exp):
    h_ref, gate_ref, lng_ref, lnb_ref = refs["h"], refs["gate"], refs["ln_g"], refs["ln_b"]
    h = h_ref[...].reshape(y.shape)
    hn = _ln_epilogue(h, y, gate_ref[0], lng_ref[...], lnb_ref[...], alpha)
    refs["h_out"][...] = hn.reshape(refs["h_out"].shape)
    if emit_u:
        u = hn * (1.0 + refs["scale"][0]) + refs["shift"][0]
        refs["u_out"][...] = u.astype(BF16).reshape(refs["u_out"].shape)
        if n_exp:
            logits = _hdot(u, refs["router"][...])
            refs["comb_out"][...] = _route(logits, n_exp).reshape(refs["comb_out"].shape)


class _RowLayout:
    def __init__(self, nb, seq, per_batch_mod, mode="rows", tm=512):
        self.nb, self.seq, self.mode, self.per_batch_mod = nb, seq, mode, per_batch_mod
        if mode == "rows":
            self.tm = _tile(seq, tm, V7X_BF16_ROWS)
            self.tps = seq // self.tm
            self.grid = (nb * self.tps,)
            self.rows = self.tm
        else:
            self.ib, self.jb = _col_tiles(GRID_W)
            self.grid = (nb, GRID_W // self.ib, GRID_W // self.jb)
            self.rows = self.ib * self.jb
        self.ngrid = len(self.grid)

    def sem(self, extra=()):
        return ("parallel",) * self.ngrid + tuple(extra)

    def _ix(self, fn):
        n = self.ngrid
        return lambda *g: fn(*g[:n])

    def raster(self, arr, c, cblock=0, lead=None):
        pre_shape = () if lead is None else (arr.shape[0],)
        pre_blk = () if lead is None else (1,)
        pre_ix = () if lead is None else (lead,)
        if self.mode == "rows":
            return arr, pl.BlockSpec(pre_blk + (self.tm, c), self._ix(lambda i: pre_ix + (i, cblock)))
        a4 = arr.reshape(pre_shape + (self.nb, GRID_W, GRID_W, arr.shape[-1]))
        return a4, pl.BlockSpec(pre_blk + (1, self.ib, self.jb, c),
                                self._ix(lambda b, i, j: pre_ix + (b, i, j, cblock)))

    def colmajor(self, arr, c, cblock=0, lead=None):
        assert self.mode == "cols"
        pre_shape = () if lead is None else (arr.shape[0],)
        pre_blk = () if lead is None else (1,)
        pre_ix = () if lead is None else (lead,)
        a4 = arr.reshape(pre_shape + (self.nb, GRID_W, GRID_W, arr.shape[-1]))
        return a4, pl.BlockSpec(pre_blk + (1, self.jb, self.ib, c),
                                self._ix(lambda b, i, j: pre_ix + (b, j, i, cblock)))

    def native(self, arr, c, cblock=0, lead=None):
        if self.mode == "cols":
            return self.colmajor(arr, c, cblock, lead)
        return self.raster(arr, c, cblock, lead)

    def mod(self, arr):
        d = arr.shape[-1]
        if not self.per_batch_mod:
            return arr, pl.BlockSpec((1, 1, d), self._ix(lambda *g: (0, 0, 0)))
        if self.mode == "rows":
            tps = self.tps
            return arr, pl.BlockSpec((1, 1, d), self._ix(lambda i: (i // tps, 0, 0)))
        return arr, pl.BlockSpec((1, 1, d), self._ix(lambda b, i, j: (b, 0, 0)))

    def const(self, arr):
        nd = arr.ndim
        return arr, pl.BlockSpec(arr.shape, self._ix(lambda *g: (0,) * nd))

    def out_raster(self, n_rows, c, dtype):
        if self.mode == "rows":
            return (jax.ShapeDtypeStruct((n_rows, c), dtype),
                    pl.BlockSpec((self.tm, c), self._ix(lambda i: (i, 0))))
        return (jax.ShapeDtypeStruct((self.nb, GRID_W, GRID_W, c), dtype),
                pl.BlockSpec((1, self.ib, self.jb, c), self._ix(lambda b, i, j: (b, i, j, 0))))


def _out_proj_call(lay, prologue, pro_inputs, w_out, bias, tail, *, alpha, name):
    d = w_out.shape[1]
    n_rows = lay.nb * lay.seq
    emit_u = "scale" in tail
    n_exp = tail.get("n_exp", 0)
    names, args, specs = [], [], []

    def add(nm, pair):
        names.append(nm)
        args.append(pair[0])
        specs.append(pair[1])

    for k, pair in enumerate(pro_inputs):
        add(f"p{k}", pair)
    add("w", lay.const(w_out))
    if bias is not None:
        add("bias", lay.const(bias.reshape(1, d).astype(F32)))
    if lay.mode == "cols":
        add("perm", lay.const(_perm_mat(lay.ib, lay.jb).T))
    add("h", lay.raster(tail["h"], d))
    add("gate", lay.mod(tail["gate"]))
    add("ln_g", lay.const(tail["ln_g"].reshape(1, d)))
    add("ln_b", lay.const(tail["ln_b"].reshape(1, d)))
    if emit_u:
        add("scale", lay.mod(tail["scale"]))
        add("shift", lay.mod(tail["shift"]))
    if n_exp:
        add("router", lay.const(tail["router"]))
    out_names, out_shapes, out_specs = [], [], []

    def add_out(nm, pair):
        out_names.append(nm)
        out_shapes.append(pair[0])
        out_specs.append(pair[1])

    add_out("h_out", lay.out_raster(n_rows, d, F32))
    if emit_u:
        add_out("u_out", lay.out_raster(n_rows, d, BF16))
    if n_exp:
        add_out("comb_out", lay.out_raster(n_rows, V7X_LANES, F32))
    n_pro = len(pro_inputs)
    n_in = len(names)

    def kernel(*refs):
        r = dict(zip(names + out_names, refs))
        z = prologue(*[refs[k] for k in range(n_pro)])
        z = z.astype(BF16)
        if lay.mode == "cols":
            z = jnp.dot(r["perm"][...], z, preferred_element_type=F32).astype(BF16)
        y = jnp.dot(z, r["w"][...], preferred_element_type=F32)
        if bias is not None:
            y = y + r["bias"][...]
        _finish_rows(r, y, alpha=alpha, emit_u=emit_u, n_exp=n_exp)

    outs = pl.pallas_call(
        kernel, grid=lay.grid, in_specs=specs, out_specs=out_specs, out_shape=out_shapes,
        compiler_params=_cparams(lay.sem()), name=name)(*args)
    outs = [o.reshape(n_rows, o.shape[-1]) for o in outs]
    res = {"h": outs[0]}
    if emit_u:
        res["u"] = outs[1]
    if n_exp:
        res["comb"] = outs[2]
    return res


def _modulate_kernel(h_ref, scale_ref, shift_ref, u_ref):
    u_ref[...] = (h_ref[...] * (1.0 + scale_ref[0]) + shift_ref[0]).astype(u_ref.dtype)


def _modulate(lay, h, scale, shift):
    n_rows, d = h.shape
    pairs = [lay.raster(h, d), lay.mod(scale), lay.mod(shift)]
    out = lay.out_raster(n_rows, d, BF16)
    return pl.pallas_call(
        _modulate_kernel, grid=lay.grid, in_specs=[pr[1] for pr in pairs], out_specs=out[1], out_shape=out[0],
        compiler_params=_cparams(lay.sem()), name="adaln_modulate")(*[pr[0] for pr in pairs])


def _ffn_kernel(*refs, names, n_f, alpha, emit_u):
    r = dict(zip(names, refs))
    f = pl.program_id(1)

    @pl.when(f == 0)
    def _():
        r["acc"][...] = jnp.zeros_like(r["acc"])

    u = r["u"][...]
    gate = jnp.dot(u, r["w1"][...], preferred_element_type=F32)
    up = jnp.dot(u, r["w3"][...], preferred_element_type=F32)
    hid = _silu(gate) * up
    r["acc"][...] += jnp.dot(hid.astype(BF16), r["w2"][...], preferred_element_type=F32)

    @pl.when(f == n_f - 1)
    def _():
        _finish_rows(r, r["acc"][...], alpha=alpha, emit_u=emit_u, n_exp=0)


def _ffn_call(lay, u, w13, w2, tail, *, alpha, name):
    d, f2 = w13.shape
    fdim = f2 // 2
    tf = _tile(fdim, 1536, V7X_LANES)
    n_f = fdim // tf
    n_rows = lay.nb * lay.seq
    emit_u = "scale" in tail
    names, args, specs = [], [], []

    def add(nm, pair):
        names.append(nm)
        args.append(pair[0])
        specs.append(pair[1])

    add("u", lay.raster(u, d))
    add("w1", (w13, pl.BlockSpec((d, tf), lambda i, f: (0, f))))
    add("w3", (w13, pl.BlockSpec((d, tf), lambda i, f: (0, n_f + f))))
    add("w2", (w2, pl.BlockSpec((tf, d), lambda i, f: (f, 0))))
    add("h", lay.raster(tail["h"], d))
    add("gate", lay.mod(tail["gate"]))
    add("ln_g", lay.const(tail["ln_g"].reshape(1, d)))
    add("ln_b", lay.const(tail["ln_b"].reshape(1, d)))
    if emit_u:
        add("scale", lay.mod(tail["scale"]))
        add("shift", lay.mod(tail["shift"]))
    out_names = ["h_out"] + (["u_out"] if emit_u else [])
    outs = [lay.out_raster(n_rows, d, F32)] + ([lay.out_raster(n_rows, d, BF16)] if emit_u else [])
    kernel = functools.partial(_ffn_kernel, names=names + out_names + ["acc"], n_f=n_f, alpha=alpha, emit_u=emit_u)
    res = pl.pallas_call(
        kernel, grid=lay.grid + (n_f,), in_specs=specs,
        out_specs=[o[1] for o in outs], out_shape=[o[0] for o in outs],
        scratch_shapes=[pltpu.VMEM((lay.rows, d), F32)],
        compiler_params=_cparams(lay.sem(("arbitrary",))), name=name)(*args)
    out = {"h": res[0]}
    if emit_u:
        out["u"] = res[1]
    return out


MOE_TOKENS = 1024
MOE_ROW_ALIGN = 32


def _moe_group_rows(tb, n_e):
    p = TOP_K / n_e
    want = tb * p + 2.0 * math.sqrt(tb * p * (1.0 - p))
    return min(tb, MOE_ROW_ALIGN * int(math.ceil(want / MOE_ROW_ALIGN)))


def _moe_kernel(cnt_ref, u_ref, comb_ref, rank_ref, rankt_ref, w1_ref, w3_ref, w2_ref, out_ref, xe_ref, ye_ref, cw_ref,
                *, n_f, tb, rc):
    blk, e, f = pl.program_id(0), pl.program_id(1), pl.program_id(2)
    n_groups = lax.div(cnt_ref[blk, e] + (rc - 1), jnp.int32(rc))
    lane = lax.broadcasted_iota(jnp.int32, (rc, V7X_LANES), 1)

    @pl.when((e == 0) & (f == 0))
    def _():
        out_ref[...] = jnp.zeros_like(out_ref)

    @pl.when(f == 0)
    def _():
        want = rankt_ref[0, pl.ds(e, 1), :]
        comb = comb_ref[...]
        comb_hi = comb.astype(BF16)
        comb_lo = (comb - comb_hi.astype(F32)).astype(BF16)
        slot = lax.broadcasted_iota(jnp.int32, (rc, tb), 0).astype(F32)

        def gather(g, carry):
            rows = pl.ds(pl.multiple_of(g * rc, MOE_ROW_ALIGN), rc)
            onehot = (want == slot + (g * rc).astype(F32)).astype(BF16)
            xe_ref[rows, :] = jnp.dot(onehot, u_ref[...], preferred_element_type=F32).astype(BF16)
            cw_ref[rows, :] = (jnp.dot(onehot, comb_hi, preferred_element_type=F32)
                               + jnp.dot(onehot, comb_lo, preferred_element_type=F32))
            ye_ref[rows, :] = jnp.zeros((rc, ye_ref.shape[1]), F32)
            return carry

        lax.fori_loop(0, n_groups, gather, 0)

    def expert(g, carry):
        rows = pl.ds(pl.multiple_of(g * rc, MOE_ROW_ALIGN), rc)
        x = xe_ref[rows, :]
        hid = _silu(jnp.dot(x, w1_ref[0], preferred_element_type=F32)) * jnp.dot(x, w3_ref[0], preferred_element_type=F32)
        cw = jnp.sum(jnp.where(lane == e, cw_ref[rows, :], 0.0), axis=-1, keepdims=True)
        ye_ref[rows, :] += jnp.dot((hid * cw).astype(BF16), w2_ref[0], preferred_element_type=F32)
        return carry

    lax.fori_loop(0, n_groups, expert, 0)

    @pl.when(f == n_f - 1)
    def _():
        lane_t = lax.broadcasted_iota(jnp.int32, (tb, V7X_LANES), 1)
        want_col = jnp.sum(jnp.where(lane_t == e, rank_ref[...], 0.0), axis=-1, keepdims=True)
        slot_t = lax.broadcasted_iota(jnp.int32, (tb, rc), 1).astype(F32)

        def scatter(g, carry):
            rows = pl.ds(pl.multiple_of(g * rc, MOE_ROW_ALIGN), rc)
            onehot_t = (want_col == slot_t + (g * rc).astype(F32)).astype(BF16)
            out_ref[...] += jnp.dot(onehot_t, ye_ref[rows, :].astype(BF16), preferred_element_type=F32)
            return carry

        lax.fori_loop(0, n_groups, scatter, 0)


def _moe_call(lay, u, w13, w2, comb, tail, *, alpha, name):
    n_e, d, f2 = w13.shape
    fdim = f2 // 2
    tf = _tile(fdim, 1536, V7X_LANES)
    n_f = fdim // tf
    n_rows = u.shape[0]
    tb = _tile(n_rows, MOE_TOKENS, MOE_ROW_ALIGN)
    rc = _moe_group_rows(tb, n_e)
    cap = rc * ((tb + rc - 1) // rc)
    n_blk = n_rows // tb
    lanes = V7X_LANES
    assert n_e == V7X_SUBLANES
    routed = (comb[:, :n_e] > 0.0).astype(jnp.int32).reshape(n_blk, tb, n_e)
    rank = jnp.where(routed > 0, jnp.cumsum(routed, axis=1) - routed, -1).astype(F32)
    cnt = jnp.sum(routed, axis=1)
    rank_col = jnp.pad(rank.reshape(n_rows, n_e), ((0, 0), (0, lanes - n_e)), constant_values=-1.0)
    rank_row = jnp.transpose(rank, (0, 2, 1))
    grid_spec = pltpu.PrefetchScalarGridSpec(
        num_scalar_prefetch=1, grid=(n_blk, n_e, n_f),
        in_specs=[pl.BlockSpec((tb, d), lambda i, e, f, c: (i, 0)),
                  pl.BlockSpec((tb, lanes), lambda i, e, f, c: (i, 0)),
                  pl.BlockSpec((tb, lanes), lambda i, e, f, c: (i, 0)),
                  pl.BlockSpec((1, n_e, tb), lambda i, e, f, c: (i, 0, 0)),
                  pl.BlockSpec((1, d, tf), lambda i, e, f, c: (e, 0, f)),
                  pl.BlockSpec((1, d, tf), lambda i, e, f, c: (e, 0, n_f + f)),
                  pl.BlockSpec((1, tf, d), lambda i, e, f, c: (e, f, 0))],
        out_specs=pl.BlockSpec((tb, d), lambda i, e, f, c: (i, 0)),
        scratch_shapes=[pltpu.VMEM((cap, d), BF16), pltpu.VMEM((cap, d), F32), pltpu.VMEM((cap, lanes), F32)])
    y = pl.pallas_call(
        functools.partial(_moe_kernel, n_f=n_f, tb=tb, rc=rc), grid_spec=grid_spec,
        out_shape=jax.ShapeDtypeStruct((n_rows, d), F32),
        compiler_params=_cparams(("parallel", "arbitrary", "arbitrary")), name=name,
    )(cnt, u, comb, rank_col, rank_row, w13, w13, w2)
    return _tail_call(lay, y, tail, alpha=alpha, name=name + "_tail")


def _tail_call(lay, y, tail, *, alpha, name):
    n_rows, d = y.shape
    emit_u = "scale" in tail
    names, args, specs = [], [], []

    def add(nm, pair):
        names.append(nm)
        args.append(pair[0])
        specs.append(pair[1])

    add("y", lay.raster(y, d))
    add("h", lay.raster(tail["h"], d))
    add("gate", lay.mod(tail["gate"]))
    add("ln_g", lay.const(tail["ln_g"].reshape(1, d)))
    add("ln_b", lay.const(tail["ln_b"].reshape(1, d)))
    if emit_u:
        add("scale", lay.mod(tail["scale"]))
        add("shift", lay.mod(tail["shift"]))
    out_names = ["h_out"] + (["u_out"] if emit_u else [])
    outs = [lay.out_raster(n_rows, d, F32)] + ([lay.out_raster(n_rows, d, BF16)] if emit_u else [])

    def kernel(*refs):
        r = dict(zip(names + out_names, refs))
        _finish_rows(r, r["y"][...], alpha=alpha, emit_u=emit_u, n_exp=0)

    res = pl.pallas_call(
        kernel, grid=lay.grid, in_specs=specs, out_specs=[o[1] for o in outs], out_shape=[o[0] for o in outs],
        compiler_params=_cparams(lay.sem()), name=name)(*args)
    out = {"h": res[0]}
    if emit_u:
        out["u"] = res[1]
    return out


def _shift_rows(cur, prev_row, next_row):
    n = cur.shape[0]
    rows = lax.broadcasted_iota(jnp.int32, cur.shape, 0)
    up = jnp.where(rows == 0, prev_row, pltpu.roll(cur, 1, 0))
    dn = jnp.where(rows == n - 1, next_row, pltpu.roll(cur, n - 1, 0))
    return up, dn


def _halo_rows(i, tps, xp_ref, xn_ref, hb):
    t = i % tps
    prev_row = jnp.where(t == 0, 0.0, xp_ref[hb - 1:hb, :].astype(F32))
    next_row = jnp.where(t == tps - 1, 0.0, xn_ref[0:1, :].astype(F32))
    return prev_row, next_row


def _halo_rows_index(tm, n_rows, hb):
    r = tm // hb
    last = n_rows // hb - 1
    return (lambda i: jnp.maximum(i * r - 1, 0)), (lambda i: jnp.minimum((i + 1) * r, last))


def _conv3_kernel(x_ref, xp_ref, xn_ref, w_ref, b_ref, o_ref, *, tps, act):
    i = pl.program_id(0)
    cur = x_ref[...].astype(F32)
    prev_row, next_row = _halo_rows(i, tps, xp_ref, xn_ref, xp_ref.shape[0])
    up, dn = _shift_rows(cur, prev_row, next_row)
    w = w_ref[...]
    y = w[0:1] * up + w[1:2] * cur + w[2:3] * dn + b_ref[...]
    if act is not None:
        y = act(y)
    o_ref[...] = y.astype(o_ref.dtype).reshape(o_ref.shape)


def _conv3(x, w, b, seq, *, act=None, out_dtype=BF16, time_major_nb=None, name="conv3"):
    n_rows, c = x.shape
    hb = V7X_BF16_ROWS if x.dtype == BF16 else V7X_SUBLANES
    tm = _tile(seq, 512, hb)
    tps = seq // tm
    if b is None:
        b = jnp.zeros((c,), F32)
    if time_major_nb is None:
        ct = _tile(c, 1024, V7X_LANES)
        out_shape = jax.ShapeDtypeStruct((n_rows, c), out_dtype)
        out_spec = pl.BlockSpec((tm, ct), lambda i, cc: (i, cc))
    else:
        nb, d = time_major_nb
        ct = d
        out_shape = jax.ShapeDtypeStruct((c // d, seq, nb * d), out_dtype)
        out_spec = pl.BlockSpec((1, tm, d), lambda i, cc: (cc, i % tps, i // tps))
    prev, nxt = _halo_rows_index(tm, n_rows, hb)
    return pl.pallas_call(
        functools.partial(_conv3_kernel, tps=tps, act=act),
        grid=(n_rows // tm, c // ct),
        in_specs=[pl.BlockSpec((tm, ct), lambda i, cc: (i, cc)),
                  pl.BlockSpec((hb, ct), lambda i, cc: (prev(i), cc)),
                  pl.BlockSpec((hb, ct), lambda i, cc: (nxt(i), cc)),
                  pl.BlockSpec((3, ct), lambda i, cc: (0, cc)),
                  pl.BlockSpec((1, ct), lambda i, cc: (0, cc))],
        out_specs=out_spec, out_shape=out_shape,
        compiler_params=_cparams(("parallel", "parallel")), name=name,
    )(x, x, x, w.astype(F32), b.reshape(1, c).astype(F32))


def _hy_filter_kernel(bands_ref, w1_ref, b1_ref, w2_ref, b2_ref, fr_ref, w3_ref, dec_ref, sum_ref, dif_ref,
                      *, seq, tl, d):
    i = pl.program_id(0)
    pos = (lax.broadcasted_iota(jnp.int32, (tl, V7X_LANES), 0) + i * tl).astype(F32)
    lane = lax.broadcasted_iota(jnp.int32, (tl, V7X_LANES), 1)
    t01 = pos / float(max(seq - 1, 1))
    ang = (2.0 * math.pi / seq) * pos * bands_ref[...]
    feats = jnp.where(lane == 0, t01, jnp.where(lane <= HY_BANDS, jnp.cos(ang), -jnp.sin(ang)))
    fr = fr_ref[...]
    h = jnp.sin(fr[0:1] * (_hdot(feats, w1_ref[...]) + b1_ref[...]))
    h = jnp.sin(fr[1:2] * (_hdot(h, w2_ref[...]) + b2_ref[...]))
    k = _hdot(h, w3_ref[...]) * jnp.exp(-t01[:, 0:1] * jnp.abs(dec_ref[...]))
    not_first = (pos[:, 0:1] > 0.0).astype(F32)
    for o in range(2):
        kf = k[:, (2 * o) * d:(2 * o + 1) * d]
        kb = k[:, (2 * o + 1) * d:(2 * o + 2) * d] * not_first
        sum_ref[:, o * d:(o + 1) * d] = (kf + kb).astype(sum_ref.dtype)
        dif_ref[:, o * d:(o + 1) * d] = (kb - kf).astype(dif_ref.dtype)


def _hy_filters(seq, p, d):
    lanes = V7X_LANES
    fw = p["hy_f_w1"].shape[1]
    bands = jnp.linspace(1e-4, HY_BANDS - 1, HY_BANDS, dtype=F32)
    bands_row = _pad_to(jnp.concatenate([jnp.zeros((1,), F32), bands, bands])[None, :], 1, lanes)
    w1 = _pad_to(_pad_to(p["hy_f_w1"].astype(F32), 0, lanes), 1, lanes)
    b1 = _pad_to(p["hy_f_b1"].astype(F32)[None, :], 1, lanes)
    w2 = _pad_to(_pad_to(p["hy_f_w2"].astype(F32), 0, lanes), 1, lanes)
    b2 = _pad_to(p["hy_f_b2"].astype(F32)[None, :], 1, lanes)
    fr = _pad_to(_pad_to(p["hy_f_freq"].astype(F32), 1, lanes), 0, V7X_SUBLANES)
    w3 = _pad_to(p["hy_f_w3"].astype(F32), 0, lanes)
    dec = p["hy_decay"].astype(F32).reshape(1, 4 * d)
    assert fw <= lanes
    tl = _tile(seq, 256, V7X_BF16_ROWS)
    full = lambda a: pl.BlockSpec(a.shape, lambda i: (0,) * a.ndim)
    ins = [bands_row, w1, b1, w2, b2, fr, w3, dec]
    return pl.pallas_call(
        functools.partial(_hy_filter_kernel, seq=seq, tl=tl, d=d),
        grid=(seq // tl,),
        in_specs=[full(a) for a in ins],
        out_specs=[pl.BlockSpec((tl, 2 * d), lambda i: (i, 0))] * 2,
        out_shape=[jax.ShapeDtypeStruct((seq, 2 * d), BF16)] * 2,
        compiler_params=_cparams(("parallel",)), name="hyena_filters",
    )(*ins)


def _dft_kernel(c_ref, s_ref, ct_ref, st_ref, *, seq, tr):
    i = pl.program_id(0)
    lanes = min(V7X_LANES, seq)
    n_hi = seq // lanes
    row = lax.broadcasted_iota(jnp.int32, (tr, lanes), 0) + i * tr
    col = lax.broadcasted_iota(jnp.int32, (tr, lanes), 1)
    scale = math.pi / (2 * seq)
    mask = 4 * seq - 1
    trig = lambda m: (jnp.cos((m & mask).astype(F32) * scale), jnp.sin((m & mask).astype(F32) * scale))
    lo_c, lo_s = trig((2 * row + 1) * col)
    hi_c, hi_s = trig((2 * row + 1) * ((col * lanes) & mask))
    tlo_c, tlo_s = trig((2 * col + 1) * row)
    thi_c, thi_s = trig(((2 * lanes * col) & mask) * row)
    for h in range(n_hi):
        cols = slice(h * lanes, (h + 1) * lanes)
        ac, as_ = hi_c[:, h:h + 1], hi_s[:, h:h + 1]
        c_ref[:, cols] = (ac * lo_c - as_ * lo_s).astype(BF16)
        s_ref[:, cols] = (as_ * lo_c + ac * lo_s).astype(BF16)
        tc, ts = thi_c[:, h:h + 1], thi_s[:, h:h + 1]
        ct_ref[:, cols] = (tc * tlo_c - ts * tlo_s).astype(BF16)
        st_ref[:, cols] = (ts * tlo_c + tc * tlo_s).astype(BF16)


def _dft_mats(seq):
    assert seq & (seq - 1) == 0, "token count must be a power of two"
    tr = _tile(seq, 256, V7X_BF16_ROWS)
    spec = pl.BlockSpec((tr, seq), lambda i: (i, 0))
    return pl.pallas_call(
        functools.partial(_dft_kernel, seq=seq, tr=tr), grid=(seq // tr,), in_specs=[],
        out_specs=[spec] * 4, out_shape=[jax.ShapeDtypeStruct((seq, seq), BF16)] * 4,
        compiler_params=_cparams(("parallel",)), name="dft_matrices")()


def _hy_fwd_kernel(c_ref, s_ref, v_ref, kr_ref, ki_ref, wr_ref, wi_ref):
    v = v_ref[0]
    cv = jnp.dot(c_ref[...], v, preferred_element_type=F32)
    sv = jnp.dot(s_ref[...], v, preferred_element_type=F32)
    kr, ki = kr_ref[...], ki_ref[...]
    wr_ref[...] = (cv * kr + sv * ki).astype(wr_ref.dtype)
    wi_ref[...] = (cv * ki - sv * kr).astype(wi_ref.dtype)


def _hy_inv_kernel(ct_ref, st_ref, wr_ref, wi_ref, v_ref, g_ref, skip_ref, o_ref, *, seq):
    y = (jnp.dot(ct_ref[...], wr_ref[...], preferred_element_type=F32)
         - jnp.dot(st_ref[...], wi_ref[...], preferred_element_type=F32)) * (1.0 / seq)
    y = y + v_ref[0].astype(F32) * skip_ref[...]
    o_ref[0] = (g_ref[0].astype(F32) * y).astype(o_ref.dtype)


def _hy_long_conv(vsrc, v_idx, gsrc, g_idx, mats, kr, ki, order, skip, d):
    c, s, ct, st = mats
    _, seq, cols = vsrc.shape
    tm = _tile(seq, 512, V7X_BF16_ROWS)
    tn = _tile(d, 512, V7X_LANES)
    cpd = d // tn
    kspec = pl.BlockSpec((tm, tn), lambda i, j: (i, order * cpd + j % cpd))
    wr, wi = pl.pallas_call(
        _hy_fwd_kernel, grid=(seq // tm, cols // tn),
        in_specs=[pl.BlockSpec((tm, seq), lambda i, j: (i, 0)), pl.BlockSpec((tm, seq), lambda i, j: (i, 0)),
                  pl.BlockSpec((1, seq, tn), lambda i, j: (v_idx, 0, j)), kspec, kspec],
        out_specs=[pl.BlockSpec((tm, tn), lambda i, j: (i, j))] * 2,
        out_shape=[jax.ShapeDtypeStruct((seq, cols), BF16)] * 2,
        compiler_params=_cparams(("parallel", "parallel")), name="hyena_dft_fwd",
    )(c, s, vsrc, kr, ki)
    return pl.pallas_call(
        functools.partial(_hy_inv_kernel, seq=seq), grid=(seq // tm, cols // tn),
        in_specs=[pl.BlockSpec((tm, seq), lambda i, j: (i, 0)), pl.BlockSpec((tm, seq), lambda i, j: (i, 0)),
                  pl.BlockSpec((seq, tn), lambda i, j: (0, j)), pl.BlockSpec((seq, tn), lambda i, j: (0, j)),
                  pl.BlockSpec((1, tm, tn), lambda i, j: (v_idx, i, j)),
                  pl.BlockSpec((1, tm, tn), lambda i, j: (g_idx, i, j)),
                  pl.BlockSpec((1, tn), lambda i, j: (0, j % cpd))],
        out_specs=pl.BlockSpec((1, tm, tn), lambda i, j: (0, i, j)),
        out_shape=jax.ShapeDtypeStruct((1, seq, cols), BF16),
        compiler_params=_cparams(("parallel", "parallel")), name="hyena_dft_inv",
    )(ct, st, wr, wi, vsrc, gsrc, skip.reshape(1, d).astype(F32))


def _hyena_mixer(u, nb, seq, p, w):
    d = u.shape[1]
    proj = _matmul(u, w["hy_w_in"], p["hy_b_in"], out_dtype=BF16, name="hyena_in_proj")
    planes = _conv3(proj, p["hy_conv_w"], p["hy_conv_b"], seq, time_major_nb=(nb, d), name="hyena_short_conv")
    ksum, kdif = _hy_filters(seq, p, d)
    mats = _dft_mats(seq)
    kr = _matmul(mats[0], ksum, out_dtype=F32, tn=512, name="hyena_filter_spec_re")
    ki = _matmul(mats[1], kdif, out_dtype=F32, tn=512, name="hyena_filter_spec_im")
    z1 = _hy_long_conv(planes, 0, planes, 1, mats, kr, ki, 0, p["hy_skip"][0], d)
    return _hy_long_conv(z1, 0, planes, 2, mats, kr, ki, 1, p["hy_skip"][1], d)


def _hyena_out(lay_rows, z, tail, p, w, nb, seq, *, alpha, name):
    d = w["hy_w_out"].shape[0]
    tm = lay_rows.tm
    tps = lay_rows.tps
    spec = pl.BlockSpec((1, tm, d), lay_rows._ix(lambda i: (0, i % tps, i // tps)))
    return _out_proj_call(lay_rows, lambda z_ref: z_ref[0], [(z, spec)], w["hy_w_out"], p["hy_b_out"], tail,
                          alpha=alpha, name=name)


def _rw_proj_kernel(*refs, names, tps, hd):
    r = dict(zip(names, refs))
    i = pl.program_id(0)
    cur = r["u"][...].astype(F32)
    prev_row, next_row = _halo_rows(i, tps, r["up"], r["un"], V7X_BF16_ROWS)
    up, dn = _shift_rows(cur, prev_row, next_row)
    xx = 0.5 * (up + dn) - cur
    mu = r["mu"][...]
    mix = lambda j: (cur + xx * mu[j:j + 1]).astype(BF16)
    xr, xw, xk, xv, xa, xg = [mix(j) for j in range(6)]
    qq = r["qq"][...]
    rr = jnp.dot(xr, r["w_rkv"][0], preferred_element_type=F32)
    kk0 = jnp.dot(xk, r["w_rkv"][1], preferred_element_type=F32)
    vv = jnp.dot(xv, r["w_rkv"][2], preferred_element_type=F32)
    gg = _bdot(jax.nn.sigmoid(jnp.dot(xg, r["g1"][...], preferred_element_type=F32)), r["g2"][...])
    kx = kk0 * r["k_k"][...]
    kkn = kx * lax.rsqrt(_head_sum(kx * kx, qq) + 1e-6)
    r["r"][...] = rr.astype(BF16)
    r["v"][...] = vv.astype(BF16)
    r["g"][...] = gg.astype(BF16)
    r["na"][...] = (-kkn).astype(BF16)
    kd_sum = jnp.zeros_like(kk0)
    for dd in range(2):
        hw = jnp.tanh(jnp.dot(xw, r["lora_w1"][dd], preferred_element_type=F32))
        wpre = r["bias_w"][dd] + _bdot(hw, r["lora_w2"][dd])
        logw = -_softplus(-wpre) - 0.5
        r["lw"][dd] = -jnp.exp(logw)
        a = jax.nn.sigmoid(
            r["bias_a"][dd] + _bdot(jnp.dot(xa, r["lora_a1"][dd], preferred_element_type=F32), r["lora_a2"][dd]))
        kd = kk0 * (1.0 + (a - 1.0) * r["k_a"][...])
        r["kd"][dd] = kd.astype(BF16)
        r["b"][dd] = (kkn * a).astype(BF16)
        kd_sum = kd_sum + kd
    r["bonus"][...] = (_head_sum(rr * kd_sum * r["r_k"][...], qq) * vv).astype(BF16)


_RW_OUTS = ["r", "v", "g", "na", "bonus", "lw", "kd", "b"]
_RW_STACKED = ("lw", "kd", "b")


def _rw_project(u, seq, p, w):
    n_rows, d = u.shape
    hb = V7X_BF16_ROWS
    tm = _tile(seq, 256, hb)
    tps = seq // tm
    lanes = V7X_LANES
    names, args, specs = [], [], []

    def add(nm, arr, spec=None):
        names.append(nm)
        args.append(arr)
        nd = arr.ndim
        specs.append(spec if spec is not None else pl.BlockSpec(arr.shape, lambda i: (0,) * nd))

    prev, nxt = _halo_rows_index(tm, n_rows, hb)
    add("u", u, pl.BlockSpec((tm, d), lambda i: (i, 0)))
    add("up", u, pl.BlockSpec((hb, d), lambda i: (prev(i), 0)))
    add("un", u, pl.BlockSpec((hb, d), lambda i: (nxt(i), 0)))
    add("mu", _pad_to(p["rw_mu"].astype(F32), 0, V7X_SUBLANES))
    add("w_rkv", w["rw_w_rkv"])
    add("g1", w["rw_g1"])
    add("g2", w["rw_g2"])
    add("bias_w", p["rw_w0"].astype(F32).reshape(2, 1, d))
    add("lora_w1", w["rw_w1"])
    add("lora_w2", w["rw_w2"])
    add("bias_a", p["rw_a0"].astype(F32).reshape(2, 1, d))
    add("lora_a1", w["rw_a1"])
    add("lora_a2", w["rw_a2"])
    add("k_k", p["rw_k_k"].astype(F32).reshape(1, d))
    add("k_a", p["rw_k_a"].astype(F32).reshape(1, d))
    add("r_k", p["rw_r_k"].astype(F32).reshape(1, d))
    add("qq", _expand_mat(V7X_LANES, d // RW_HEAD, RW_HEAD))
    out_dt = {nm: (F32 if nm == "lw" else BF16) for nm in _RW_OUTS}
    row_spec = pl.BlockSpec((tm, d), lambda i: (i, 0))
    dir_spec = pl.BlockSpec((2, tm, d), lambda i: (0, i, 0))
    outs = pl.pallas_call(
        functools.partial(_rw_proj_kernel, names=names + _RW_OUTS, tps=tps, hd=RW_HEAD),
        grid=(n_rows // tm,), in_specs=specs,
        out_specs=[dir_spec if nm in _RW_STACKED else row_spec for nm in _RW_OUTS],
        out_shape=[jax.ShapeDtypeStruct(((2,) if nm in _RW_STACKED else ()) + (n_rows, d), out_dt[nm])
                   for nm in _RW_OUTS],
        compiler_params=_cparams(("parallel",)), name="rwkv7_projections")(*args)
    return dict(zip(_RW_OUTS, outs))


TRI_BASE = 8


def _unit_tri_inverse(nmat, cl):
    ii = lax.broadcasted_iota(jnp.int32, (cl, cl), 0)
    jj = lax.broadcasted_iota(jnp.int32, (cl, cl), 1)
    zero = jnp.float32(0.0)
    ident = (ii == jj).astype(F32)
    block_gap = lambda k: lax.shift_right_logical(ii ^ jj, k)
    k0 = TRI_BASE.bit_length() - 1
    n0 = [jnp.where(block_gap(k0) == 0, n, zero) for n in nmat]
    minv = [ident + n for n in n0]
    pw = n0
    for _ in range(k0 - 1):
        pw = [_bdot(x, x) for x in pw]
        minv = [m + _bdot(m, x) for m, x in zip(minv, pw)]
    k = k0
    while (1 << k) < cl:
        off = block_gap(k) == 1
        minv = [m + _bdot(_bdot(m, jnp.where(off, n, zero)), m) for m, n in zip(minv, nmat)]
        k += 1
    return minv


def _rw_scan_kernel(r_ref, lw_ref, k_ref, v_ref, a_ref, b_ref, s0_ref, y_ref, sfin_ref, s_ref, *, n_chunks, hd):
    dirn, c = pl.program_id(1), pl.program_id(2)
    cl, d = r_ref.shape[1], r_ref.shape[2]
    pw_ = 2 * hd
    n_pairs = d // pw_

    @pl.when(c == 0)
    def _():
        s_ref[...] = s0_ref[0, 0]

    incl, strict = _time_masks(dirn, cl)
    lw = lw_ref[0, 0]
    cum = _hdot(incl.astype(F32), lw)
    p_tot = jnp.exp(jnp.sum(lw, axis=0, keepdims=True))
    p_inv = jnp.exp(-cum)
    at = a_ref[0] * jnp.exp(cum - lw)
    rt = r_ref[0] * jnp.exp(cum)
    bt = (b_ref[0, 0] * p_inv).astype(BF16)
    kt = (k_ref[0, 0] * p_inv).astype(BF16)
    v = v_ref[0]
    hd_shift = hd.bit_length() - 1
    head_of = lambda shape, axis: lax.shift_right_logical(lax.broadcasted_iota(jnp.int32, shape, axis), hd_shift)
    lane = head_of((cl, pw_), 1)
    lane2 = head_of((2 * cl, pw_), 1)
    blk = head_of((pw_, pw_), 0) == head_of((pw_, pw_), 1)
    zero = jnp.float32(0.0)
    pairs = range(n_pairs)
    halves = [(pr, hf) for pr in pairs for hf in range(2)]
    sl = lambda x, pr: x[:, pr * pw_:(pr + 1) * pw_]
    s_old = [s_ref[pr] for pr in pairs]
    s_bf = [s.astype(BF16) for s in s_old]
    xs = [jnp.concatenate([sl(at, pr), sl(rt, pr)], axis=0) for pr in pairs]
    xs0 = [_bdot_nt(xs[pr], s_bf[pr]) for pr in pairs]
    xm = [jnp.where(lane2 == hf, xs[pr], zero).astype(BF16) for pr, hf in halves]
    gb = [lax.dot_general(xm[i], sl(bt, pr), NT_DIMS, preferred_element_type=F32) for i, (pr, hf) in enumerate(halves)]
    gk = [lax.dot_general(xm[i], sl(kt, pr), NT_DIMS, preferred_element_type=F32) for i, (pr, hf) in enumerate(halves)]
    minv = _unit_tri_inverse([jnp.where(strict, g[:cl], zero) for g in gb], cl)
    vm = [jnp.where(lane == hf, sl(v, pr), zero).astype(BF16) for pr, hf in halves]
    rhs = [jnp.where(lane == hf, xs0[pr][:cl], zero)
           + jnp.dot(jnp.where(strict, gk[i][:cl], zero).astype(BF16), vm[i], preferred_element_type=F32)
           for i, (pr, hf) in enumerate(halves)]
    u = [_bdot(minv[i], rhs[i]) for i in range(len(halves))]
    u_b = [x.astype(BF16) for x in u]
    yh = [jnp.dot(jnp.where(incl, gb[i][cl:], zero).astype(BF16), u_b[i], preferred_element_type=F32)
          + jnp.dot(jnp.where(incl, gk[i][cl:], zero).astype(BF16), vm[i], preferred_element_type=F32)
          for i in range(len(halves))]
    for pr in pairs:
        y_ref[0, 0, :, pr * pw_:(pr + 1) * pw_] = (xs0[pr][cl:] + yh[2 * pr] + yh[2 * pr + 1]).astype(y_ref.dtype)
    for pr in pairs:
        uv = jnp.concatenate([u[2 * pr] + u[2 * pr + 1], sl(v, pr)], axis=0).astype(BF16)
        bk = jnp.concatenate([sl(bt, pr), sl(kt, pr)], axis=0)
        delta = lax.dot_general(uv, bk, TN_DIMS, preferred_element_type=F32)
        s_ref[pr] = (s_old[pr] + jnp.where(blk, delta, zero)) * sl(p_tot, pr)

    @pl.when(c == n_chunks - 1)
    def _():
        sfin_ref[0, 0] = s_ref[...]


def _rw_scan(q, s0, nb, seq, d):
    hd = RW_HEAD
    pw_ = 2 * hd
    assert pw_ == V7X_LANES and d % pw_ == 0
    n_pairs = d // pw_
    n_chunks = seq // CHUNK
    cidx = lambda dd, c: c + dd * (n_chunks - 1 - 2 * c)
    v3 = lambda a: a.reshape(nb, seq, d)
    v4 = lambda a: a.reshape(2, nb, seq, d)
    tok = pl.BlockSpec((1, CHUNK, d), lambda b, dd, c: (b, cidx(dd, c), 0))
    tok_d = pl.BlockSpec((1, 1, CHUNK, d), lambda b, dd, c: (dd, b, cidx(dd, c), 0))
    st_spec = pl.BlockSpec((1, 1, n_pairs, pw_, pw_), lambda b, dd, c: (b, dd, 0, 0, 0))
    y, sfin = pl.pallas_call(
        functools.partial(_rw_scan_kernel, n_chunks=n_chunks, hd=hd),
        grid=(nb, 2, n_chunks),
        in_specs=[tok, tok_d, tok_d, tok, tok, tok_d, st_spec],
        out_specs=[tok_d, st_spec],
        out_shape=[jax.ShapeDtypeStruct((2, nb, seq, d), BF16),
                   jax.ShapeDtypeStruct((nb, 2, n_pairs, pw_, pw_), F32)],
        scratch_shapes=[pltpu.VMEM((n_pairs, pw_, pw_), F32)],
        compiler_params=_cparams(("parallel", "arbitrary", "arbitrary")), name="rwkv7_scan",
    )(v3(q["r"]), v4(q["lw"]), v4(q["kd"]), v3(q["v"]), v3(q["na"]), v4(q["b"]), s0)
    return y, sfin


def _rw_out(lay, y, q, tail, p, w, *, alpha, name):
    d = w["rw_w_out"].shape[0]
    n_rows = lay.nb * lay.seq
    y2 = y.reshape(2, n_rows, d)
    qq = _expand_mat(V7X_LANES, d // RW_HEAD, RW_HEAD)
    lnx_g = p["rw_lnx_g"].astype(F32).reshape(1, d)
    lnx_b = p["rw_lnx_b"].astype(F32).reshape(1, d)
    yspec = pl.BlockSpec((2, lay.tm, d), lay._ix(lambda i: (0, i, 0)))

    def prologue(y_ref, bonus_ref, g_ref, qq_ref, lg_ref, lb_ref):
        yy = y_ref[0].astype(F32) + y_ref[1].astype(F32)
        inv = 1.0 / RW_HEAD
        mean = _head_sum(yy, qq_ref[...]) * inv
        yc = yy - mean
        var = _head_sum(yc * yc, qq_ref[...]) * inv
        yn = yc * lax.rsqrt(var + RW_GN_EPS) * lg_ref[...] + lb_ref[...]
        return (yn + bonus_ref[...].astype(F32)) * g_ref[...].astype(F32)

    pro = [(y2, yspec), lay.raster(q["bonus"], d), lay.raster(q["g"], d), lay.const(qq), lay.const(lnx_g),
           lay.const(lnx_b)]
    return _out_proj_call(lay, prologue, pro, w["rw_w_out"], None, tail, alpha=alpha, name=name)


def _ssd_scan_kernel(x_ref, b_ref, c_ref, dt_ref, dtb_ref, a_ref, s0_ref, xp_ref, eye_ref, y_ref, sfin_ref, s_ref,
                     *, n_chunks, ng, nr, hp, ns):
    dirn, c = pl.program_id(1), pl.program_id(2)
    cl = x_ref.shape[1]

    @pl.when(c == 0)
    def _():
        s_ref[...] = s0_ref[0, 0]

    incl, _ = _time_masks(dirn, cl)
    tri = incl.astype(F32)
    eye = eye_ref[...]
    xp = xp_ref[...]
    dt = _softplus(dt_ref[0] + dtb_ref[0])
    dta = dt * a_ref[0]
    cum = _hdot(tri, dta)
    tot = jnp.sum(dta, axis=0, keepdims=True)
    cum_t = _transpose_via_eye(cum, eye)
    dt_t = _transpose_via_eye(dt, eye)
    e_in = _split_dot(jnp.exp(cum), xp)
    e_end = _split_dot(jnp.exp(tot - cum) * dt, xp)
    e_tot = _split_dot(jnp.broadcast_to(jnp.exp(tot), (V7X_SUBLANES, tot.shape[1])), xp)[0:1]
    x = x_ref[0]
    xw = (x.astype(F32) * e_end).astype(BF16)
    gw = nr * hp
    pair = 2 * hp
    lane = lax.broadcasted_iota(jnp.int32, (cl, pair), 1)
    neg = jnp.float32(-jnp.inf)
    s_all = [s_ref[g] for g in range(ng)]
    y_parts, s_new = [], []
    for g in range(ng):
        bm = b_ref[0, :, g * ns:(g + 1) * ns]
        cm = c_ref[0, :, g * ns:(g + 1) * ns]
        cb = lax.dot_general(cm, bm, NT_DIMS, preferred_element_type=F32)
        s_g = s_all[g]
        y_inter = jnp.dot(cm, s_g.astype(BF16), preferred_element_type=F32) * e_in[:, g * gw:(g + 1) * gw]
        for rp in range(nr // 2):
            y_pair = jnp.zeros((cl, pair), F32)
            xpair = x[:, g * gw + rp * pair:g * gw + (rp + 1) * pair]
            for half in range(2):
                h = g * nr + rp * 2 + half
                dec = jnp.exp(jnp.where(incl, cum[:, h:h + 1] - cum_t[h:h + 1, :], neg))
                sc = (cb * dec * dt_t[h:h + 1, :]).astype(BF16)
                xh = jnp.where((lane >= half * hp) & (lane < (half + 1) * hp), xpair, jnp.zeros_like(xpair))
                y_pair = y_pair + jnp.dot(sc, xh, preferred_element_type=F32)
            y_parts.append((g * gw + rp * pair, y_pair + y_inter[:, rp * pair:(rp + 1) * pair]))
        s_new.append(s_g * e_tot[:, g * gw:(g + 1) * gw] + lax.dot_general(
            bm, xw[:, g * gw:(g + 1) * gw], TN_DIMS, preferred_element_type=F32))
    for lo, val in y_parts:
        y_ref[0, 0, :, lo:lo + pair] = val.astype(y_ref.dtype)
    for g in range(ng):
        s_ref[g] = s_new[g]

    @pl.when(c == n_chunks - 1)
    def _():
        sfin_ref[0, 0] = s_ref[...]


def _ssd_scan(xbc, dt_raw, s0, nb, seq, p):
    ng, ns, hp = SSD_GROUPS, SSD_STATE, SSD_HEAD
    inner = xbc.shape[1] - 2 * ng * ns
    nh = inner // hp
    nr = nh // ng
    lanes = V7X_LANES
    assert ns == lanes and nh <= lanes and nr % 2 == 0
    n_chunks = seq // CHUNK
    cidx = lambda dd, c: c + dd * (n_chunks - 1 - 2 * c)
    xbc3 = xbc.reshape(nb, seq, xbc.shape[1])
    dt3 = dt_raw.reshape(nb, seq, 2 * lanes)
    a = -jnp.exp(p["ssd_A_log"].astype(F32))
    a_pad = _pad_to(a, 1, lanes).reshape(2, 1, lanes)
    dtb = _pad_to(p["ssd_dt_bias"].astype(F32), 1, lanes).reshape(2, 1, lanes)
    xp = _expand_mat(lanes, nh, hp)
    eye = jnp.eye(lanes, dtype=F32)
    bw = ng * ns
    y, sfin = pl.pallas_call(
        functools.partial(_ssd_scan_kernel, n_chunks=n_chunks, ng=ng, nr=nr, hp=hp, ns=ns),
        grid=(nb, 2, n_chunks),
        in_specs=[pl.BlockSpec((1, CHUNK, inner), lambda b, dd, c: (b, cidx(dd, c), 0)),
                  pl.BlockSpec((1, CHUNK, bw), lambda b, dd, c: (b, cidx(dd, c), inner // bw)),
                  pl.BlockSpec((1, CHUNK, bw), lambda b, dd, c: (b, cidx(dd, c), inner // bw + 1)),
                  pl.BlockSpec((1, CHUNK, lanes), lambda b, dd, c: (b, cidx(dd, c), dd)),
                  pl.BlockSpec((1, 1, lanes), lambda b, dd, c: (dd, 0, 0)),
                  pl.BlockSpec((1, 1, lanes), lambda b, dd, c: (dd, 0, 0)),
                  pl.BlockSpec((1, 1, ng, ns, nr * hp), lambda b, dd, c: (b, dd, 0, 0, 0)),
                  pl.BlockSpec(xp.shape, lambda b, dd, c: (0, 0)),
                  pl.BlockSpec(eye.shape, lambda b, dd, c: (0, 0))],
        out_specs=[pl.BlockSpec((1, 1, CHUNK, inner), lambda b, dd, c: (dd, b, cidx(dd, c), 0)),
                   pl.BlockSpec((1, 1, ng, ns, nr * hp), lambda b, dd, c: (b, dd, 0, 0, 0))],
        out_shape=[jax.ShapeDtypeStruct((2, nb, seq, inner), BF16),
                   jax.ShapeDtypeStruct((nb, 2, ng, ns, nr * hp), F32)],
        scratch_shapes=[pltpu.VMEM((ng, ns, nr * hp), F32)],
        compiler_params=_cparams(("parallel", "arbitrary", "arbitrary")), name="ssd_scan",
    )(xbc3, xbc3, xbc3, dt3, dtb, a_pad, s0, xp, eye)
    return y.reshape(2, nb * seq, inner), sfin


def _ssd_project(u, nb, seq, p, w, colmajor):
    mm = (lambda x, wt, **kw: _matmul_r2c(x, wt, None, nb, **kw)) if colmajor else (
        lambda x, wt, **kw: _matmul(x, wt, None, **kw))
    z = mm(u, w["ssd_w_z"], out_dtype=BF16, name="ssd_in_proj_z")
    xbc_raw = mm(u, w["ssd_w_xbc"], out_dtype=BF16, name="ssd_in_proj_xbc")
    dt_raw = mm(u, w["ssd_w_dt"], out_dtype=F32, name="ssd_in_proj_dt")
    xbc = _conv3(xbc_raw, p["ssd_conv_w"], p["ssd_conv_b"], seq, act=_silu, name="ssd_short_conv")
    return z, xbc, dt_raw


def _ssd_out(lay, y, z, xbc, tail, p, w, *, alpha, name):
    inner = w["ssd_w_out"].shape[0]
    ng = SSD_GROUPS
    gwid = inner // ng
    d_row = jnp.repeat(p["ssd_d_skip"].astype(F32), SSD_HEAD).reshape(1, inner)
    ng_row = p["ssd_norm_g"].astype(F32).reshape(1, inner)

    def prologue(y0_ref, y1_ref, xs_ref, z_ref, d_ref, g_ref):
        flat = lambda ref: ref[...].reshape(-1, ref.shape[-1]).astype(F32)
        yy = flat(xs_ref) * d_ref[...] + flat(y0_ref) + flat(y1_ref)
        yy = yy * _silu(flat(z_ref))
        parts = []
        for g in range(ng):
            yg = yy[:, g * gwid:(g + 1) * gwid]
            ms = jnp.mean(yg * yg, axis=-1, keepdims=True)
            parts.append(yg * lax.rsqrt(ms + 1e-6))
        return jnp.concatenate(parts, axis=1) * g_ref[...]

    pro = [lay.native(y, inner, lead=0), lay.native(y, inner, lead=1), lay.native(xbc, inner),
           lay.native(z, inner), lay.const(d_row), lay.const(ng_row)]
    return _out_proj_call(lay, prologue, pro, w["ssd_w_out"], None, tail, alpha=alpha, name=name)


def _gdn_scan_kernel(q_ref, k_ref, v_ref, a_ref, bta_ref, dtb_ref, al_ref, s0_ref, eye_ref, o_ref, sfin_ref, s_ref,
                     *, n_chunks, hk, hv, dh):
    dirn, c = pl.program_id(1), pl.program_id(2)
    cl = q_ref.shape[1]

    @pl.when(c == 0)
    def _():
        s_ref[...] = s0_ref[0, 0]

    incl, strict = _time_masks(dirn, cl)
    tri = incl.astype(F32)
    eye = eye_ref[...]
    neg = jnp.float32(-jnp.inf)
    g = al_ref[0] * _softplus(a_ref[0] + dtb_ref[0])
    beta = jax.nn.sigmoid(bta_ref[0])
    gn = _hdot(tri, g)
    gtot = jnp.sum(g, axis=0, keepdims=True)
    gn_t = _transpose_via_eye(gn, eye)
    e_in = jnp.exp(gn)
    e_end = jnp.exp(gtot - gn)
    e_tot = jnp.exp(gtot)
    rep = hv // hk
    heads = range(hv)
    s_old = [s_ref[h] for h in heads]
    s_bf = [s.astype(BF16) for s in s_old]
    qn, kn, kk, qk = [], [], [], []
    for hq in range(hk):
        qh = q_ref[0, :, hq * dh:(hq + 1) * dh].astype(F32)
        kh = k_ref[0, :, hq * dh:(hq + 1) * dh].astype(F32)
        qn.append(qh * lax.rsqrt(jnp.sum(qh * qh, axis=-1, keepdims=True) + 1e-6) * (dh ** -0.5))
        kn.append(kh * lax.rsqrt(jnp.sum(kh * kh, axis=-1, keepdims=True) + 1e-6))
        gram = _bdot_nt(jnp.concatenate([kn[hq], qn[hq]], axis=0), kn[hq])
        kk.append(gram[:cl])
        qk.append(gram[cl:])
    bh = [beta[:, h:h + 1] for h in heads]
    diff = [gn[:, h:h + 1] - gn_t[h:h + 1, :] for h in heads]
    nmat = [-(kk[h // rep] * bh[h] * jnp.exp(jnp.where(strict, diff[h], neg))) for h in heads]
    attn = [(qk[h // rep] * jnp.exp(jnp.where(incl, diff[h], neg))).astype(BF16) for h in heads]
    minv = _unit_tri_inverse(nmat, cl)
    rhs =[jnp.concatenate([v_ref[0, :, h * dh:(h + 1) * dh].astype(F32) * bh[h],
                            kn[h // rep] * (bh[h] * e_in[:, h:h + 1])], axis=1) for h in heads]
    sol = [_bdot(minv[h], rhs[h]) for h in heads]
    u_b = [(sol[h][:, :dh] - _bdot(sol[h][:, dh:], s_bf[h])).astype(BF16) for h in heads]
    out = [_bdot(qn[h // rep] * e_in[:, h:h + 1], s_bf[h]) + jnp.dot(attn[h], u_b[h], preferred_element_type=F32)
           for h in heads]
    s_new = [s_old[h] * e_tot[:, h:h + 1]
             + lax.dot_general((kn[h // rep] * e_end[:, h:h + 1]).astype(BF16), u_b[h], TN_DIMS,
                               preferred_element_type=F32) for h in heads]
    for h in heads:
        o_ref[0, 0, :, h * dh:(h + 1) * dh] = out[h].astype(o_ref.dtype)
    for h in heads:
        s_ref[h] = s_new[h]

    @pl.when(c == n_chunks - 1)
    def _():
        sfin_ref[0, 0] = s_ref[...]


def _gdn_scan(qkv, ab, s0, nb, seq, p):
    dh = GDN_HEAD
    lanes = V7X_LANES
    hv = p["gdn_dt_bias"].shape[1]
    vw = hv * dh
    qk = (qkv.shape[1] - vw) // 2
    hk = qk // dh
    assert dh == lanes and hv <= lanes
    n_chunks = seq // CHUNK
    cidx = lambda dd, c: c + dd * (n_chunks - 1 - 2 * c)
    qkv3 = qkv.reshape(nb, seq, qkv.shape[1])
    ab3 = ab.reshape(nb, seq, 4 * lanes)
    dtb = _pad_to(p["gdn_dt_bias"].astype(F32), 1, lanes).reshape(2, 1, lanes)
    al = _pad_to(-jnp.exp(p["gdn_A_log"].astype(F32)), 1, lanes).reshape(2, 1, lanes)
    eye = jnp.eye(lanes, dtype=F32)
    o, sfin = pl.pallas_call(
        functools.partial(_gdn_scan_kernel, n_chunks=n_chunks, hk=hk, hv=hv, dh=dh),
        grid=(nb, 2, n_chunks),
        in_specs=[pl.BlockSpec((1, CHUNK, qk), lambda b, dd, c: (b, cidx(dd, c), 0)),
                  pl.BlockSpec((1, CHUNK, qk), lambda b, dd, c: (b, cidx(dd, c), 1)),
                  pl.BlockSpec((1, CHUNK, vw), lambda b, dd, c: (b, cidx(dd, c), 2 * qk // vw)),
                  pl.BlockSpec((1, CHUNK, lanes), lambda b, dd, c: (b, cidx(dd, c), 2 * dd)),
                  pl.BlockSpec((1, CHUNK, lanes), lambda b, dd, c: (b, cidx(dd, c), 2 * dd + 1)),
                  pl.BlockSpec((1, 1, lanes), lambda b, dd, c: (dd, 0, 0)),
                  pl.BlockSpec((1, 1, lanes), lambda b, dd, c: (dd, 0, 0)),
                  pl.BlockSpec((1, 1, hv, dh, dh), lambda b, dd, c: (b, dd, 0, 0, 0)),
                  pl.BlockSpec(eye.shape, lambda b, dd, c: (0, 0))],
        out_specs=[pl.BlockSpec((1, 1, CHUNK, vw), lambda b, dd, c: (dd, b, cidx(dd, c), 0)),
                   pl.BlockSpec((1, 1, hv, dh, dh), lambda b, dd, c: (b, dd, 0, 0, 0))],
        out_shape=[jax.ShapeDtypeStruct((2, nb, seq, vw), BF16),
                   jax.ShapeDtypeStruct((nb, 2, hv, dh, dh), F32)],
        scratch_shapes=[pltpu.VMEM((hv, dh, dh), F32)],
        compiler_params=_cparams(("parallel", "arbitrary", "arbitrary")), name="gdn_scan",
    )(qkv3, qkv3, qkv3, ab3, ab3, dtb, al, s0, eye)
    return o.reshape(2, nb * seq, vw), sfin


def _gdn_project(u, nb, seq, p, w, colmajor):
    mm = (lambda x, wt, **kw: _matmul_r2c(x, wt, None, nb, **kw)) if colmajor else (
        lambda x, wt, **kw: _matmul(x, wt, None, **kw))
    qkv_raw = mm(u, w["gdn_w_qkv"], out_dtype=BF16, name="gdn_in_proj_qkv")
    z = mm(u, w["gdn_w_z"], out_dtype=BF16, name="gdn_in_proj_z")
    ab = mm(u, w["gdn_w_ab"], out_dtype=F32, name="gdn_in_proj_ab")
    qkv = _conv3(qkv_raw, p["gdn_conv_w"], None, seq, act=_silu, name="gdn_short_conv")
    return qkv, z, ab


def _gdn_out(lay, o, z, tail, p, w, *, alpha, name):
    vw = w["gdn_w_out"].shape[0]
    dh = GDN_HEAD
    hv = vw // dh
    g_row = jnp.tile(p["gdn_norm_g"].astype(F32), hv).reshape(1, vw)

    def prologue(o0_ref, o1_ref, z_ref, g_ref):
        flat = lambda ref: ref[...].reshape(-1, ref.shape[-1]).astype(F32)
        oo = flat(o0_ref) + flat(o1_ref)
        parts = []
        for h in range(hv):
            oh = oo[:, h * dh:(h + 1) * dh]
            ms = jnp.mean(oh * oh, axis=-1, keepdims=True)
            parts.append(oh * lax.rsqrt(ms + 1e-6))
        return jnp.concatenate(parts, axis=1) * g_ref[...] * _silu(flat(z_ref))

    pro = [lay.native(o, vw, lead=0), lay.native(o, vw, lead=1), lay.native(z, vw), lay.const(g_row)]
    return _out_proj_call(lay, prologue, pro, w["gdn_w_out"], None, tail, alpha=alpha, name=name)


def _prep_weights(p):
    lanes = V7X_LANES
    bf = lambda a: a.astype(BF16)
    d = p["hy_w_out"].shape[0]
    w = {k: bf(p[k]) for k in ("hy_w_in", "hy_w_out", "rw_w_rkv", "rw_w_out", "ssd_w_out", "gdn_w_out",
                               "ffn_w13", "ffn_w2", "moe_w13", "moe_w2", "mod_w")}
    lora = lambda a, ax: bf(_pad_to(a, ax, lanes * ((a.shape[ax] + lanes - 1) // lanes)))
    w["rw_w1"], w["rw_w2"] = lora(p["rw_w1"], 2), lora(p["rw_w2"], 1)
    w["rw_a1"], w["rw_a2"] = lora(p["rw_a1"], 2), lora(p["rw_a2"], 1)
    w["rw_g1"], w["rw_g2"] = lora(p["rw_g1"], 1), lora(p["rw_g2"], 0)
    inner = p["ssd_w_out"].shape[0]
    nbc = 2 * SSD_GROUPS * SSD_STATE
    nh = inner // SSD_HEAD
    ws = p["ssd_w_in"]
    w["ssd_w_z"] = bf(ws[:, :inner])
    w["ssd_w_xbc"] = bf(ws[:, inner:2 * inner + nbc])
    wdt = ws[:, 2 * inner + nbc:].reshape(d, 2, nh)
    w["ssd_w_dt"] = bf(_pad_to(wdt, 2, lanes).reshape(d, 2 * lanes))
    vw = p["gdn_w_out"].shape[0]
    hv = vw // GDN_HEAD
    qkvw = p["gdn_conv_w"].shape[1]
    wg = p["gdn_w_in"]
    w["gdn_w_qkv"] = bf(wg[:, :qkvw])
    w["gdn_w_z"] = bf(wg[:, qkvw:qkvw + vw])
    wab = wg[:, qkvw + vw:].reshape(d, 2, 2, hv)
    wab = jnp.transpose(wab, (0, 2, 1, 3))
    w["gdn_w_ab"] = bf(_pad_to(wab, 3, lanes).reshape(d, 4 * lanes))
    w["moe_router"] = _pad_to(p["moe_router"].astype(F32), 2, lanes)
    return w


_PARAM_NAMES = (
    "mod_w mod_b ln_g ln_b hy_w_in hy_b_in hy_conv_w hy_conv_b hy_f_w1 hy_f_b1 hy_f_w2 hy_f_b2 hy_f_freq "
    "hy_f_w3 hy_decay hy_skip hy_w_out hy_b_out rw_mu rw_w_rkv rw_w0 rw_w1 rw_w2 rw_a0 rw_a1 rw_a2 rw_g1 rw_g2 "
    "rw_k_k rw_k_a rw_r_k rw_lnx_g rw_lnx_b rw_w_out ssd_w_in ssd_conv_w ssd_conv_b ssd_dt_bias ssd_A_log "
    "ssd_d_skip ssd_norm_g ssd_w_out gdn_w_in gdn_conv_w gdn_dt_bias gdn_A_log gdn_norm_g gdn_w_out ffn_w13 "
    "ffn_w2 moe_router moe_w13 moe_w2").split()


def _forward(x, c, ctx, c_ctx, p):
    nb, seq, d = x.shape
    lc = ctx.shape[1]
    depth = p["mod_w"].shape[0]
    alpha = (2 * depth) ** 0.25
    n_exp = p["moe_router"].shape[2]
    assert seq == GRID_W * GRID_W and seq % CHUNK == 0 and lc % CHUNK == 0
    w = _prep_weights(p)

    cc = _pad_to(jnp.concatenate([c, c_ctx[None, :]], axis=0).astype(F32), 0, V7X_BF16_ROWS)
    mods = [
        _matmul(cc, w["mod_w"][i], p["mod_b"][i], pre=_silu, out_dtype=F32, name="adaln_modulation")
        .reshape(cc.shape[0], 6, d) for i in range(depth)]

    def chunk(i, k, stream):
        m = mods[i][:, k]
        return m[:nb, None, :] if stream == "x" else m[nb:nb + 1, None, :]

    lay = {"x": _RowLayout(nb, seq, True), "c": _RowLayout(nb, lc, False),
           "xcol": _RowLayout(nb, seq, True, mode="cols")}
    seqs = {"x": seq, "c": lc}
    h = {"x": x.reshape(nb * seq, d).astype(F32), "c": ctx.reshape(nb * lc, d).astype(F32)}

    u = {s: _modulate(lay[s], h[s], chunk(0, 1, s), chunk(0, 0, s)) for s in ("x", "c")}

    for i in range(depth):
        last = i == depth - 1
        kind = i % 4
        moe = i % 2 == 1
        streams = ("x",) if last else ("x", "c")

        def tail1(s):
            t = dict(h=h[s], gate=chunk(i, 2, s), ln_g=p["ln_g"][i, 0], ln_b=p["ln_b"][i, 0],
                     scale=chunk(i, 4, s), shift=chunk(i, 3, s))
            if moe:
                t.update(router=w["moe_router"][i // 2], n_exp=n_exp)
            return t

        res = {}
        if kind == 0:
            for s in streams:
                z = _hyena_mixer(u[s], nb, seqs[s], p, w)
                res[s] = _hyena_out(lay[s], z, tail1(s), p, w, nb, seqs[s], alpha=alpha, name=f"hyena_out_{s}")
        elif kind == 1:
            qc = _rw_project(u["c"], lc, p, w)
            qx = _rw_project(u["x"], seq, p, w)
            s0 = jnp.zeros((nb, 2, d // (2 * RW_HEAD), 2 * RW_HEAD, 2 * RW_HEAD), F32)
            yc, s_c = _rw_scan(qc, s0, nb, lc, d)
            yx, _ = _rw_scan(qx, s_c, nb, seq, d)
            ys, qs = {"x": yx, "c": yc}, {"x": qx, "c": qc}
            for s in streams:
                res[s] = _rw_out(lay[s], ys[s], qs[s], tail1(s), p, w, alpha=alpha, name=f"rwkv7_out_{s}")
        elif kind == 2:
            zc, xbc_c, dt_c = _ssd_project(u["c"], nb, lc, p, w, False)
            zx, xbc_x, dt_x = _ssd_project(u["x"], nb, seq, p, w, True)
            inner = w["ssd_w_out"].shape[0]
            s0 = jnp.zeros((nb, 2, SSD_GROUPS, SSD_STATE, inner // SSD_GROUPS), F32)
            yc, s_c = _ssd_scan(xbc_c, dt_c, s0, nb, lc, p)
            yx, _ = _ssd_scan(xbc_x, dt_x, s_c, nb, seq, p)
            res["x"] = _ssd_out(lay["xcol"], yx, zx, xbc_x, tail1("x"), p, w, alpha=alpha, name="ssd_out_x")
            if not last:
                res["c"] = _ssd_out(lay["c"], yc, zc, xbc_c, tail1("c"), p, w, alpha=alpha, name="ssd_out_c")
        else:
            qkv_c, zc, ab_c = _gdn_project(u["c"], nb, lc, p, w, False)
            qkv_x, zx, ab_x = _gdn_project(u["x"], nb, seq, p, w, True)
            vw = w["gdn_w_out"].shape[0]
            s0 = jnp.zeros((nb, 2, vw // GDN_HEAD, GDN_HEAD, GDN_HEAD), F32)
            oc, s_c = _gdn_scan(qkv_c, ab_c, s0, nb, lc, p)
            ox, _ = _gdn_scan(qkv_x, ab_x, s_c, nb, seq, p)
            res["x"] = _gdn_out(lay["xcol"], ox, zx, tail1("x"), p, w, alpha=alpha, name="gdn_out_x")
            if not last:
                res["c"] = _gdn_out(lay["c"], oc, zc, tail1("c"), p, w, alpha=alpha, name="gdn_out_c")

        for s in streams:
            t = dict(h=res[s]["h"], gate=chunk(i, 5, s), ln_g=p["ln_g"][i, 1], ln_b=p["ln_b"][i, 1])
            if not last:
                t.update(scale=chunk(i + 1, 1, s), shift=chunk(i + 1, 0, s))
            if moe:
                out = _moe_call(lay[s], res[s]["u"], w["moe_w13"][i // 2], w["moe_w2"][i // 2], res[s]["comb"], t,
                                alpha=alpha, name=f"moe_ffn_{s}")
            else:
                out = _ffn_call(lay[s], res[s]["u"], w["ffn_w13"][i // 2], w["ffn_w2"][i // 2], t,
                                alpha=alpha, name=f"dense_ffn_{s}")
            h[s] = out["h"]
            if not last:
                u[s] = out["u"]
    return h["x"].reshape(nb, seq, d).astype(x.dtype)


def kernel(x, c, ctx, c_ctx, mod_w, mod_b, ln_g, ln_b, hy_w_in, hy_b_in, hy_conv_w, hy_conv_b, hy_f_w1, hy_f_b1, hy_f_w2, hy_f_b2, hy_f_freq, hy_f_w3, hy_decay, hy_skip, hy_w_out, hy_b_out, rw_mu, rw_w_rkv, rw_w0, rw_w1, rw_w2, rw_a0, rw_a1, rw_a2, rw_g1, rw_g2, rw_k_k, rw_k_a, rw_r_k, rw_lnx_g, rw_lnx_b, rw_w_out, ssd_w_in, ssd_conv_w, ssd_conv_b, ssd_dt_bias, ssd_A_log, ssd_d_skip, ssd_norm_g, ssd_w_out, gdn_w_in, gdn_conv_w, gdn_dt_bias, gdn_A_log, gdn_norm_g, gdn_w_out, ffn_w13, ffn_w2, moe_router, moe_w13, moe_w2):
    vals = (mod_w, mod_b, ln_g, ln_b, hy_w_in, hy_b_in, hy_conv_w, hy_conv_b, hy_f_w1, hy_f_b1, hy_f_w2, hy_f_b2,
            hy_f_freq, hy_f_w3, hy_decay, hy_skip, hy_w_out, hy_b_out, rw_mu, rw_w_rkv, rw_w0, rw_w1, rw_w2, rw_a0,
            rw_a1, rw_a2, rw_g1, rw_g2, rw_k_k, rw_k_a, rw_r_k, rw_lnx_g, rw_lnx_b, rw_w_out, ssd_w_in, ssd_conv_w,
            ssd_conv_b, ssd_dt_bias, ssd_A_log, ssd_d_skip, ssd_norm_g, ssd_w_out, gdn_w_in, gdn_conv_w,
            gdn_dt_bias, gdn_A_log, gdn_norm_g, gdn_w_out, ffn_w13, ffn_w2, moe_router, moe_w13, moe_w2)
    return _forward(x, c, ctx, c_ctx, dict(zip(_PARAM_NAMES, vals)))
```

```python
import functools
import math

import jax
import jax.numpy as jnp
import numpy as np
from jax import lax
from jax.experimental import pallas as pl
from jax.experimental.pallas import tpu as pltpu

F32 = jnp.float32
BF16 = jnp.bfloat16
HIGHEST = lax.Precision.HIGHEST

GRID_W = 64
LN_EPS = 1e-5
HY_BANDS = 8
RW_HEAD = 64
RW_GN_EPS = 64e-5
SSD_HEAD = 64
SSD_STATE = 128
SSD_GROUPS = 4
GDN_HEAD = 128
CHUNK = 64
TOP_K = 2

V7X_LANES = 128
V7X_SUBLANES = 8
V7X_BF16_ROWS = 16
V7X_VMEM_BUDGET = 56 * 1024 * 1024

NT_DIMS = (((1,), (1,)), ((), ()))
TN_DIMS = (((0,), (0,)), ((), ()))


def _tile(n, pref, mult):
    best = None
    t = mult
    while t <= min(n, pref):
        if n % t == 0:
            best = t
        t += mult
    return best if best is not None else n


def _pad_to(a, axis, size):
    pad = size - a.shape[axis]
    if pad == 0:
        return a
    cfg = [(0, 0)] * a.ndim
    cfg[axis] = (0, pad)
    return jnp.pad(a, cfg)


def _cparams(sem):
    return pltpu.CompilerParams(dimension_semantics=sem, vmem_limit_bytes=V7X_VMEM_BUDGET)


def _silu(x):
    return x * jax.nn.sigmoid(x)


def _softplus(x):
    return jnp.maximum(x, 0.0) + jnp.log1p(jnp.exp(-jnp.abs(x)))


def _bdot(a, b):
    return jnp.dot(a.astype(BF16), b.astype(BF16), preferred_element_type=F32)


def _bdot_nt(a, b):
    return lax.dot_general(a.astype(BF16), b.astype(BF16), NT_DIMS, preferred_element_type=F32)


def _hdot(a, b):
    return jnp.dot(a, b, precision=HIGHEST, preferred_element_type=F32)


def _split_dot(x, m):
    hi = x.astype(BF16)
    lo = (x - hi.astype(F32)).astype(BF16)
    return jnp.dot(hi, m, preferred_element_type=F32) + jnp.dot(lo, m, preferred_element_type=F32)


def _transpose_via_eye(x, eye):
    return lax.dot_general(eye, x, NT_DIMS, precision=HIGHEST, preferred_element_type=F32)


def _head_sum(x, expand):
    hi = x.astype(BF16)
    lo = (x - hi.astype(F32)).astype(BF16)
    s = (lax.dot_general(hi, expand, NT_DIMS, preferred_element_type=F32)
         + lax.dot_general(lo, expand, NT_DIMS, preferred_element_type=F32))
    return _split_dot(s, expand)


def _expand_mat(n_in_pad, n_heads, width):
    m = np.zeros((n_in_pad, n_heads * width), np.float32)
    for h in range(n_heads):
        m[h, h * width:(h + 1) * width] = 1.0
    return jnp.asarray(m, BF16)


def _perm_mat(ib, jb):
    n = ib * jb
    q = np.arange(n)
    p = (q % ib) * jb + q // ib
    m = np.zeros((n, n), np.float32)
    m[q, p] = 1.0
    return jnp.asarray(m, BF16)


def _time_masks(d, c):
    ii = lax.broadcasted_iota(jnp.int32, (c, c), 0)
    jj = lax.broadcasted_iota(jnp.int32, (c, c), 1)
    lag = (ii - jj) * (1 - 2 * d)
    return lag >= 0, lag > 0


def _mm_kernel(*refs, nk, pre, act, has_bias, has_perm):
    it = iter(refs)
    x_ref, w_ref = next(it), next(it)
    b_ref = next(it) if has_bias else None
    p_ref = next(it) if has_perm else None
    o_ref = next(it)
    acc_ref = next(it) if nk > 1 else None
    xp_ref = next(it) if has_perm else None

    def load_x():
        x = x_ref[...]
        x = x.reshape(-1, x.shape[-1])
        if pre is not None:
            x = pre(x.astype(F32))
        return x.astype(BF16)

    if has_perm:
        @pl.when(pl.program_id(3) == 0)
        def _():
            xp_ref[...] = jnp.dot(p_ref[...], load_x(), preferred_element_type=F32).astype(BF16)

        x = xp_ref[...]
    else:
        x = load_x()
    part = jnp.dot(x, w_ref[...], preferred_element_type=F32)

    def finish(r):
        if has_bias:
            r = r + b_ref[...]
        if act is not None:
            r = act(r)
        o_ref[...] = r.astype(o_ref.dtype).reshape(o_ref.shape)

    if nk == 1:
        finish(part)
    else:
        k = pl.program_id(2)

        @pl.when(k == 0)
        def _():
            acc_ref[...] = jnp.zeros_like(acc_ref)

        acc_ref[...] += part

        @pl.when(k == nk - 1)
        def _():
            finish(acc_ref[...])


def _matmul(x, w, bias=None, *, out_dtype=F32, act=None, pre=None, tm=512, tn=2048, tk=None, name="matmul"):
    m, kdim = x.shape
    n = w.shape[1]
    tm = _tile(m, tm, V7X_BF16_ROWS)
    tn = _tile(n, tn, V7X_LANES)
    tk = kdim if tk is None else _tile(kdim, tk, V7X_LANES)
    nk = kdim // tk
    in_specs = [pl.BlockSpec((tm, tk), lambda i, j, k: (i, k)),
                pl.BlockSpec((tk, tn), lambda i, j, k: (k, j))]
    args = [x, w]
    if bias is not None:
        in_specs.append(pl.BlockSpec((1, tn), lambda i, j, k: (0, j)))
        args.append(bias.reshape(1, n).astype(F32))
    return pl.pallas_call(
        functools.partial(_mm_kernel, nk=nk, pre=pre, act=act, has_bias=bias is not None, has_perm=False),
        grid=(m // tm, n // tn, nk),
        in_specs=in_specs,
        out_specs=pl.BlockSpec((tm, tn), lambda i, j, k: (i, j)),
        out_shape=jax.ShapeDtypeStruct((m, n), out_dtype),
        scratch_shapes=[pltpu.VMEM((tm, tn), F32)] if nk > 1 else [],
        compiler_params=_cparams(("parallel", "parallel", "arbitrary")),
        name=name,
    )(*args)


def _col_tiles(gw):
    ib = min(V7X_BF16_ROWS, gw)
    jb = min(32, gw)
    return ib, jb


def _matmul_r2c(u, w, bias, nb, *, out_dtype, tn=2048, name="matmul_r2c"):
    gw = GRID_W
    kdim, n = w.shape
    ib, jb = _col_tiles(gw)
    tn = _tile(n, tn, V7X_LANES)
    u4 = u.reshape(nb, gw, gw, kdim)
    in_specs = [pl.BlockSpec((1, ib, jb, kdim), lambda b, i, j, c: (b, i, j, 0)),
                pl.BlockSpec((kdim, tn), lambda b, i, j, c: (0, c))]
    args = [u4, w]
    if bias is not None:
        in_specs.append(pl.BlockSpec((1, tn), lambda b, i, j, c: (0, c)))
        args.append(bias.reshape(1, n).astype(F32))
    in_specs.append(pl.BlockSpec((ib * jb, ib * jb), lambda b, i, j, c: (0, 0)))
    args.append(_perm_mat(ib, jb))
    out = pl.pallas_call(
        functools.partial(_mm_kernel, nk=1, pre=None, act=None, has_bias=bias is not None, has_perm=True),
        grid=(nb, gw // ib, gw // jb, n // tn),
        in_specs=in_specs,
        out_specs=pl.BlockSpec((1, jb, ib, tn), lambda b, i, j, c: (b, j, i, c)),
        out_shape=jax.ShapeDtypeStruct((nb, gw, gw, n), out_dtype),
        scratch_shapes=[pltpu.VMEM((ib * jb, kdim), BF16)],
        compiler_params=_cparams(("parallel", "parallel", "parallel", "arbitrary")),
        name=name,
    )(*args)
    return out.reshape(nb * gw * gw, n)


def _ln_epilogue(h, y, gate, ln_g, ln_b, alpha):
    pre = alpha * h + gate * y
    mu = jnp.mean(pre, axis=-1, keepdims=True)
    xc = pre - mu
    var = jnp.mean(xc * xc, axis=-1, keepdims=True)
    return xc * lax.rsqrt(var + LN_EPS) * ln_g + ln_b


def _route(logits, n_exp):
    lane = lax.broadcasted_iota(jnp.int32, logits.shape, 1)
    neg = jnp.float32(-jnp.inf)
    lg = jnp.where(lane < n_exp, logits, neg)
    big = jnp.int32(logits.shape[1])
    m1 = jnp.max(lg, axis=-1, keepdims=True)
    i1 = jnp.min(jnp.where(lg == m1, lane, big), axis=-1, keepdims=True)
    lg2 = jnp.where(lane == i1, neg, lg)
    m2 = jnp.max(lg2, axis=-1, keepdims=True)
    i2 = jnp.min(jnp.where(lg2 == m2, lane, big), axis=-1, keepdims=True)
    e2 = jnp.exp(m2 - m1)
    den = 1.0 + e2
    return jnp.where(lane == i1, 1.0 / den, jnp.where(lane == i2, e2 / den, 0.0))


def _finish_rows(refs, y, *, alpha, emit_u, n_exp):
    h_ref, gate_ref, lng_ref, lnb_ref = refs["h"], refs["gate"], refs["ln_g"], refs["ln_b"]
    h = h_ref[...].reshape(y.shape)
    hn = _ln_epilogue(h, y, gate_ref[0], lng_ref[...], lnb_ref[...], alpha)
    refs["h_out"][...] = hn.reshape(refs["h_out"].shape)
    if emit_u:
        u = hn * (1.0 + refs["scale"][0]) + refs["shift"][0]
        refs["u_out"][...] = u.astype(BF16).reshape(refs["u_out"].shape)
        if n_exp:
            logits = _hdot(u, refs["router"][...])
            refs["comb_out"][...] = _route(logits, n_exp).reshape(refs["comb_out"].shape)


class _RowLayout:
    def __init__(self, nb, seq, per_batch_mod, mode="rows", tm=512):
        self.nb, self.seq, self.mode, self.per_batch_mod = nb, seq, mode, per_batch_mod
        if mode == "rows":
            self.tm = _tile(seq, tm, V7X_BF16_ROWS)
            self.tps = seq // self.tm
            self.grid = (nb * self.tps,)
            self.rows = self.tm
        else:
            self.ib, self.jb = _col_tiles(GRID_W)
            self.grid = (nb, GRID_W // self.ib, GRID_W // self.jb)
            self.rows = self.ib * self.jb
        self.ngrid = len(self.grid)

    def sem(self, extra=()):
        return ("parallel",) * self.ngrid + tuple(extra)

    def _ix(self, fn):
        n = self.ngrid
        return lambda *g: fn(*g[:n])

    def raster(self, arr, c, cblock=0, lead=None):
        pre_shape = () if lead is None else (arr.shape[0],)
        pre_blk = () if lead is None else (1,)
        pre_ix = () if lead is None else (lead,)
        if self.mode == "rows":
            return arr, pl.BlockSpec(pre_blk + (self.tm, c), self._ix(lambda i: pre_ix + (i, cblock)))
        a4 = arr.reshape(pre_shape + (self.nb, GRID_W, GRID_W, arr.shape[-1]))
        return a4, pl.BlockSpec(pre_blk + (1, self.ib, self.jb, c),
                                self._ix(lambda b, i, j: pre_ix + (b, i, j, cblock)))

    def colmajor(self, arr, c, cblock=0, lead=None):
        assert self.mode == "cols"
        pre_shape = () if lead is None else (arr.shape[0],)
        pre_blk = () if lead is None else (1,)
        pre_ix = () if lead is None else (lead,)
        a4 = arr.reshape(pre_shape + (self.nb, GRID_W, GRID_W, arr.shape[-1]))
        return a4, pl.BlockSpec(pre_blk + (1, self.jb, self.ib, c),
                                self._ix(lambda b, i, j: pre_ix + (b, j, i, cblock)))

    def native(self, arr, c, cblock=0, lead=None):
        if self.mode == "cols":
            return self.colmajor(arr, c, cblock, lead)
        return self.raster(arr, c, cblock, lead)

    def mod(self, arr):
        d = arr.shape[-1]
        if not self.per_batch_mod:
            return arr, pl.BlockSpec((1, 1, d), self._ix(lambda *g: (0, 0, 0)))
        if self.mode == "rows":
            tps = self.tps
            return arr, pl.BlockSpec((1, 1, d), self._ix(lambda i: (i // tps, 0, 0)))
        return arr, pl.BlockSpec((1, 1, d), self._ix(lambda b, i, j: (b, 0, 0)))

    def const(self, arr):
        nd = arr.ndim
        return arr, pl.BlockSpec(arr.shape, self._ix(lambda *g: (0,) * nd))

    def out_raster(self, n_rows, c, dtype):
        if self.mode == "rows":
            return (jax.ShapeDtypeStruct((n_rows, c), dtype),
                    pl.BlockSpec((self.tm, c), self._ix(lambda i: (i, 0))))
        return (jax.ShapeDtypeStruct((self.nb, GRID_W, GRID_W, c), dtype),
                pl.BlockSpec((1, self.ib, self.jb, c), self._ix(lambda b, i, j: (b, i, j, 0))))


def _out_proj_call(lay, prologue, pro_inputs, w_out, bias, tail, *, alpha, name):
    d = w_out.shape[1]
    n_rows = lay.nb * lay.seq
    emit_u = "scale" in tail
    n_exp = tail.get("n_exp", 0)
    names, args, specs = [], [], []

    def add(nm, pair):
        names.append(nm)
        args.append(pair[0])
        specs.append(pair[1])

    for k, pair in enumerate(pro_inputs):
        add(f"p{k}", pair)
    add("w", lay.const(w_out))
    if bias is not None:
        add("bias", lay.const(bias.reshape(1, d).astype(F32)))
    if lay.mode == "cols":
        add("perm", lay.const(_perm_mat(lay.ib, lay.jb).T))
    add("h", lay.raster(tail["h"], d))
    add("gate", lay.mod(tail["gate"]))
    add("ln_g", lay.const(tail["ln_g"].reshape(1, d)))
    add("ln_b", lay.const(tail["ln_b"].reshape(1, d)))
    if emit_u:
        add("scale", lay.mod(tail["scale"]))
        add("shift", lay.mod(tail["shift"]))
    if n_exp:
        add("router", lay.const(tail["router"]))
    out_names, out_shapes, out_specs = [], [], []

    def add_out(nm, pair):
        out_names.append(nm)
        out_shapes.append(pair[0])
        out_specs.append(pair[1])

    add_out("h_out", lay.out_raster(n_rows, d, F32))
    if emit_u:
        add_out("u_out", lay.out_raster(n_rows, d, BF16))
    if n_exp:
        add_out("comb_out", lay.out_raster(n_rows, V7X_LANES, F32))
    n_pro = len(pro_inputs)
    n_in = len(names)

    def kernel(*refs):
        r = dict(zip(names + out_names, refs))
        z = prologue(*[refs[k] for k in range(n_pro)])
        z = z.astype(BF16)
        if lay.mode == "cols":
            z = jnp.dot(r["perm"][...], z, preferred_element_type=F32).astype(BF16)
        y = jnp.dot(z, r["w"][...], preferred_element_type=F32)
        if bias is not None:
            y = y + r["bias"][...]
        _finish_rows(r, y, alpha=alpha, emit_u=emit_u, n_exp=n_exp)

    outs = pl.pallas_call(
        kernel, grid=lay.grid, in_specs=specs, out_specs=out_specs, out_shape=out_shapes,
        compiler_params=_cparams(lay.sem()), name=name)(*args)
    outs = [o.reshape(n_rows, o.shape[-1]) for o in outs]
    res = {"h": outs[0]}
    if emit_u:
        res["u"] = outs[1]
    if n_exp:
        res["comb"] = outs[2]
    return res


def _modulate_kernel(h_ref, scale_ref, shift_ref, u_ref):
    u_ref[...] = (h_ref[...] * (1.0 + scale_ref[0]) + shift_ref[0]).astype(u_ref.dtype)


def _modulate(lay, h, scale, shift):
    n_rows, d = h.shape
    pairs = [lay.raster(h, d), lay.mod(scale), lay.mod(shift)]
    out = lay.out_raster(n_rows, d, BF16)
    return pl.pallas_call(
        _modulate_kernel, grid=lay.grid, in_specs=[pr[1] for pr in pairs], out_specs=out[1], out_shape=out[0],
        compiler_params=_cparams(lay.sem()), name="adaln_modulate")(*[pr[0] for pr in pairs])


def _ffn_kernel(*refs, names, n_f, alpha, emit_u):
    r = dict(zip(names, refs))
    f = pl.program_id(1)

    @pl.when(f == 0)
    def _():
        r["acc"][...] = jnp.zeros_like(r["acc"])

    u = r["u"][...]
    gate = jnp.dot(u, r["w1"][...], preferred_element_type=F32)
    up = jnp.dot(u, r["w3"][...], preferred_element_type=F32)
    hid = _silu(gate) * up
    r["acc"][...] += jnp.dot(hid.astype(BF16), r["w2"][...], preferred_element_type=F32)

    @pl.when(f == n_f - 1)
    def _():
        _finish_rows(r, r["acc"][...], alpha=alpha, emit_u=emit_u, n_exp=0)


def _ffn_call(lay, u, w13, w2, tail, *, alpha, name):
    d, f2 = w13.shape
    fdim = f2 // 2
    tf = _tile(fdim, 1536, V7X_LANES)
    n_f = fdim // tf
    n_rows = lay.nb * lay.seq
    emit_u = "scale" in tail
    names, args, specs = [], [], []

    def add(nm, pair):
        names.append(nm)
        args.append(pair[0])
        specs.append(pair[1])

    add("u", lay.raster(u, d))
    add("w1", (w13, pl.BlockSpec((d, tf), lambda i, f: (0, f))))
    add("w3", (w13, pl.BlockSpec((d, tf), lambda i, f: (0, n_f + f))))
    add("w2", (w2, pl.BlockSpec((tf, d), lambda i, f: (f, 0))))
    add("h", lay.raster(tail["h"], d))
    add("gate", lay.mod(tail["gate"]))
    add("ln_g", lay.const(tail["ln_g"].reshape(1, d)))
    add("ln_b", lay.const(tail["ln_b"].reshape(1, d)))
    if emit_u:
        add("scale", lay.mod(tail["scale"]))
        add("shift", lay.mod(tail["shift"]))
    out_names = ["h_out"] + (["u_out"] if emit_u else [])
    outs = [lay.out_raster(n_rows, d, F32)] + ([lay.out_raster(n_rows, d, BF16)] if emit_u else [])
    kernel = functools.partial(_ffn_kernel, names=names + out_names + ["acc"], n_f=n_f, alpha=alpha, emit_u=emit_u)
    res = pl.pallas_call(
        kernel, grid=lay.grid + (n_f,), in_specs=specs,
        out_specs=[o[1] for o in outs], out_shape=[o[0] for o in outs],
        scratch_shapes=[pltpu.VMEM((lay.rows, d), F32)],
        compiler_params=_cparams(lay.sem(("arbitrary",))), name=name)(*args)
    out = {"h": res[0]}
    if emit_u:
        out["u"] = res[1]
    return out


MOE_TOKENS = 1024
MOE_ROW_ALIGN = 32
V7X_MXU_DIM = 256


def _moe_group_rows(tb):
    return min(tb, V7X_MXU_DIM)


def _moe_kernel(cnt_ref, u_ref, comb_ref, rank_ref, rankt_ref, w1_ref, w3_ref, w2_ref, out_ref, xe_ref, ye_ref, cw_ref,
                *, n_f, tb, rc):
    blk, e, f = pl.program_id(0), pl.program_id(1), pl.program_id(2)
    n_groups = lax.div(cnt_ref[blk, e] + (rc - 1), jnp.int32(rc))
    lane = lax.broadcasted_iota(jnp.int32, (rc, V7X_LANES), 1)

    @pl.when((e == 0) & (f == 0))
    def _():
        out_ref[...] = jnp.zeros_like(out_ref)

    @pl.when(f == 0)
    def _():
        want = rankt_ref[0, pl.ds(e, 1), :]
        comb = comb_ref[...]
        comb_hi = comb.astype(BF16)
        comb_lo = (comb - comb_hi.astype(F32)).astype(BF16)
        slot = lax.broadcasted_iota(jnp.int32, (rc, tb), 0).astype(F32)

        def gather(g, carry):
            rows = pl.ds(pl.multiple_of(g * rc, MOE_ROW_ALIGN), rc)
            onehot = (want == slot + (g * rc).astype(F32)).astype(BF16)
            xe_ref[rows, :] = jnp.dot(onehot, u_ref[...], preferred_element_type=F32).astype(BF16)
            cw_ref[rows, :] = (jnp.dot(onehot, comb_hi, preferred_element_type=F32)
                               + jnp.dot(onehot, comb_lo, preferred_element_type=F32))
            ye_ref[rows, :] = jnp.zeros((rc, ye_ref.shape[1]), F32)
            return carry

        lax.fori_loop(0, n_groups, gather, 0)

    def expert(g, carry):
        rows = pl.ds(pl.multiple_of(g * rc, MOE_ROW_ALIGN), rc)
        x = xe_ref[rows, :]
        hid = _silu(jnp.dot(x, w1_ref[0], preferred_element_type=F32)) * jnp.dot(x, w3_ref[0], preferred_element_type=F32)
        cw = jnp.sum(jnp.where(lane == e, cw_ref[rows, :], 0.0), axis=-1, keepdims=True)
        ye_ref[rows, :] += jnp.dot((hid * cw).astype(BF16), w2_ref[0], preferred_element_type=F32)
        return carry

    lax.fori_loop(0, n_groups, expert, 0)

    @pl.when(f == n_f - 1)
    def _():
        lane_t = lax.broadcasted_iota(jnp.int32, (tb, V7X_LANES), 1)
        want_col = jnp.sum(jnp.where(lane_t == e, rank_ref[...], 0.0), axis=-1, keepdims=True)
        slot_t = lax.broadcasted_iota(jnp.int32, (tb, rc), 1).astype(F32)

        def scatter(g, carry):
            rows = pl.ds(pl.multiple_of(g * rc, MOE_ROW_ALIGN), rc)
            onehot_t = (want_col == slot_t + (g * rc).astype(F32)).astype(BF16)
            out_ref[...] += jnp.dot(onehot_t, ye_ref[rows, :].astype(BF16), preferred_element_type=F32)
            return carry

        lax.fori_loop(0, n_groups, scatter, 0)


def _moe_call(lay, u, w13, w2, comb, tail, *, alpha, name):
    n_e, d, f2 = w13.shape
    fdim = f2 // 2
    tf = _tile(fdim, 1536, V7X_LANES)
    n_f = fdim // tf
    n_rows = u.shape[0]
    tb = _tile(n_rows, MOE_TOKENS, MOE_ROW_ALIGN)
    rc = _moe_group_rows(tb)
    cap = rc * ((tb + rc - 1) // rc)
    n_blk = n_rows // tb
    lanes = V7X_LANES
    assert n_e == V7X_SUBLANES
    routed = (comb[:, :n_e] > 0.0).astype(jnp.int32).reshape(n_blk, tb, n_e)
    rank = jnp.where(routed > 0, jnp.cumsum(routed, axis=1) - routed, -1).astype(F32)
    cnt = jnp.sum(routed, axis=1)
    rank_col = jnp.pad(rank.reshape(n_rows, n_e), ((0, 0), (0, lanes - n_e)), constant_values=-1.0)
    rank_row = jnp.transpose(rank, (0, 2, 1))
    grid_spec = pltpu.PrefetchScalarGridSpec(
        num_scalar_prefetch=1, grid=(n_blk, n_e, n_f),
        in_specs=[pl.BlockSpec((tb, d), lambda i, e, f, c: (i, 0)),
                  pl.BlockSpec((tb, lanes), lambda i, e, f, c: (i, 0)),
                  pl.BlockSpec((tb, lanes), lambda i, e, f, c: (i, 0)),
                  pl.BlockSpec((1, n_e, tb), lambda i, e, f, c: (i, 0, 0)),
                  pl.BlockSpec((1, d, tf), lambda i, e, f, c: (e, 0, f)),
                  pl.BlockSpec((1, d, tf), lambda i, e, f, c: (e, 0, n_f + f)),
                  pl.BlockSpec((1, tf, d), lambda i, e, f, c: (e, f, 0))],
        out_specs=pl.BlockSpec((tb, d), lambda i, e, f, c: (i, 0)),
        scratch_shapes=[pltpu.VMEM((cap, d), BF16), pltpu.VMEM((cap, d), F32), pltpu.VMEM((cap, lanes), F32)])
    y = pl.pallas_call(
        functools.partial(_moe_kernel, n_f=n_f, tb=tb, rc=rc), grid_spec=grid_spec,
        out_shape=jax.ShapeDtypeStruct((n_rows, d), F32),
        compiler_params=_cparams(("parallel", "arbitrary", "arbitrary")), name=name,
    )(cnt, u, comb, rank_col, rank_row, w13, w13, w2)
    return _tail_call(lay, y, tail, alpha=alpha, name=name + "_tail")


def _tail_call(lay, y, tail, *, alpha, name):
    n_rows, d = y.shape
    emit_u = "scale" in tail
    names, args, specs = [], [], []

    def add(nm, pair):
        names.append(nm)
        args.append(pair[0])
        specs.append(pair[1])

    add("y", lay.raster(y, d))
    add("h", lay.raster(tail["h"], d))
    add("gate", lay.mod(tail["gate"]))
    add("ln_g", lay.const(tail["ln_g"].reshape(1, d)))
    add("ln_b", lay.const(tail["ln_b"].reshape(1, d)))
    if emit_u:
        add("scale", lay.mod(tail["scale"]))
        add("shift", lay.mod(tail["shift"]))
    out_names = ["h_out"] + (["u_out"] if emit_u else [])
    outs = [lay.out_raster(n_rows, d, F32)] + ([lay.out_raster(n_rows, d, BF16)] if emit_u else [])

    def kernel(*refs):
        r = dict(zip(names + out_names, refs))
        _finish_rows(r, r["y"][...], alpha=alpha, emit_u=emit_u, n_exp=0)

    res = pl.pallas_call(
        kernel, grid=lay.grid, in_specs=specs, out_specs=[o[1] for o in outs], out_shape=[o[0] for o in outs],
        compiler_params=_cparams(lay.sem()), name=name)(*args)
    out = {"h": res[0]}
    if emit_u:
        out["u"] = res[1]
    return out


def _shift_rows(cur, prev_row, next_row):
    n = cur.shape[0]
    rows = lax.broadcasted_iota(jnp.int32, cur.shape, 0)
    up = jnp.where(rows == 0, prev_row, pltpu.roll(cur, 1, 0))
    dn = jnp.where(rows == n - 1, next_row, pltpu.roll(cur, n - 1, 0))
    return up, dn


def _halo_rows(i, tps, xp_ref, xn_ref, hb):
    t = i % tps
    prev_row = jnp.where(t == 0, 0.0, xp_ref[hb - 1:hb, :].astype(F32))
    next_row = jnp.where(t == tps - 1, 0.0, xn_ref[0:1, :].astype(F32))
    return prev_row, next_row


def _halo_rows_index(tm, n_rows, hb):
    r = tm // hb
    last = n_rows // hb - 1
    return (lambda i: jnp.maximum(i * r - 1, 0)), (lambda i: jnp.minimum((i + 1) * r, last))


def _conv3_kernel(x_ref, xp_ref, xn_ref, w_ref, b_ref, o_ref, *, tps, act):
    i = pl.program_id(0)
    cur = x_ref[...].astype(F32)
    prev_row, next_row = _halo_rows(i, tps, xp_ref, xn_ref, xp_ref.shape[0])
    up, dn = _shift_rows(cur, prev_row, next_row)
    w = w_ref[...]
    y = w[0:1] * up + w[1:2] * cur + w[2:3] * dn + b_ref[...]
    if act is not None:
        y = act(y)
    o_ref[...] = y.astype(o_ref.dtype).reshape(o_ref.shape)


def _conv3(x, w, b, seq, *, act=None, out_dtype=BF16, time_major_nb=None, name="conv3"):
    n_rows, c = x.shape
    hb = V7X_BF16_ROWS if x.dtype == BF16 else V7X_SUBLANES
    tm = _tile(seq, 512, hb)
    tps = seq // tm
    if b is None:
        b = jnp.zeros((c,), F32)
    if time_major_nb is None:
        ct = _tile(c, 1024, V7X_LANES)
        out_shape = jax.ShapeDtypeStruct((n_rows, c), out_dtype)
        out_spec = pl.BlockSpec((tm, ct), lambda i, cc: (i, cc))
    else:
        nb, d = time_major_nb
        ct = d
        out_shape = jax.ShapeDtypeStruct((c // d, seq, nb * d), out_dtype)
        out_spec = pl.BlockSpec((1, tm, d), lambda i, cc: (cc, i % tps, i // tps))
    prev, nxt = _halo_rows_index(tm, n_rows, hb)
    return pl.pallas_call(
        functools.partial(_conv3_kernel, tps=tps, act=act),
        grid=(n_rows // tm, c // ct),
        in_specs=[pl.BlockSpec((tm, ct), lambda i, cc: (i, cc)),
                  pl.BlockSpec((hb, ct), lambda i, cc: (prev(i), cc)),
                  pl.BlockSpec((hb, ct), lambda i, cc: (nxt(i), cc)),
                  pl.BlockSpec((3, ct), lambda i, cc: (0, cc)),
                  pl.BlockSpec((1, ct), lambda i, cc: (0, cc))],
        out_specs=out_spec, out_shape=out_shape,
        compiler_params=_cparams(("parallel", "parallel")), name=name,
    )(x, x, x, w.astype(F32), b.reshape(1, c).astype(F32))


def _hy_filter_kernel(bands_ref, w1_ref, b1_ref, w2_ref, b2_ref, fr_ref, w3_ref, dec_ref, sum_ref, dif_ref,
                      *, seq, tl, d):
    i = pl.program_id(0)
    pos = (lax.broadcasted_iota(jnp.int32, (tl, V7X_LANES), 0) + i * tl).astype(F32)
    lane = lax.broadcasted_iota(jnp.int32, (tl, V7X_LANES), 1)
    t01 = pos / float(max(seq - 1, 1))
    ang = (2.0 * math.pi / seq) * pos * bands_ref[...]
    feats = jnp.where(lane == 0, t01, jnp.where(lane <= HY_BANDS, jnp.cos(ang), -jnp.sin(ang)))
    fr = fr_ref[...]
    h = jnp.sin(fr[0:1] * (_hdot(feats, w1_ref[...]) + b1_ref[...]))
    h = jnp.sin(fr[1:2] * (_hdot(h, w2_ref[...]) + b2_ref[...]))
    k = _hdot(h, w3_ref[...]) * jnp.exp(-t01[:, 0:1] * jnp.abs(dec_ref[...]))
    not_first = (pos[:, 0:1] > 0.0).astype(F32)
    for o in range(2):
        kf = k[:, (2 * o) * d:(2 * o + 1) * d]
        kb = k[:, (2 * o + 1) * d:(2 * o + 2) * d] * not_first
        sum_ref[:, o * d:(o + 1) * d] = (kf + kb).astype(sum_ref.dtype)
        dif_ref[:, o * d:(o + 1) * d] = (kb - kf).astype(dif_ref.dtype)


def _hy_filters(seq, p, d):
    lanes = V7X_LANES
    fw = p["hy_f_w1"].shape[1]
    bands = jnp.linspace(1e-4, HY_BANDS - 1, HY_BANDS, dtype=F32)
    bands_row = _pad_to(jnp.concatenate([jnp.zeros((1,), F32), bands, bands])[None, :], 1, lanes)
    w1 = _pad_to(_pad_to(p["hy_f_w1"].astype(F32), 0, lanes), 1, lanes)
    b1 = _pad_to(p["hy_f_b1"].astype(F32)[None, :], 1, lanes)
    w2 = _pad_to(_pad_to(p["hy_f_w2"].astype(F32), 0, lanes), 1, lanes)
    b2 = _pad_to(p["hy_f_b2"].astype(F32)[None, :], 1, lanes)
    fr = _pad_to(_pad_to(p["hy_f_freq"].astype(F32), 1, lanes), 0, V7X_SUBLANES)
    w3 = _pad_to(p["hy_f_w3"].astype(F32), 0, lanes)
    dec = p["hy_decay"].astype(F32).reshape(1, 4 * d)
    assert fw <= lanes
    tl = _tile(seq, 256, V7X_BF16_ROWS)
    full = lambda a: pl.BlockSpec(a.shape, lambda i: (0,) * a.ndim)
    ins = [bands_row, w1, b1, w2, b2, fr, w3, dec]
    return pl.pallas_call(
        functools.partial(_hy_filter_kernel, seq=seq, tl=tl, d=d),
        grid=(seq // tl,),
        in_specs=[full(a) for a in ins],
        out_specs=[pl.BlockSpec((tl, 2 * d), lambda i: (i, 0))] * 2,
        out_shape=[jax.ShapeDtypeStruct((seq, 2 * d), BF16)] * 2,
        compiler_params=_cparams(("parallel",)), name="hyena_filters",
    )(*ins)


def _dft_kernel(c_ref, s_ref, ct_ref, st_ref, *, seq, tr):
    i = pl.program_id(0)
    lanes = min(V7X_LANES, seq)
    n_hi = seq // lanes
    row = lax.broadcasted_iota(jnp.int32, (tr, lanes), 0) + i * tr
    col = lax.broadcasted_iota(jnp.int32, (tr, lanes), 1)
    scale = math.pi / (2 * seq)
    mask = 4 * seq - 1
    trig = lambda m: (jnp.cos((m & mask).astype(F32) * scale), jnp.sin((m & mask).astype(F32) * scale))
    lo_c, lo_s = trig((2 * row + 1) * col)
    hi_c, hi_s = trig((2 * row + 1) * ((col * lanes) & mask))
    tlo_c, tlo_s = trig((2 * col + 1) * row)
    thi_c, thi_s = trig(((2 * lanes * col) & mask) * row)
    for h in range(n_hi):
        cols = slice(h * lanes, (h + 1) * lanes)
        ac, as_ = hi_c[:, h:h + 1], hi_s[:, h:h + 1]
        c_ref[:, cols] = (ac * lo_c - as_ * lo_s).astype(BF16)
        s_ref[:, cols] = (as_ * lo_c + ac * lo_s).astype(BF16)
        tc, ts = thi_c[:, h:h + 1], thi_s[:, h:h + 1]
        ct_ref[:, cols] = (tc * tlo_c - ts * tlo_s).astype(BF16)
        st_ref[:, cols] = (ts * tlo_c + tc * tlo_s).astype(BF16)


def _dft_mats(seq):
    assert seq & (seq - 1) == 0, "token count must be a power of two"
    tr = _tile(seq, 256, V7X_BF16_ROWS)
    spec = pl.BlockSpec((tr, seq), lambda i: (i, 0))
    return pl.pallas_call(
        functools.partial(_dft_kernel, seq=seq, tr=tr), grid=(seq // tr,), in_specs=[],
        out_specs=[spec] * 4, out_shape=[jax.ShapeDtypeStruct((seq, seq), BF16)] * 4,
        compiler_params=_cparams(("parallel",)), name="dft_matrices")()


def _hy_fwd_kernel(c_ref, s_ref, v_ref, kr_ref, ki_ref, wr_ref, wi_ref):
    v = v_ref[0]
    cv = jnp.dot(c_ref[...], v, preferred_element_type=F32)
    sv = jnp.dot(s_ref[...], v, preferred_element_type=F32)
    kr, ki = kr_ref[...], ki_ref[...]
    wr_ref[...] = (cv * kr + sv * ki).astype(wr_ref.dtype)
    wi_ref[...] = (cv * ki - sv * kr).astype(wi_ref.dtype)


def _hy_inv_kernel(ct_ref, st_ref, wr_ref, wi_ref, v_ref, g_ref, skip_ref, o_ref, *, seq):
    y = (jnp.dot(ct_ref[...], wr_ref[...], preferred_element_type=F32)
         - jnp.dot(st_ref[...], wi_ref[...], preferred_element_type=F32)) * (1.0 / seq)
    y = y + v_ref[0].astype(F32) * skip_ref[...]
    o_ref[0] = (g_ref[0].astype(F32) * y).astype(o_ref.dtype)


def _hy_long_conv(vsrc, v_idx, gsrc, g_idx, mats, kr, ki, order, skip, d):
    c, s, ct, st = mats
    _, seq, cols = vsrc.shape
    tm = _tile(seq, 512, V7X_BF16_ROWS)
    tn = _tile(d, 512, V7X_LANES)
    cpd = d // tn
    kspec = pl.BlockSpec((tm, tn), lambda i, j: (i, order * cpd + j % cpd))
    wr, wi = pl.pallas_call(
        _hy_fwd_kernel, grid=(seq // tm, cols // tn),
        in_specs=[pl.BlockSpec((tm, seq), lambda i, j: (i, 0)), pl.BlockSpec((tm, seq), lambda i, j: (i, 0)),
                  pl.BlockSpec((1, seq, tn), lambda i, j: (v_idx, 0, j)), kspec, kspec],
        out_specs=[pl.BlockSpec((tm, tn), lambda i, j: (i, j))] * 2,
        out_shape=[jax.ShapeDtypeStruct((seq, cols), BF16)] * 2,
        compiler_params=_cparams(("parallel", "parallel")), name="hyena_dft_fwd",
    )(c, s, vsrc, kr, ki)
    return pl.pallas_call(
        functools.partial(_hy_inv_kernel, seq=seq), grid=(seq // tm, cols // tn),
        in_specs=[pl.BlockSpec((tm, seq), lambda i, j: (i, 0)), pl.BlockSpec((tm, seq), lambda i, j: (i, 0)),
                  pl.BlockSpec((seq, tn), lambda i, j: (0, j)), pl.BlockSpec((seq, tn), lambda i, j: (0, j)),
                  pl.BlockSpec((1, tm, tn), lambda i, j: (v_idx, i, j)),
                  pl.BlockSpec((1, tm, tn), lambda i, j: (g_idx, i, j)),
                  pl.BlockSpec((1, tn), lambda i, j: (0, j % cpd))],
        out_specs=pl.BlockSpec((1, tm, tn), lambda i, j: (0, i, j)),
        out_shape=jax.ShapeDtypeStruct((1, seq, cols), BF16),
        compiler_params=_cparams(("parallel", "parallel")), name="hyena_dft_inv",
    )(ct, st, wr, wi, vsrc, gsrc, skip.reshape(1, d).astype(F32))


def _hyena_mixer(u, nb, seq, p, w):
    d = u.shape[1]
    proj = _matmul(u, w["hy_w_in"], p["hy_b_in"], out_dtype=BF16, name="hyena_in_proj")
    planes = _conv3(proj, p["hy_conv_w"], p["hy_conv_b"], seq, time_major_nb=(nb, d), name="hyena_short_conv")
    ksum, kdif = _hy_filters(seq, p, d)
    mats = _dft_mats(seq)
    kr = _matmul(mats[0], ksum, out_dtype=F32, tn=512, name="hyena_filter_spec_re")
    ki = _matmul(mats[1], kdif, out_dtype=F32, tn=512, name="hyena_filter_spec_im")
    z1 = _hy_long_conv(planes, 0, planes, 1, mats, kr, ki, 0, p["hy_skip"][0], d)
    return _hy_long_conv(z1, 0, planes, 2, mats, kr, ki, 1, p["hy_skip"][1], d)


def _hyena_out(lay_rows, z, tail, p, w, nb, seq, *, alpha, name):
    d = w["hy_w_out"].shape[0]
    tm = lay_rows.tm
    tps = lay_rows.tps
    spec = pl.BlockSpec((1, tm, d), lay_rows._ix(lambda i: (0, i % tps, i // tps)))
    return _out_proj_call(lay_rows, lambda z_ref: z_ref[0], [(z, spec)], w["hy_w_out"], p["hy_b_out"], tail,
                          alpha=alpha, name=name)


def _rw_proj_kernel(*refs, names, tps, hd):
    r = dict(zip(names, refs))
    i = pl.program_id(0)
    cur = r["u"][...].astype(F32)
    prev_row, next_row = _halo_rows(i, tps, r["up"], r["un"], V7X_BF16_ROWS)
    up, dn = _shift_rows(cur, prev_row, next_row)
    xx = 0.5 * (up + dn) - cur
    mu = r["mu"][...]
    mix = lambda j: (cur + xx * mu[j:j + 1]).astype(BF16)
    xr, xw, xk, xv, xa, xg = [mix(j) for j in range(6)]
    qq = r["qq"][...]
    rr = jnp.dot(xr, r["w_rkv"][0], preferred_element_type=F32)
    kk0 = jnp.dot(xk, r["w_rkv"][1], preferred_element_type=F32)
    vv = jnp.dot(xv, r["w_rkv"][2], preferred_element_type=F32)
    gg = _bdot(jax.nn.sigmoid(jnp.dot(xg, r["g1"][...], preferred_element_type=F32)), r["g2"][...])
    kx = kk0 * r["k_k"][...]
    kkn = kx * lax.rsqrt(_head_sum(kx * kx, qq) + 1e-6)
    r["r"][...] = rr.astype(BF16)
    r["v"][...] = vv.astype(BF16)
    r["g"][...] = gg.astype(BF16)
    r["na"][...] = (-kkn).astype(BF16)
    kd_sum = jnp.zeros_like(kk0)
    for dd in range(2):
        hw = jnp.tanh(jnp.dot(xw, r["lora_w1"][dd], preferred_element_type=F32))
        wpre = r["bias_w"][dd] + _bdot(hw, r["lora_w2"][dd])
        logw = -_softplus(-wpre) - 0.5
        r["lw"][dd] = -jnp.exp(logw)
        a = jax.nn.sigmoid(
            r["bias_a"][dd] + _bdot(jnp.dot(xa, r["lora_a1"][dd], preferred_element_type=F32), r["lora_a2"][dd]))
        kd = kk0 * (1.0 + (a - 1.0) * r["k_a"][...])
        r["kd"][dd] = kd.astype(BF16)
        r["b"][dd] = (kkn * a).astype(BF16)
        kd_sum = kd_sum + kd
    r["bonus"][...] = (_head_sum(rr * kd_sum * r["r_k"][...], qq) * vv).astype(BF16)


_RW_OUTS = ["r", "v", "g", "na", "bonus", "lw", "kd", "b"]
_RW_STACKED = ("lw", "kd", "b")


def _rw_project(u, seq, p, w):
    n_rows, d = u.shape
    hb = V7X_BF16_ROWS
    tm = _tile(seq, 256, hb)
    tps = seq // tm
    lanes = V7X_LANES
    names, args, specs = [], [], []

    def add(nm, arr, spec=None):
        names.append(nm)
        args.append(arr)
        nd = arr.ndim
        specs.append(spec if spec is not None else pl.BlockSpec(arr.shape, lambda i: (0,) * nd))

    prev, nxt = _halo_rows_index(tm, n_rows, hb)
    add("u", u, pl.BlockSpec((tm, d), lambda i: (i, 0)))
    add("up", u, pl.BlockSpec((hb, d), lambda i: (prev(i), 0)))
    add("un", u, pl.BlockSpec((hb, d), lambda i: (nxt(i), 0)))
    add("mu", _pad_to(p["rw_mu"].astype(F32), 0, V7X_SUBLANES))
    add("w_rkv", w["rw_w_rkv"])
    add("g1", w["rw_g1"])
    add("g2", w["rw_g2"])
    add("bias_w", p["rw_w0"].astype(F32).reshape(2, 1, d))
    add("lora_w1", w["rw_w1"])
    add("lora_w2", w["rw_w2"])
    add("bias_a", p["rw_a0"].astype(F32).reshape(2, 1, d))
    add("lora_a1", w["rw_a1"])
    add("lora_a2", w["rw_a2"])
    add("k_k", p["rw_k_k"].astype(F32).reshape(1, d))
    add("k_a", p["rw_k_a"].astype(F32).reshape(1, d))
    add("r_k", p["rw_r_k"].astype(F32).reshape(1, d))
    add("qq", _expand_mat(V7X_LANES, d // RW_HEAD, RW_HEAD))
    out_dt = {nm: (F32 if nm == "lw" else BF16) for nm in _RW_OUTS}
    row_spec = pl.BlockSpec((tm, d), lambda i: (i, 0))
    dir_spec = pl.BlockSpec((2, tm, d), lambda i: (0, i, 0))
    outs = pl.pallas_call(
        functools.partial(_rw_proj_kernel, names=names + _RW_OUTS, tps=tps, hd=RW_HEAD),
        grid=(n_rows // tm,), in_specs=specs,
        out_specs=[dir_spec if nm in _RW_STACKED else row_spec for nm in _RW_OUTS],
        out_shape=[jax.ShapeDtypeStruct(((2,) if nm in _RW_STACKED else ()) + (n_rows, d), out_dt[nm])
                   for nm in _RW_OUTS],
        compiler_params=_cparams(("parallel",)), name="rwkv7_projections")(*args)
    return dict(zip(_RW_OUTS, outs))


TRI_BASE = 8


def _unit_tri_inverse(nmat, cl):
    ii = lax.broadcasted_iota(jnp.int32, (cl, cl), 0)
    jj = lax.broadcasted_iota(jnp.int32, (cl, cl), 1)
    zero = jnp.float32(0.0)
    ident = (ii == jj).astype(F32)
    block_gap = lambda k: lax.shift_right_logical(ii ^ jj, k)
    k0 = TRI_BASE.bit_length() - 1
    n0 = [jnp.where(block_gap(k0) == 0, n, zero) for n in nmat]
    minv = [ident + n for n in n0]
    pw = n0
    for _ in range(k0 - 1):
        pw = [_bdot(x, x) for x in pw]
        minv = [m + _bdot(m, x) for m, x in zip(minv, pw)]
    k = k0
    while (1 << k) < cl:
        off = block_gap(k) == 1
        minv = [m + _bdot(_bdot(m, jnp.where(off, n, zero)), m) for m, n in zip(minv, nmat)]
        k += 1
    return minv


def _rw_scan_kernel(r_ref, lw_ref, k_ref, v_ref, a_ref, b_ref, s0_ref, y_ref, sfin_ref, s_ref, *, n_chunks, hd):
    dirn, c = pl.program_id(1), pl.program_id(2)
    cl, d = r_ref.shape[1], r_ref.shape[2]
    pw_ = 2 * hd
    n_pairs = d // pw_

    @pl.when(c == 0)
    def _():
        s_ref[...] = s0_ref[0, 0]

    incl, strict = _time_masks(dirn, cl)
    lw = lw_ref[0, 0]
    cum = _hdot(incl.astype(F32), lw)
    p_tot = jnp.exp(jnp.sum(lw, axis=0, keepdims=True))
    p_inv = jnp.exp(-cum)
    at = a_ref[0] * jnp.exp(cum - lw)
    rt = r_ref[0] * jnp.exp(cum)
    bt = (b_ref[0, 0] * p_inv).astype(BF16)
    kt = (k_ref[0, 0] * p_inv).astype(BF16)
    v = v_ref[0]
    hd_shift = hd.bit_length() - 1
    head_of = lambda shape, axis: lax.shift_right_logical(lax.broadcasted_iota(jnp.int32, shape, axis), hd_shift)
    lane = head_of((cl, pw_), 1)
    lane2 = head_of((2 * cl, pw_), 1)
    blk = head_of((pw_, pw_), 0) == head_of((pw_, pw_), 1)
    zero = jnp.float32(0.0)
    pairs = range(n_pairs)
    halves = [(pr, hf) for pr in pairs for hf in range(2)]
    sl = lambda x, pr: x[:, pr * pw_:(pr + 1) * pw_]
    s_old = [s_ref[pr] for pr in pairs]
    s_bf = [s.astype(BF16) for s in s_old]
    xs = [jnp.concatenate([sl(at, pr), sl(rt, pr)], axis=0) for pr in pairs]
    xs0 = [_bdot_nt(xs[pr], s_bf[pr]) for pr in pairs]
    xm = [jnp.where(lane2 == hf, xs[pr], zero).astype(BF16) for pr, hf in halves]
    gb = [lax.dot_general(xm[i], sl(bt, pr), NT_DIMS, preferred_element_type=F32) for i, (pr, hf) in enumerate(halves)]
    gk = [lax.dot_general(xm[i], sl(kt, pr), NT_DIMS, preferred_element_type=F32) for i, (pr, hf) in enumerate(halves)]
    minv = _unit_tri_inverse([jnp.where(strict, g[:cl], zero) for g in gb], cl)
    vm = [jnp.where(lane == hf, sl(v, pr), zero).astype(BF16) for pr, hf in halves]
    rhs = [jnp.where(lane == hf, xs0[pr][:cl], zero)
           + jnp.dot(jnp.where(strict, gk[i][:cl], zero).astype(BF16), vm[i], preferred_element_type=F32)
           for i, (pr, hf) in enumerate(halves)]
    u = [_bdot(minv[i], rhs[i]) for i in range(len(halves))]
    u_b = [x.astype(BF16) for x in u]
    yh = [jnp.dot(jnp.where(incl, gb[i][cl:], zero).astype(BF16), u_b[i], preferred_element_type=F32)
          + jnp.dot(jnp.where(incl, gk[i][cl:], zero).astype(BF16), vm[i], preferred_element_type=F32)
          for i in range(len(halves))]
    for pr in pairs:
        y_ref[0, 0, :, pr * pw_:(pr + 1) * pw_] = (xs0[pr][cl:] + yh[2 * pr] + yh[2 * pr + 1]).astype(y_ref.dtype)
    for pr in pairs:
        uv = jnp.concatenate([u[2 * pr] + u[2 * pr + 1], sl(v, pr)], axis=0).astype(BF16)
        bk = jnp.concatenate([sl(bt, pr), sl(kt, pr)], axis=0)
        delta = lax.dot_general(uv, bk, TN_DIMS, preferred_element_type=F32)
        s_ref[pr] = (s_old[pr] + jnp.where(blk, delta, zero)) * sl(p_tot, pr)

    @pl.when(c == n_chunks - 1)
    def _():
        sfin_ref[0, 0] = s_ref[...]


def _rw_scan(q, s0, nb, seq, d):
    hd = RW_HEAD
    pw_ = 2 * hd
    assert pw_ == V7X_LANES and d % pw_ == 0
    n_pairs = d // pw_
    n_chunks = seq // CHUNK
    cidx = lambda dd, c: c + dd * (n_chunks - 1 - 2 * c)
    v3 = lambda a: a.reshape(nb, seq, d)
    v4 = lambda a: a.reshape(2, nb, seq, d)
    tok = pl.BlockSpec((1, CHUNK, d), lambda b, dd, c: (b, cidx(dd, c), 0))
    tok_d = pl.BlockSpec((1, 1, CHUNK, d), lambda b, dd, c: (dd, b, cidx(dd, c), 0))
    st_spec = pl.BlockSpec((1, 1, n_pairs, pw_, pw_), lambda b, dd, c: (b, dd, 0, 0, 0))
    y, sfin = pl.pallas_call(
        functools.partial(_rw_scan_kernel, n_chunks=n_chunks, hd=hd),
        grid=(nb, 2, n_chunks),
        in_specs=[tok, tok_d, tok_d, tok, tok, tok_d, st_spec],
        out_specs=[tok_d, st_spec],
        out_shape=[jax.ShapeDtypeStruct((2, nb, seq, d), BF16),
                   jax.ShapeDtypeStruct((nb, 2, n_pairs, pw_, pw_), F32)],
        scratch_shapes=[pltpu.VMEM((n_pairs, pw_, pw_), F32)],
        compiler_params=_cparams(("parallel", "arbitrary", "arbitrary")), name="rwkv7_scan",
    )(v3(q["r"]), v4(q["lw"]), v4(q["kd"]), v3(q["v"]), v3(q["na"]), v4(q["b"]), s0)
    return y, sfin


def _rw_out(lay, y, q, tail, p, w, *, alpha, name):
    d = w["rw_w_out"].shape[0]
    n_rows = lay.nb * lay.seq
    y2 = y.reshape(2, n_rows, d)
    qq = _expand_mat(V7X_LANES, d // RW_HEAD, RW_HEAD)
    lnx_g = p["rw_lnx_g"].astype(F32).reshape(1, d)
    lnx_b = p["rw_lnx_b"].astype(F32).reshape(1, d)
    yspec = pl.BlockSpec((2, lay.tm, d), lay._ix(lambda i: (0, i, 0)))

    def prologue(y_ref, bonus_ref, g_ref, qq_ref, lg_ref, lb_ref):
        yy = y_ref[0].astype(F32) + y_ref[1].astype(F32)
        inv = 1.0 / RW_HEAD
        mean = _head_sum(yy, qq_ref[...]) * inv
        yc = yy - mean
        var = _head_sum(yc * yc, qq_ref[...]) * inv
        yn = yc * lax.rsqrt(var + RW_GN_EPS) * lg_ref[...] + lb_ref[...]
        return (yn + bonus_ref[...].astype(F32)) * g_ref[...].astype(F32)

    pro = [(y2, yspec), lay.raster(q["bonus"], d), lay.raster(q["g"], d), lay.const(qq), lay.const(lnx_g),
           lay.const(lnx_b)]
    return _out_proj_call(lay, prologue, pro, w["rw_w_out"], None, tail, alpha=alpha, name=name)


def _ssd_scan_kernel(x_ref, b_ref, c_ref, dt_ref, dtb_ref, a_ref, s0_ref, xp_ref, eye_ref, y_ref, sfin_ref, s_ref,
                     *, n_chunks, ng, nr, hp, ns):
    dirn, c = pl.program_id(1), pl.program_id(2)
    cl = x_ref.shape[1]

    @pl.when(c == 0)
    def _():
        s_ref[...] = s0_ref[0, 0]

    incl, _ = _time_masks(dirn, cl)
    tri = incl.astype(F32)
    eye = eye_ref[...]
    xp = xp_ref[...]
    dt = _softplus(dt_ref[0] + dtb_ref[0])
    dta = dt * a_ref[0]
    cum = _hdot(tri, dta)
    tot = jnp.sum(dta, axis=0, keepdims=True)
    cum_t = _transpose_via_eye(cum, eye)
    dt_t = _transpose_via_eye(dt, eye)
    e_in = _split_dot(jnp.exp(cum), xp)
    e_end = _split_dot(jnp.exp(tot - cum) * dt, xp)
    e_tot = _split_dot(jnp.broadcast_to(jnp.exp(tot), (V7X_SUBLANES, tot.shape[1])), xp)[0:1]
    x = x_ref[0]
    xw = (x.astype(F32) * e_end).astype(BF16)
    gw = nr * hp
    pair = 2 * hp
    lane = lax.broadcasted_iota(jnp.int32, (cl, pair), 1)
    neg = jnp.float32(-jnp.inf)
    s_all = [s_ref[g] for g in range(ng)]
    y_parts, s_new = [], []
    for g in range(ng):
        bm = b_ref[0, :, g * ns:(g + 1) * ns]
        cm = c_ref[0, :, g * ns:(g + 1) * ns]
        cb = lax.dot_general(cm, bm, NT_DIMS, preferred_element_type=F32)
        s_g = s_all[g]
        y_inter = jnp.dot(cm, s_g.astype(BF16), preferred_element_type=F32) * e_in[:, g * gw:(g + 1) * gw]
        for rp in range(nr // 2):
            y_pair = jnp.zeros((cl, pair), F32)
            xpair = x[:, g * gw + rp * pair:g * gw + (rp + 1) * pair]
            for half in range(2):
                h = g * nr + rp * 2 + half
                dec = jnp.exp(jnp.where(incl, cum[:, h:h + 1] - cum_t[h:h + 1, :], neg))
                sc = (cb * dec * dt_t[h:h + 1, :]).astype(BF16)
                xh = jnp.where((lane >= half * hp) & (lane < (half + 1) * hp), xpair, jnp.zeros_like(xpair))
                y_pair = y_pair + jnp.dot(sc, xh, preferred_element_type=F32)
            y_parts.append((g * gw + rp * pair, y_pair + y_inter[:, rp * pair:(rp + 1) * pair]))
        s_new.append(s_g * e_tot[:, g * gw:(g + 1) * gw] + lax.dot_general(
            bm, xw[:, g * gw:(g + 1) * gw], TN_DIMS, preferred_element_type=F32))
    for lo, val in y_parts:
        y_ref[0, 0, :, lo:lo + pair] = val.astype(y_ref.dtype)
    for g in range(ng):
        s_ref[g] = s_new[g]

    @pl.when(c == n_chunks - 1)
    def _():
        sfin_ref[0, 0] = s_ref[...]


def _ssd_scan(xbc, dt_raw, s0, nb, seq, p):
    ng, ns, hp = SSD_GROUPS, SSD_STATE, SSD_HEAD
    inner = xbc.shape[1] - 2 * ng * ns
    nh = inner // hp
    nr = nh // ng
    lanes = V7X_LANES
    assert ns == lanes and nh <= lanes and nr % 2 == 0
    n_chunks = seq // CHUNK
    cidx = lambda dd, c: c + dd * (n_chunks - 1 - 2 * c)
    xbc3 = xbc.reshape(nb, seq, xbc.shape[1])
    dt3 = dt_raw.reshape(nb, seq, 2 * lanes)
    a = -jnp.exp(p["ssd_A_log"].astype(F32))
    a_pad = _pad_to(a, 1, lanes).reshape(2, 1, lanes)
    dtb = _pad_to(p["ssd_dt_bias"].astype(F32), 1, lanes).reshape(2, 1, lanes)
    xp = _expand_mat(lanes, nh, hp)
    eye = jnp.eye(lanes, dtype=F32)
    bw = ng * ns
    y, sfin = pl.pallas_call(
        functools.partial(_ssd_scan_kernel, n_chunks=n_chunks, ng=ng, nr=nr, hp=hp, ns=ns),
        grid=(nb, 2, n_chunks),
        in_specs=[pl.BlockSpec((1, CHUNK, inner), lambda b, dd, c: (b, cidx(dd, c), 0)),
                  pl.BlockSpec((1, CHUNK, bw), lambda b, dd, c: (b, cidx(dd, c), inner // bw)),
                  pl.BlockSpec((1, CHUNK, bw), lambda b, dd, c: (b, cidx(dd, c), inner // bw + 1)),
                  pl.BlockSpec((1, CHUNK, lanes), lambda b, dd, c: (b, cidx(dd, c), dd)),
                  pl.BlockSpec((1, 1, lanes), lambda b, dd, c: (dd, 0, 0)),
                  pl.BlockSpec((1, 1, lanes), lambda b, dd, c: (dd, 0, 0)),
                  pl.BlockSpec((1, 1, ng, ns, nr * hp), lambda b, dd, c: (b, dd, 0, 0, 0)),
                  pl.BlockSpec(xp.shape, lambda b, dd, c: (0, 0)),
                  pl.BlockSpec(eye.shape, lambda b, dd, c: (0, 0))],
        out_specs=[pl.BlockSpec((1, 1, CHUNK, inner), lambda b, dd, c: (dd, b, cidx(dd, c), 0)),
                   pl.BlockSpec((1, 1, ng, ns, nr * hp), lambda b, dd, c: (b, dd, 0, 0, 0))],
        out_shape=[jax.ShapeDtypeStruct((2, nb, seq, inner), BF16),
                   jax.ShapeDtypeStruct((nb, 2, ng, ns, nr * hp), F32)],
        scratch_shapes=[pltpu.VMEM((ng, ns, nr * hp), F32)],
        compiler_params=_cparams(("parallel", "arbitrary", "arbitrary")), name="ssd_scan",
    )(xbc3, xbc3, xbc3, dt3, dtb, a_pad, s0, xp, eye)
    return y.reshape(2, nb * seq, inner), sfin


def _ssd_project(u, nb, seq, p, w, colmajor):
    mm = (lambda x, wt, **kw: _matmul_r2c(x, wt, None, nb, **kw)) if colmajor else (
        lambda x, wt, **kw: _matmul(x, wt, None, **kw))
    z = mm(u, w["ssd_w_z"], out_dtype=BF16, name="ssd_in_proj_z")
    xbc_raw = mm(u, w["ssd_w_xbc"], out_dtype=BF16, name="ssd_in_proj_xbc")
    dt_raw = mm(u, w["ssd_w_dt"], out_dtype=F32, name="ssd_in_proj_dt")
    xbc = _conv3(xbc_raw, p["ssd_conv_w"], p["ssd_conv_b"], seq, act=_silu, name="ssd_short_conv")
    return z, xbc, dt_raw


def _ssd_out(lay, y, z, xbc, tail, p, w, *, alpha, name):
    inner = w["ssd_w_out"].shape[0]
    ng = SSD_GROUPS
    gwid = inner // ng
    d_row = jnp.repeat(p["ssd_d_skip"].astype(F32), SSD_HEAD).reshape(1, inner)
    ng_row = p["ssd_norm_g"].astype(F32).reshape(1, inner)

    def prologue(y0_ref, y1_ref, xs_ref, z_ref, d_ref, g_ref):
        flat = lambda ref: ref[...].reshape(-1, ref.shape[-1]).astype(F32)
        yy = flat(xs_ref) * d_ref[...] + flat(y0_ref) + flat(y1_ref)
        yy = yy * _silu(flat(z_ref))
        parts = []
        for g in range(ng):
            yg = yy[:, g * gwid:(g + 1) * gwid]
            ms = jnp.mean(yg * yg, axis=-1, keepdims=True)
            parts.append(yg * lax.rsqrt(ms + 1e-6))
        return jnp.concatenate(parts, axis=1) * g_ref[...]

    pro = [lay.native(y, inner, lead=0), lay.native(y, inner, lead=1), lay.native(xbc, inner),
           lay.native(z, inner), lay.const(d_row), lay.const(ng_row)]
    return _out_proj_call(lay, prologue, pro, w["ssd_w_out"], None, tail, alpha=alpha, name=name)


def _gdn_scan_kernel(q_ref, k_ref, v_ref, a_ref, bta_ref, dtb_ref, al_ref, s0_ref, eye_ref, o_ref, sfin_ref, s_ref,
                     *, n_chunks, hk, hv, dh):
    dirn, c = pl.program_id(1), pl.program_id(2)
    cl = q_ref.shape[1]

    @pl.when(c == 0)
    def _():
        s_ref[...] = s0_ref[0, 0]

    incl, strict = _time_masks(dirn, cl)
    tri = incl.astype(F32)
    eye = eye_ref[...]
    neg = jnp.float32(-jnp.inf)
    g = al_ref[0] * _softplus(a_ref[0] + dtb_ref[0])
    beta = jax.nn.sigmoid(bta_ref[0])
    gn = _hdot(tri, g)
    gtot = jnp.sum(g, axis=0, keepdims=True)
    gn_t = _transpose_via_eye(gn, eye)
    e_in = jnp.exp(gn)
    e_end = jnp.exp(gtot - gn)
    e_tot = jnp.exp(gtot)
    rep = hv // hk
    heads = range(hv)
    s_old = [s_ref[h] for h in heads]
    s_bf = [s.astype(BF16) for s in s_old]
    qn, kn, kk, qk = [], [], [], []
    for hq in range(hk):
        qh = q_ref[0, :, hq * dh:(hq + 1) * dh].astype(F32)
        kh = k_ref[0, :, hq * dh:(hq + 1) * dh].astype(F32)
        qn.append(qh * lax.rsqrt(jnp.sum(qh * qh, axis=-1, keepdims=True) + 1e-6) * (dh ** -0.5))
        kn.append(kh * lax.rsqrt(jnp.sum(kh * kh, axis=-1, keepdims=True) + 1e-6))
        gram = _bdot_nt(jnp.concatenate([kn[hq], qn[hq]], axis=0), kn[hq])
        kk.append(gram[:cl])
        qk.append(gram[cl:])
    bh = [beta[:, h:h + 1] for h in heads]
    diff = [gn[:, h:h + 1] - gn_t[h:h + 1, :] for h in heads]
    nmat = [-(kk[h // rep] * bh[h] * jnp.exp(jnp.where(strict, diff[h], neg))) for h in heads]
    attn = [(qk[h // rep] * jnp.exp(jnp.where(incl, diff[h], neg))).astype(BF16) for h in heads]
    minv = _unit_tri_inverse(nmat, cl)
    rhs =[jnp.concatenate([v_ref[0, :, h * dh:(h + 1) * dh].astype(F32) * bh[h],
                            kn[h // rep] * (bh[h] * e_in[:, h:h + 1])], axis=1) for h in heads]
    sol = [_bdot(minv[h], rhs[h]) for h in heads]
    u_b = [(sol[h][:, :dh] - _bdot(sol[h][:, dh:], s_bf[h])).astype(BF16) for h in heads]
    out = [_bdot(qn[h // rep] * e_in[:, h:h + 1], s_bf[h]) + jnp.dot(attn[h], u_b[h], preferred_element_type=F32)
           for h in heads]
    s_new = [s_old[h] * e_tot[:, h:h + 1]
             + lax.dot_general((kn[h // rep] * e_end[:, h:h + 1]).astype(BF16), u_b[h], TN_DIMS,
                               preferred_element_type=F32) for h in heads]
    for h in heads:
        o_ref[0, 0, :, h * dh:(h + 1) * dh] = out[h].astype(o_ref.dtype)
    for h in heads:
        s_ref[h] = s_new[h]

    @pl.when(c == n_chunks - 1)
    def _():
        sfin_ref[0, 0] = s_ref[...]


def _gdn_scan(qkv, ab, s0, nb, seq, p):
    dh = GDN_HEAD
    lanes = V7X_LANES
    hv = p["gdn_dt_bias"].shape[1]
    vw = hv * dh
    qk = (qkv.shape[1] - vw) // 2
    hk = qk // dh
    assert dh == lanes and hv <= lanes
    n_chunks = seq // CHUNK
    cidx = lambda dd, c: c + dd * (n_chunks - 1 - 2 * c)
    qkv3 = qkv.reshape(nb, seq, qkv.shape[1])
    ab3 = ab.reshape(nb, seq, 4 * lanes)
    dtb = _pad_to(p["gdn_dt_bias"].astype(F32), 1, lanes).reshape(2, 1, lanes)
    al = _pad_to(-jnp.exp(p["gdn_A_log"].astype(F32)), 1, lanes).reshape(2, 1, lanes)
    eye = jnp.eye(lanes, dtype=F32)
    o, sfin = pl.pallas_call(
        functools.partial(_gdn_scan_kernel, n_chunks=n_chunks, hk=hk, hv=hv, dh=dh),
        grid=(nb, 2, n_chunks),
        in_specs=[pl.BlockSpec((1, CHUNK, qk), lambda b, dd, c: (b, cidx(dd, c), 0)),
                  pl.BlockSpec((1, CHUNK, qk), lambda b, dd, c: (b, cidx(dd, c), 1)),
                  pl.BlockSpec((1, CHUNK, vw), lambda b, dd, c: (b, cidx(dd, c), 2 * qk // vw)),
                  pl.BlockSpec((1, CHUNK, lanes), lambda b, dd, c: (b, cidx(dd, c), 2 * dd)),
                  pl.BlockSpec((1, CHUNK, lanes), lambda b, dd, c: (b, cidx(dd, c), 2 * dd + 1)),
                  pl.BlockSpec((1, 1, lanes), lambda b, dd, c: (dd, 0, 0)),
                  pl.BlockSpec((1, 1, lanes), lambda b, dd, c: (dd, 0, 0)),
                  pl.BlockSpec((1, 1, hv, dh, dh), lambda b, dd, c: (b, dd, 0, 0, 0)),
                  pl.BlockSpec(eye.shape, lambda b, dd, c: (0, 0))],
        out_specs=[pl.BlockSpec((1, 1, CHUNK, vw), lambda b, dd, c: (dd, b, cidx(dd, c), 0)),
                   pl.BlockSpec((1, 1, hv, dh, dh), lambda b, dd, c: (b, dd, 0, 0, 0))],
        out_shape=[jax.ShapeDtypeStruct((2, nb, seq, vw), BF16),
                   jax.ShapeDtypeStruct((nb, 2, hv, dh, dh), F32)],
        scratch_shapes=[pltpu.VMEM((hv, dh, dh), F32)],
        compiler_params=_cparams(("parallel", "arbitrary", "arbitrary")), name="gdn_scan",
    )(qkv3, qkv3, qkv3, ab3, ab3, dtb, al, s0, eye)
    return o.reshape(2, nb * seq, vw), sfin


def _gdn_project(u, nb, seq, p, w, colmajor):
    mm = (lambda x, wt, **kw: _matmul_r2c(x, wt, None, nb, **kw)) if colmajor else (
        lambda x, wt, **kw: _matmul(x, wt, None, **kw))
    qkv_raw = mm(u, w["gdn_w_qkv"], out_dtype=BF16, name="gdn_in_proj_qkv")
    z = mm(u, w["gdn_w_z"], out_dtype=BF16, name="gdn_in_proj_z")
    ab = mm(u, w["gdn_w_ab"], out_dtype=F32, name="gdn_in_proj_ab")
    qkv = _conv3(qkv_raw, p["gdn_conv_w"], None, seq, act=_silu, name="gdn_short_conv")
    return qkv, z, ab


def _gdn_out(lay, o, z, tail, p, w, *, alpha, name):
    vw = w["gdn_w_out"].shape[0]
    dh = GDN_HEAD
    hv = vw // dh
    g_row = jnp.tile(p["gdn_norm_g"].astype(F32), hv).reshape(1, vw)

    def prologue(o0_ref, o1_ref, z_ref, g_ref):
        flat = lambda ref: ref[...].reshape(-1, ref.shape[-1]).astype(F32)
        oo = flat(o0_ref) + flat(o1_ref)
        parts = []
        for h in range(hv):
            oh = oo[:, h * dh:(h + 1) * dh]
            ms = jnp.mean(oh * oh, axis=-1, keepdims=True)
            parts.append(oh * lax.rsqrt(ms + 1e-6))
        return jnp.concatenate(parts, axis=1) * g_ref[...] * _silu(flat(z_ref))

    pro = [lay.native(o, vw, lead=0), lay.native(o, vw, lead=1), lay.native(z, vw), lay.const(g_row)]
    return _out_proj_call(lay, prologue, pro, w["gdn_w_out"], None, tail, alpha=alpha, name=name)


def _prep_weights(p):
    lanes = V7X_LANES
    bf = lambda a: a.astype(BF16)
    d = p["hy_w_out"].shape[0]
    w = {k: bf(p[k]) for k in ("hy_w_in", "hy_w_out", "rw_w_rkv", "rw_w_out", "ssd_w_out", "gdn_w_out",
                               "ffn_w13", "ffn_w2", "moe_w13", "moe_w2", "mod_w")}
    lora = lambda a, ax: bf(_pad_to(a, ax, lanes * ((a.shape[ax] + lanes - 1) // lanes)))
    w["rw_w1"], w["rw_w2"] = lora(p["rw_w1"], 2), lora(p["rw_w2"], 1)
    w["rw_a1"], w["rw_a2"] = lora(p["rw_a1"], 2), lora(p["rw_a2"], 1)
    w["rw_g1"], w["rw_g2"] = lora(p["rw_g1"], 1), lora(p["rw_g2"], 0)
    inner = p["ssd_w_out"].shape[0]
    nbc = 2 * SSD_GROUPS * SSD_STATE
    nh = inner // SSD_HEAD
    ws = p["ssd_w_in"]
    w["ssd_w_z"] = bf(ws[:, :inner])
    w["ssd_w_xbc"] = bf(ws[:, inner:2 * inner + nbc])
    wdt = ws[:, 2 * inner + nbc:].reshape(d, 2, nh)
    w["ssd_w_dt"] = bf(_pad_to(wdt, 2, lanes).reshape(d, 2 * lanes))
    vw = p["gdn_w_out"].shape[0]
    hv = vw // GDN_HEAD
    qkvw = p["gdn_conv_w"].shape[1]
    wg = p["gdn_w_in"]
    w["gdn_w_qkv"] = bf(wg[:, :qkvw])
    w["gdn_w_z"] = bf(wg[:, qkvw:qkvw + vw])
    wab = wg[:, qkvw + vw:].reshape(d, 2, 2, hv)
    wab = jnp.transpose(wab, (0, 2, 1, 3))
    w["gdn_w_ab"] = bf(_pad_to(wab, 3, lanes).reshape(d, 4 * lanes))
    w["moe_router"] = _pad_to(p["moe_router"].astype(F32), 2, lanes)
    return w


_PARAM_NAMES = (
    "mod_w mod_b ln_g ln_b hy_w_in hy_b_in hy_conv_w hy_conv_b hy_f_w1 hy_f_b1 hy_f_w2 hy_f_b2 hy_f_freq "
    "hy_f_w3 hy_decay hy_skip hy_w_out hy_b_out rw_mu rw_w_rkv rw_w0 rw_w1 rw_w2 rw_a0 rw_a1 rw_a2 rw_g1 rw_g2 "
    "rw_k_k rw_k_a rw_r_k rw_lnx_g rw_lnx_b rw_w_out ssd_w_in ssd_conv_w ssd_conv_b ssd_dt_bias ssd_A_log "
    "ssd_d_skip ssd_norm_g ssd_w_out gdn_w_in gdn_conv_w gdn_dt_bias gdn_A_log gdn_norm_g gdn_w_out ffn_w13 "
    "ffn_w2 moe_router moe_w13 moe_w2").split()


def _forward(x, c, ctx, c_ctx, p):
    nb, seq, d = x.shape
    lc = ctx.shape[1]
    depth = p["mod_w"].shape[0]
    alpha = (2 * depth) ** 0.25
    n_exp = p["moe_router"].shape[2]
    assert seq == GRID_W * GRID_W and seq % CHUNK == 0 and lc % CHUNK == 0
    w = _prep_weights(p)

    cc = _pad_to(jnp.concatenate([c, c_ctx[None, :]], axis=0).astype(F32), 0, V7X_BF16_ROWS)
    mods = [
        _matmul(cc, w["mod_w"][i], p["mod_b"][i], pre=_silu, out_dtype=F32, name="adaln_modulation")
        .reshape(cc.shape[0], 6, d) for i in range(depth)]

    def chunk(i, k, stream):
        m = mods[i][:, k]
        return m[:nb, None, :] if stream == "x" else m[nb:nb + 1, None, :]

    lay = {"x": _RowLayout(nb, seq, True), "c": _RowLayout(nb, lc, False),
           "xcol": _RowLayout(nb, seq, True, mode="cols")}
    seqs = {"x": seq, "c": lc}
    h = {"x": x.reshape(nb * seq, d).astype(F32), "c": ctx.reshape(nb * lc, d).astype(F32)}

    u = {s: _modulate(lay[s], h[s], chunk(0, 1, s), chunk(0, 0, s)) for s in ("x", "c")}

    for i in range(depth):
        last = i == depth - 1
        kind = i % 4
        moe = i % 2 == 1
        streams = ("x",) if last else ("x", "c")

        def tail1(s):
            t = dict(h=h[s], gate=chunk(i, 2, s), ln_g=p["ln_g"][i, 0], ln_b=p["ln_b"][i, 0],
                     scale=chunk(i, 4, s), shift=chunk(i, 3, s))
            if moe:
                t.update(router=w["moe_router"][i // 2], n_exp=n_exp)
            return t

        res = {}
        if kind == 0:
            for s in streams:
                z = _hyena_mixer(u[s], nb, seqs[s], p, w)
                res[s] = _hyena_out(lay[s], z, tail1(s), p, w, nb, seqs[s], alpha=alpha, name=f"hyena_out_{s}")
        elif kind == 1:
            qc = _rw_project(u["c"], lc, p, w)
            qx = _rw_project(u["x"], seq, p, w)
            s0 = jnp.zeros((nb, 2, d // (2 * RW_HEAD), 2 * RW_HEAD, 2 * RW_HEAD), F32)
            yc, s_c = _rw_scan(qc, s0, nb, lc, d)
            yx, _ = _rw_scan(qx, s_c, nb, seq, d)
            ys, qs = {"x": yx, "c": yc}, {"x": qx, "c": qc}
            for s in streams:
                res[s] = _rw_out(lay[s], ys[s], qs[s], tail1(s), p, w, alpha=alpha, name=f"rwkv7_out_{s}")
        elif kind == 2:
            zc, xbc_c, dt_c = _ssd_project(u["c"], nb, lc, p, w, False)
            zx, xbc_x, dt_x = _ssd_project(u["x"], nb, seq, p, w, True)
            inner = w["ssd_w_out"].shape[0]
            s0 = jnp.zeros((nb, 2, SSD_GROUPS, SSD_STATE, inner // SSD_GROUPS), F32)
            yc, s_c = _ssd_scan(xbc_c, dt_c, s0, nb, lc, p)
            yx, _ = _ssd_scan(xbc_x, dt_x, s_c, nb, seq, p)
            res["x"] = _ssd_out(lay["xcol"], yx, zx, xbc_x, tail1("x"), p, w, alpha=alpha, name="ssd_out_x")
            if not last:
                res["c"] = _ssd_out(lay["c"], yc, zc, xbc_c, tail1("c"), p, w, alpha=alpha, name="ssd_out_c")
        else:
            qkv_c, zc, ab_c = _gdn_project(u["c"], nb, lc, p, w, False)
            qkv_x, zx, ab_x = _gdn_project(u["x"], nb, seq, p, w, True)
            vw = w["gdn_w_out"].shape[0]
            s0 = jnp.zeros((nb, 2, vw // GDN_HEAD, GDN_HEAD, GDN_HEAD), F32)
            oc, s_c = _gdn_scan(qkv_c, ab_c, s0, nb, lc, p)
            ox, _ = _gdn_scan(qkv_x, ab_x, s_c, nb, seq, p)
            res["x"] = _gdn_out(lay["xcol"], ox, zx, tail1("x"), p, w, alpha=alpha, name="gdn_out_x")
            if not last:
                res["c"] = _gdn_out(lay["c"], oc, zc, tail1("c"), p, w, alpha=alpha, name="gdn_out_c")

        for s in streams:
            t = dict(h=res[s]["h"], gate=chunk(i, 5, s), ln_g=p["ln_g"][i, 1], ln_b=p["ln_b"][i, 1])
            if not last:
                t.update(scale=chunk(i + 1, 1, s), shift=chunk(i + 1, 0, s))
            if moe:
                out = _moe_call(lay[s], res[s]["u"], w["moe_w13"][i // 2], w["moe_w2"][i // 2], res[s]["comb"], t,
                                alpha=alpha, name=f"moe_ffn_{s}")
            else:
                out = _ffn_call(lay[s], res[s]["u"], w["ffn_w13"][i // 2], w["ffn_w2"][i // 2], t,
                                alpha=alpha, name=f"dense_ffn_{s}")
            h[s] = out["h"]
            if not last:
                u[s] = out["u"]
    return h["x"].reshape(nb, seq, d).astype(x.dtype)


def kernel(x, c, ctx, c_ctx, mod_w, mod_b, ln_g, ln_b, hy_w_in, hy_b_in, hy_conv_w, hy_conv_b, hy_f_w1, hy_f_b1, hy_f_w2, hy_f_b2, hy_f_freq, hy_f_w3, hy_decay, hy_skip, hy_w_out, hy_b_out, rw_mu, rw_w_rkv, rw_w0, rw_w1, rw_w2, rw_a0, rw_a1, rw_a2, rw_g1, rw_g2, rw_k_k, rw_k_a, rw_r_k, rw_lnx_g, rw_lnx_b, rw_w_out, ssd_w_in, ssd_conv_w, ssd_conv_b, ssd_dt_bias, ssd_A_log, ssd_d_skip, ssd_norm_g, ssd_w_out, gdn_w_in, gdn_conv_w, gdn_dt_bias, gdn_A_log, gdn_norm_g, gdn_w_out, ffn_w13, ffn_w2, moe_router, moe_w13, moe_w2):
    vals = (mod_w, mod_b, ln_g, ln_b, hy_w_in, hy_b_in, hy_conv_w, hy_conv_b, hy_f_w1, hy_f_b1, hy_f_w2, hy_f_b2,
            hy_f_freq, hy_f_w3, hy_decay, hy_skip, hy_w_out, hy_b_out, rw_mu, rw_w_rkv, rw_w0, rw_w1, rw_w2, rw_a0,
            rw_a1, rw_a2, rw_g1, rw_g2, rw_k_k, rw_k_a, rw_r_k, rw_lnx_g, rw_lnx_b, rw_w_out, ssd_w_in, ssd_conv_w,
            ssd_conv_b, ssd_dt_bias, ssd_A_log, ssd_d_skip, ssd_norm_g, ssd_w_out, gdn_w_in, gdn_conv_w,
            gdn_dt_bias, gdn_A_log, gdn_norm_g, gdn_w_out, ffn_w13, ffn_w2, moe_router, moe_w13, moe_w2)
    return _forward(x, c, ctx, c_ctx, dict(zip(_PARAM_NAMES, vals)))
```

```python
import functools
import math

import jax
import jax.numpy as jnp
import numpy as np
from jax import lax
from jax.experimental import pallas as pl
from jax.experimental.pallas import tpu as pltpu

F32 = jnp.float32
BF16 = jnp.bfloat16
HIGHEST = lax.Precision.HIGHEST

GRID_W = 64
LN_EPS = 1e-5
HY_BANDS = 8
RW_HEAD = 64
RW_GN_EPS = 64e-5
SSD_HEAD = 64
SSD_STATE = 128
SSD_GROUPS = 4
GDN_HEAD = 128
CHUNK = 64
TOP_K = 2

V7X_LANES = 128
V7X_SUBLANES = 8
V7X_BF16_ROWS = 16
V7X_VMEM_BUDGET = 56 * 1024 * 1024

NT_DIMS = (((1,), (1,)), ((), ()))
TN_DIMS = (((0,), (0,)), ((), ()))


def _tile(n, pref, mult):
    best = None
    t = mult
    while t <= min(n, pref):
        if n % t == 0:
            best = t
        t += mult
    return best if best is not None else n


def _pad_to(a, axis, size):
    pad = size - a.shape[axis]
    if pad == 0:
        return a
    cfg = [(0, 0)] * a.ndim
    cfg[axis] = (0, pad)
    return jnp.pad(a, cfg)


def _cparams(sem):
    return pltpu.CompilerParams(dimension_semantics=sem, vmem_limit_bytes=V7X_VMEM_BUDGET)


def _silu(x):
    return x * jax.nn.sigmoid(x)


def _softplus(x):
    return jnp.maximum(x, 0.0) + jnp.log1p(jnp.exp(-jnp.abs(x)))


def _bdot(a, b):
    return jnp.dot(a.astype(BF16), b.astype(BF16), preferred_element_type=F32)


def _bdot_nt(a, b):
    return lax.dot_general(a.astype(BF16), b.astype(BF16), NT_DIMS, preferred_element_type=F32)


def _hdot(a, b):
    return jnp.dot(a, b, precision=HIGHEST, preferred_element_type=F32)


def _split_dot(x, m):
    hi = x.astype(BF16)
    lo = (x - hi.astype(F32)).astype(BF16)
    return jnp.dot(hi, m, preferred_element_type=F32) + jnp.dot(lo, m, preferred_element_type=F32)


def _transpose_via_eye(x, eye):
    return lax.dot_general(eye, x, NT_DIMS, precision=HIGHEST, preferred_element_type=F32)


def _head_sum(x, expand):
    hi = x.astype(BF16)
    lo = (x - hi.astype(F32)).astype(BF16)
    s = (lax.dot_general(hi, expand, NT_DIMS, preferred_element_type=F32)
         + lax.dot_general(lo, expand, NT_DIMS, preferred_element_type=F32))
    return _split_dot(s, expand)


def _expand_mat(n_in_pad, n_heads, width):
    m = np.zeros((n_in_pad, n_heads * width), np.float32)
    for h in range(n_heads):
        m[h, h * width:(h + 1) * width] = 1.0
    return jnp.asarray(m, BF16)


def _perm_mat(ib, jb):
    n = ib * jb
    q = np.arange(n)
    p = (q % ib) * jb + q // ib
    m = np.zeros((n, n), np.float32)
    m[q, p] = 1.0
    return jnp.asarray(m, BF16)


def _time_masks(d, c):
    ii = lax.broadcasted_iota(jnp.int32, (c, c), 0)
    jj = lax.broadcasted_iota(jnp.int32, (c, c), 1)
    lag = (ii - jj) * (1 - 2 * d)
    return lag >= 0, lag > 0


def _mm_kernel(*refs, nk, pre, act, has_bias, has_perm):
    it = iter(refs)
    x_ref, w_ref = next(it), next(it)
    b_ref = next(it) if has_bias else None
    p_ref = next(it) if has_perm else None
    o_ref = next(it)
    acc_ref = next(it) if nk > 1 else None
    xp_ref = next(it) if has_perm else None

    def load_x():
        x = x_ref[...]
        x = x.reshape(-1, x.shape[-1])
        if pre is not None:
            x = pre(x.astype(F32))
        return x.astype(BF16)

    if has_perm:
        @pl.when(pl.program_id(3) == 0)
        def _():
            xp_ref[...] = jnp.dot(p_ref[...], load_x(), preferred_element_type=F32).astype(BF16)

        x = xp_ref[...]
    else:
        x = load_x()
    part = jnp.dot(x, w_ref[...], preferred_element_type=F32)

    def finish(r):
        if has_bias:
            r = r + b_ref[...]
        if act is not None:
            r = act(r)
        o_ref[...] = r.astype(o_ref.dtype).reshape(o_ref.shape)

    if nk == 1:
        finish(part)
    else:
        k = pl.program_id(2)

        @pl.when(k == 0)
        def _():
            acc_ref[...] = jnp.zeros_like(acc_ref)

        acc_ref[...] += part

        @pl.when(k == nk - 1)
        def _():
            finish(acc_ref[...])


def _matmul(x, w, bias=None, *, out_dtype=F32, act=None, pre=None, tm=512, tn=2048, tk=None, name="matmul"):
    m, kdim = x.shape
    n = w.shape[1]
    tm = _tile(m, tm, V7X_BF16_ROWS)
    tn = _tile(n, tn, V7X_LANES)
    tk = kdim if tk is None else _tile(kdim, tk, V7X_LANES)
    nk = kdim // tk
    in_specs = [pl.BlockSpec((tm, tk), lambda i, j, k: (i, k)),
                pl.BlockSpec((tk, tn), lambda i, j, k: (k, j))]
    args = [x, w]
    if bias is not None:
        in_specs.append(pl.BlockSpec((1, tn), lambda i, j, k: (0, j)))
        args.append(bias.reshape(1, n).astype(F32))
    return pl.pallas_call(
        functools.partial(_mm_kernel, nk=nk, pre=pre, act=act, has_bias=bias is not None, has_perm=False),
        grid=(m // tm, n // tn, nk),
        in_specs=in_specs,
        out_specs=pl.BlockSpec((tm, tn), lambda i, j, k: (i, j)),
        out_shape=jax.ShapeDtypeStruct((m, n), out_dtype),
        scratch_shapes=[pltpu.VMEM((tm, tn), F32)] if nk > 1 else [],
        compiler_params=_cparams(("parallel", "parallel", "arbitrary")),
        name=name,
    )(*args)


def _col_tiles(gw):
    ib = min(V7X_BF16_ROWS, gw)
    jb = min(32, gw)
    return ib, jb


def _matmul_r2c(u, w, bias, nb, *, out_dtype, tn=2048, name="matmul_r2c"):
    gw = GRID_W
    kdim, n = w.shape
    ib, jb = _col_tiles(gw)
    tn = _tile(n, tn, V7X_LANES)
    u4 = u.reshape(nb, gw, gw, kdim)
    in_specs = [pl.BlockSpec((1, ib, jb, kdim), lambda b, i, j, c: (b, i, j, 0)),
                pl.BlockSpec((kdim, tn), lambda b, i, j, c: (0, c))]
    args = [u4, w]
    if bias is not None:
        in_specs.append(pl.BlockSpec((1, tn), lambda b, i, j, c: (0, c)))
        args.append(bias.reshape(1, n).astype(F32))
    in_specs.append(pl.BlockSpec((ib * jb, ib * jb), lambda b, i, j, c: (0, 0)))
    args.append(_perm_mat(ib, jb))
    out = pl.pallas_call(
        functools.partial(_mm_kernel, nk=1, pre=None, act=None, has_bias=bias is not None, has_perm=True),
        grid=(nb, gw // ib, gw // jb, n // tn),
        in_specs=in_specs,
        out_specs=pl.BlockSpec((1, jb, ib, tn), lambda b, i, j, c: (b, j, i, c)),
        out_shape=jax.ShapeDtypeStruct((nb, gw, gw, n), out_dtype),
        scratch_shapes=[pltpu.VMEM((ib * jb, kdim), BF16)],
        compiler_params=_cparams(("parallel", "parallel", "parallel", "arbitrary")),
        name=name,
    )(*args)
    return out.reshape(nb * gw * gw, n)


def _ln_epilogue(h, y, gate, ln_g, ln_b, alpha):
    pre = alpha * h + gate * y
    mu = jnp.mean(pre, axis=-1, keepdims=True)
    xc = pre - mu
    var = jnp.mean(xc * xc, axis=-1, keepdims=True)
    return xc * lax.rsqrt(var + LN_EPS) * ln_g + ln_b


def _route(logits, n_exp):
    lane = lax.broadcasted_iota(jnp.int32, logits.shape, 1)
    neg = jnp.float32(-jnp.inf)
    lg = jnp.where(lane < n_exp, logits, neg)
    big = jnp.int32(logits.shape[1])
    m1 = jnp.max(lg, axis=-1, keepdims=True)
    i1 = jnp.min(jnp.where(lg == m1, lane, big), axis=-1, keepdims=True)
    lg2 = jnp.where(lane == i1, neg, lg)
    m2 = jnp.max(lg2, axis=-1, keepdims=True)
    i2 = jnp.min(jnp.where(lg2 == m2, lane, big), axis=-1, keepdims=True)
    e2 = jnp.exp(m2 - m1)
    den = 1.0 + e2
    return jnp.where(lane == i1, 1.0 / den, jnp.where(lane == i2, e2 / den, 0.0))


def _finish_rows(refs, y, *, alpha, emit_u, n_exp):
    h_ref, gate_ref, lng_ref, lnb_ref = refs["h"], refs["gate"], refs["ln_g"], refs["ln_b"]
    h = h_ref[...].reshape(y.shape)
    hn = _ln_epilogue(h, y, gate_ref[0], lng_ref[...], lnb_ref[...], alpha)
    refs["h_out"][...] = hn.reshape(refs["h_out"].shape)
    if emit_u:
        u = hn * (1.0 + refs["scale"][0]) + refs["shift"][0]
        refs["u_out"][...] = u.astype(BF16).reshape(refs["u_out"].shape)
        if n_exp:
            logits = _hdot(u, refs["router"][...])
            refs["comb_out"][...] = _route(logits, n_exp).reshape(refs["comb_out"].shape)


class _RowLayout:
    def __init__(self, nb, seq, per_batch_mod, mode="rows", tm=512):
        self.nb, self.seq, self.mode, self.per_batch_mod = nb, seq, mode, per_batch_mod
        if mode == "rows":
            self.tm = _tile(seq, tm, V7X_BF16_ROWS)
            self.tps = seq // self.tm
            self.grid = (nb * self.tps,)
            self.rows = self.tm
        else:
            self.ib, self.jb = _col_tiles(GRID_W)
            self.grid = (nb, GRID_W // self.ib, GRID_W // self.jb)
            self.rows = self.ib * self.jb
        self.ngrid = len(self.grid)

    def sem(self, extra=()):
        return ("parallel",) * self.ngrid + tuple(extra)

    def _ix(self, fn):
        n = self.ngrid
        return lambda *g: fn(*g[:n])

    def raster(self, arr, c, cblock=0, lead=None):
        pre_shape = () if lead is None else (arr.shape[0],)
        pre_blk = () if lead is None else (1,)
        pre_ix = () if lead is None else (lead,)
        if self.mode == "rows":
            return arr, pl.BlockSpec(pre_blk + (self.tm, c), self._ix(lambda i: pre_ix + (i, cblock)))
        a4 = arr.reshape(pre_shape + (self.nb, GRID_W, GRID_W, arr.shape[-1]))
        return a4, pl.BlockSpec(pre_blk + (1, self.ib, self.jb, c),
                                self._ix(lambda b, i, j: pre_ix + (b, i, j, cblock)))

    def colmajor(self, arr, c, cblock=0, lead=None):
        assert self.mode == "cols"
        pre_shape = () if lead is None else (arr.shape[0],)
        pre_blk = () if lead is None else (1,)
        pre_ix = () if lead is None else (lead,)
        a4 = arr.reshape(pre_shape + (self.nb, GRID_W, GRID_W, arr.shape[-1]))
        return a4, pl.BlockSpec(pre_blk + (1, self.jb, self.ib, c),
                                self._ix(lambda b, i, j: pre_ix + (b, j, i, cblock)))

    def native(self, arr, c, cblock=0, lead=None):
        if self.mode == "cols":
            return self.colmajor(arr, c, cblock, lead)
        return self.raster(arr, c, cblock, lead)

    def mod(self, arr):
        d = arr.shape[-1]
        if not self.per_batch_mod:
            return arr, pl.BlockSpec((1, 1, d), self._ix(lambda *g: (0, 0, 0)))
        if self.mode == "rows":
            tps = self.tps
            return arr, pl.BlockSpec((1, 1, d), self._ix(lambda i: (i // tps, 0, 0)))
        return arr, pl.BlockSpec((1, 1, d), self._ix(lambda b, i, j: (b, 0, 0)))

    def const(self, arr):
        nd = arr.ndim
        return arr, pl.BlockSpec(arr.shape, self._ix(lambda *g: (0,) * nd))

    def out_raster(self, n_rows, c, dtype):
        if self.mode == "rows":
            return (jax.ShapeDtypeStruct((n_rows, c), dtype),
                    pl.BlockSpec((self.tm, c), self._ix(lambda i: (i, 0))))
        return (jax.ShapeDtypeStruct((self.nb, GRID_W, GRID_W, c), dtype),
                pl.BlockSpec((1, self.ib, self.jb, c), self._ix(lambda b, i, j: (b, i, j, 0))))


def _out_proj_call(lay, prologue, pro_inputs, w_out, bias, tail, *, alpha, name):
    d = w_out.shape[1]
    n_rows = lay.nb * lay.seq
    emit_u = "scale" in tail
    n_exp = tail.get("n_exp", 0)
    names, args, specs = [], [], []

    def add(nm, pair):
        names.append(nm)
        args.append(pair[0])
        specs.append(pair[1])

    for k, pair in enumerate(pro_inputs):
        add(f"p{k}", pair)
    add("w", lay.const(w_out))
    if bias is not None:
        add("bias", lay.const(bias.reshape(1, d).astype(F32)))
    if lay.mode == "cols":
        add("perm", lay.const(_perm_mat(lay.ib, lay.jb).T))
    add("h", lay.raster(tail["h"], d))
    add("gate", lay.mod(tail["gate"]))
    add("ln_g", lay.const(tail["ln_g"].reshape(1, d)))
    add("ln_b", lay.const(tail["ln_b"].reshape(1, d)))
    if emit_u:
        add("scale", lay.mod(tail["scale"]))
        add("shift", lay.mod(tail["shift"]))
    if n_exp:
        add("router", lay.const(tail["router"]))
    out_names, out_shapes, out_specs = [], [], []

    def add_out(nm, pair):
        out_names.append(nm)
        out_shapes.append(pair[0])
        out_specs.append(pair[1])

    add_out("h_out", lay.out_raster(n_rows, d, F32))
    if emit_u:
        add_out("u_out", lay.out_raster(n_rows, d, BF16))
    if n_exp:
        add_out("comb_out", lay.out_raster(n_rows, V7X_LANES, F32))
    n_pro = len(pro_inputs)
    n_in = len(names)

    def kernel(*refs):
        r = dict(zip(names + out_names, refs))
        z = prologue(*[refs[k] for k in range(n_pro)])
        z = z.astype(BF16)
        if lay.mode == "cols":
            z = jnp.dot(r["perm"][...], z, preferred_element_type=F32).astype(BF16)
        y = jnp.dot(z, r["w"][...], preferred_element_type=F32)
        if bias is not None:
            y = y + r["bias"][...]
        _finish_rows(r, y, alpha=alpha, emit_u=emit_u, n_exp=n_exp)

    outs = pl.pallas_call(
        kernel, grid=lay.grid, in_specs=specs, out_specs=out_specs, out_shape=out_shapes,
        compiler_params=_cparams(lay.sem()), name=name)(*args)
    outs = [o.reshape(n_rows, o.shape[-1]) for o in outs]
    res = {"h": outs[0]}
    if emit_u:
        res["u"] = outs[1]
    if n_exp:
        res["comb"] = outs[2]
    return res


def _modulate_kernel(h_ref, scale_ref, shift_ref, u_ref):
    u_ref[...] = (h_ref[...] * (1.0 + scale_ref[0]) + shift_ref[0]).astype(u_ref.dtype)


def _modulate(lay, h, scale, shift):
    n_rows, d = h.shape
    pairs = [lay.raster(h, d), lay.mod(scale), lay.mod(shift)]
    out = lay.out_raster(n_rows, d, BF16)
    return pl.pallas_call(
        _modulate_kernel, grid=lay.grid, in_specs=[pr[1] for pr in pairs], out_specs=out[1], out_shape=out[0],
        compiler_params=_cparams(lay.sem()), name="adaln_modulate")(*[pr[0] for pr in pairs])


def _ffn_kernel(*refs, names, n_f, alpha, emit_u):
    r = dict(zip(names, refs))
    f = pl.program_id(1)

    @pl.when(f == 0)
    def _():
        r["acc"][...] = jnp.zeros_like(r["acc"])

    u = r["u"][...]
    gate = jnp.dot(u, r["w1"][...], preferred_element_type=F32)
    up = jnp.dot(u, r["w3"][...], preferred_element_type=F32)
    hid = _silu(gate) * up
    r["acc"][...] += jnp.dot(hid.astype(BF16), r["w2"][...], preferred_element_type=F32)

    @pl.when(f == n_f - 1)
    def _():
        _finish_rows(r, r["acc"][...], alpha=alpha, emit_u=emit_u, n_exp=0)


def _ffn_call(lay, u, w13, w2, tail, *, alpha, name):
    d, f2 = w13.shape
    fdim = f2 // 2
    tf = _tile(fdim, 1536, V7X_LANES)
    n_f = fdim // tf
    n_rows = lay.nb * lay.seq
    emit_u = "scale" in tail
    names, args, specs = [], [], []

    def add(nm, pair):
        names.append(nm)
        args.append(pair[0])
        specs.append(pair[1])

    add("u", lay.raster(u, d))
    add("w1", (w13, pl.BlockSpec((d, tf), lambda i, f: (0, f))))
    add("w3", (w13, pl.BlockSpec((d, tf), lambda i, f: (0, n_f + f))))
    add("w2", (w2, pl.BlockSpec((tf, d), lambda i, f: (f, 0))))
    add("h", lay.raster(tail["h"], d))
    add("gate", lay.mod(tail["gate"]))
    add("ln_g", lay.const(tail["ln_g"].reshape(1, d)))
    add("ln_b", lay.const(tail["ln_b"].reshape(1, d)))
    if emit_u:
        add("scale", lay.mod(tail["scale"]))
        add("shift", lay.mod(tail["shift"]))
    out_names = ["h_out"] + (["u_out"] if emit_u else [])
    outs = [lay.out_raster(n_rows, d, F32)] + ([lay.out_raster(n_rows, d, BF16)] if emit_u else [])
    kernel = functools.partial(_ffn_kernel, names=names + out_names + ["acc"], n_f=n_f, alpha=alpha, emit_u=emit_u)
    res = pl.pallas_call(
        kernel, grid=lay.grid + (n_f,), in_specs=specs,
        out_specs=[o[1] for o in outs], out_shape=[o[0] for o in outs],
        scratch_shapes=[pltpu.VMEM((lay.rows, d), F32)],
        compiler_params=_cparams(lay.sem(("arbitrary",))), name=name)(*args)
    out = {"h": res[0]}
    if emit_u:
        out["u"] = res[1]
    return out


MOE_TOKENS = 1024
MOE_ROW_ALIGN = 32
V7X_MXU_DIM = 256


def _moe_group_rows(tb):
    return min(tb, V7X_MXU_DIM)


def _moe_kernel(cnt_ref, u_ref, comb_ref, rank_ref, rankt_ref, w1_ref, w3_ref, w2_ref, out_ref, xe_ref, ye_ref, cw_ref,
                *, n_f, tb, rc):
    blk, e, f = pl.program_id(0), pl.program_id(1), pl.program_id(2)
    n_groups = lax.div(cnt_ref[blk, e] + (rc - 1), jnp.int32(rc))
    lane = lax.broadcasted_iota(jnp.int32, (rc, V7X_LANES), 1)

    @pl.when((e == 0) & (f == 0))
    def _():
        out_ref[...] = jnp.zeros_like(out_ref)

    @pl.when(f == 0)
    def _():
        want = rankt_ref[0, pl.ds(e, 1), :]
        comb = comb_ref[...]
        comb_hi = comb.astype(BF16)
        comb_lo = (comb - comb_hi.astype(F32)).astype(BF16)
        slot = lax.broadcasted_iota(jnp.int32, (rc, tb), 0).astype(F32)

        def gather(g, carry):
            rows = pl.ds(pl.multiple_of(g * rc, MOE_ROW_ALIGN), rc)
            onehot = (want == slot + (g * rc).astype(F32)).astype(BF16)
            xe_ref[rows, :] = jnp.dot(onehot, u_ref[...], preferred_element_type=F32).astype(BF16)
            cw_ref[rows, :] = (jnp.dot(onehot, comb_hi, preferred_element_type=F32)
                               + jnp.dot(onehot, comb_lo, preferred_element_type=F32))
            ye_ref[rows, :] = jnp.zeros((rc, ye_ref.shape[1]), F32)
            return carry

        lax.fori_loop(0, n_groups, gather, 0)

    def expert(g, carry):
        rows = pl.ds(pl.multiple_of(g * rc, MOE_ROW_ALIGN), rc)
        x = xe_ref[rows, :]
        hid = _silu(jnp.dot(x, w1_ref[0], preferred_element_type=F32)) * jnp.dot(x, w3_ref[0], preferred_element_type=F32)
        cw = jnp.sum(jnp.where(lane == e, cw_ref[rows, :], 0.0), axis=-1, keepdims=True)
        ye_ref[rows, :] += jnp.dot((hid * cw).astype(BF16), w2_ref[0], preferred_element_type=F32)
        return carry

    lax.fori_loop(0, n_groups, expert, 0)

    @pl.when(f == n_f - 1)
    def _():
        lane_t = lax.broadcasted_iota(jnp.int32, (tb, V7X_LANES), 1)
        want_col = jnp.sum(jnp.where(lane_t == e, rank_ref[...], 0.0), axis=-1, keepdims=True)
        slot_t = lax.broadcasted_iota(jnp.int32, (tb, rc), 1).astype(F32)

        def scatter(g, carry):
            rows = pl.ds(pl.multiple_of(g * rc, MOE_ROW_ALIGN), rc)
            onehot_t = (want_col == slot_t + (g * rc).astype(F32)).astype(BF16)
            out_ref[...] += jnp.dot(onehot_t, ye_ref[rows, :].astype(BF16), preferred_element_type=F32)
            return carry

        lax.fori_loop(0, n_groups, scatter, 0)


def _moe_call(lay, u, w13, w2, comb, tail, *, alpha, name):
    n_e, d, f2 = w13.shape
    fdim = f2 // 2
    tf = _tile(fdim, 2048, V7X_LANES)
    n_f = fdim // tf
    n_rows = u.shape[0]
    tb = _tile(n_rows, MOE_TOKENS, MOE_ROW_ALIGN)
    rc = _moe_group_rows(tb)
    cap = rc * ((tb + rc - 1) // rc)
    n_blk = n_rows // tb
    lanes = V7X_LANES
    assert n_e == V7X_SUBLANES
    routed = (comb[:, :n_e] > 0.0).astype(jnp.int32).reshape(n_blk, tb, n_e)
    rank = jnp.where(routed > 0, jnp.cumsum(routed, axis=1) - routed, -1).astype(F32)
    cnt = jnp.sum(routed, axis=1)
    rank_col = jnp.pad(rank.reshape(n_rows, n_e), ((0, 0), (0, lanes - n_e)), constant_values=-1.0)
    rank_row = jnp.transpose(rank, (0, 2, 1))
    grid_spec = pltpu.PrefetchScalarGridSpec(
        num_scalar_prefetch=1, grid=(n_blk, n_e, n_f),
        in_specs=[pl.BlockSpec((tb, d), lambda i, e, f, c: (i, 0)),
                  pl.BlockSpec((tb, lanes), lambda i, e, f, c: (i, 0)),
                  pl.BlockSpec((tb, lanes), lambda i, e, f, c: (i, 0)),
                  pl.BlockSpec((1, n_e, tb), lambda i, e, f, c: (i, 0, 0)),
                  pl.BlockSpec((1, d, tf), lambda i, e, f, c: (e, 0, f)),
                  pl.BlockSpec((1, d, tf), lambda i, e, f, c: (e, 0, n_f + f)),
                  pl.BlockSpec((1, tf, d), lambda i, e, f, c: (e, f, 0))],
        out_specs=pl.BlockSpec((tb, d), lambda i, e, f, c: (i, 0)),
        scratch_shapes=[pltpu.VMEM((cap, d), BF16), pltpu.VMEM((cap, d), F32), pltpu.VMEM((cap, lanes), F32)])
    y = pl.pallas_call(
        functools.partial(_moe_kernel, n_f=n_f, tb=tb, rc=rc), grid_spec=grid_spec,
        out_shape=jax.ShapeDtypeStruct((n_rows, d), F32),
        compiler_params=_cparams(("parallel", "arbitrary", "arbitrary")), name=name,
    )(cnt, u, comb, rank_col, rank_row, w13, w13, w2)
    return _tail_call(lay, y, tail, alpha=alpha, name=name + "_tail")


def _tail_call(lay, y, tail, *, alpha, name):
    n_rows, d = y.shape
    emit_u = "scale" in tail
    names, args, specs = [], [], []

    def add(nm, pair):
        names.append(nm)
        args.append(pair[0])
        specs.append(pair[1])

    add("y", lay.raster(y, d))
    add("h", lay.raster(tail["h"], d))
    add("gate", lay.mod(tail["gate"]))
    add("ln_g", lay.const(tail["ln_g"].reshape(1, d)))
    add("ln_b", lay.const(tail["ln_b"].reshape(1, d)))
    if emit_u:
        add("scale", lay.mod(tail["scale"]))
        add("shift", lay.mod(tail["shift"]))
    out_names = ["h_out"] + (["u_out"] if emit_u else [])
    outs = [lay.out_raster(n_rows, d, F32)] + ([lay.out_raster(n_rows, d, BF16)] if emit_u else [])

    def kernel(*refs):
        r = dict(zip(names + out_names, refs))
        _finish_rows(r, r["y"][...], alpha=alpha, emit_u=emit_u, n_exp=0)

    res = pl.pallas_call(
        kernel, grid=lay.grid, in_specs=specs, out_specs=[o[1] for o in outs], out_shape=[o[0] for o in outs],
        compiler_params=_cparams(lay.sem()), name=name)(*args)
    out = {"h": res[0]}
    if emit_u:
        out["u"] = res[1]
    return out


def _shift_rows(cur, prev_row, next_row):
    n = cur.shape[0]
    rows = lax.broadcasted_iota(jnp.int32, cur.shape, 0)
    up = jnp.where(rows == 0, prev_row, pltpu.roll(cur, 1, 0))
    dn = jnp.where(rows == n - 1, next_row, pltpu.roll(cur, n - 1, 0))
    return up, dn


def _halo_rows(i, tps, xp_ref, xn_ref, hb):
    t = i % tps
    prev_row = jnp.where(t == 0, 0.0, xp_ref[hb - 1:hb, :].astype(F32))
    next_row = jnp.where(t == tps - 1, 0.0, xn_ref[0:1, :].astype(F32))
    return prev_row, next_row


def _halo_rows_index(tm, n_rows, hb):
    r = tm // hb
    last = n_rows // hb - 1
    return (lambda i: jnp.maximum(i * r - 1, 0)), (lambda i: jnp.minimum((i + 1) * r, last))


def _conv3_kernel(x_ref, xp_ref, xn_ref, w_ref, b_ref, o_ref, *, tps, act):
    i = pl.program_id(0)
    cur = x_ref[...].astype(F32)
    prev_row, next_row = _halo_rows(i, tps, xp_ref, xn_ref, xp_ref.shape[0])
    up, dn = _shift_rows(cur, prev_row, next_row)
    w = w_ref[...]
    y = w[0:1] * up + w[1:2] * cur + w[2:3] * dn + b_ref[...]
    if act is not None:
        y = act(y)
    o_ref[...] = y.astype(o_ref.dtype).reshape(o_ref.shape)


def _conv3(x, w, b, seq, *, act=None, out_dtype=BF16, time_major_nb=None, name="conv3"):
    n_rows, c = x.shape
    hb = V7X_BF16_ROWS if x.dtype == BF16 else V7X_SUBLANES
    tm = _tile(seq, 512, hb)
    tps = seq // tm
    if b is None:
        b = jnp.zeros((c,), F32)
    if time_major_nb is None:
        ct = _tile(c, 1024, V7X_LANES)
        out_shape = jax.ShapeDtypeStruct((n_rows, c), out_dtype)
        out_spec = pl.BlockSpec((tm, ct), lambda i, cc: (i, cc))
    else:
        nb, d = time_major_nb
        ct = d
        out_shape = jax.ShapeDtypeStruct((c // d, seq, nb * d), out_dtype)
        out_spec = pl.BlockSpec((1, tm, d), lambda i, cc: (cc, i % tps, i // tps))
    prev, nxt = _halo_rows_index(tm, n_rows, hb)
    return pl.pallas_call(
        functools.partial(_conv3_kernel, tps=tps, act=act),
        grid=(n_rows // tm, c // ct),
        in_specs=[pl.BlockSpec((tm, ct), lambda i, cc: (i, cc)),
                  pl.BlockSpec((hb, ct), lambda i, cc: (prev(i), cc)),
                  pl.BlockSpec((hb, ct), lambda i, cc: (nxt(i), cc)),
                  pl.BlockSpec((3, ct), lambda i, cc: (0, cc)),
                  pl.BlockSpec((1, ct), lambda i, cc: (0, cc))],
        out_specs=out_spec, out_shape=out_shape,
        compiler_params=_cparams(("parallel", "parallel")), name=name,
    )(x, x, x, w.astype(F32), b.reshape(1, c).astype(F32))


def _hy_filter_kernel(bands_ref, w1_ref, b1_ref, w2_ref, b2_ref, fr_ref, w3_ref, dec_ref, sum_ref, dif_ref,
                      *, seq, tl, d):
    i = pl.program_id(0)
    pos = (lax.broadcasted_iota(jnp.int32, (tl, V7X_LANES), 0) + i * tl).astype(F32)
    lane = lax.broadcasted_iota(jnp.int32, (tl, V7X_LANES), 1)
    t01 = pos / float(max(seq - 1, 1))
    ang = (2.0 * math.pi / seq) * pos * bands_ref[...]
    feats = jnp.where(lane == 0, t01, jnp.where(lane <= HY_BANDS, jnp.cos(ang), -jnp.sin(ang)))
    fr = fr_ref[...]
    h = jnp.sin(fr[0:1] * (_hdot(feats, w1_ref[...]) + b1_ref[...]))
    h = jnp.sin(fr[1:2] * (_hdot(h, w2_ref[...]) + b2_ref[...]))
    k = _hdot(h, w3_ref[...]) * jnp.exp(-t01[:, 0:1] * jnp.abs(dec_ref[...]))
    not_first = (pos[:, 0:1] > 0.0).astype(F32)
    for o in range(2):
        kf = k[:, (2 * o) * d:(2 * o + 1) * d]
        kb = k[:, (2 * o + 1) * d:(2 * o + 2) * d] * not_first
        sum_ref[:, o * d:(o + 1) * d] = (kf + kb).astype(sum_ref.dtype)
        dif_ref[:, o * d:(o + 1) * d] = (kb - kf).astype(dif_ref.dtype)


def _hy_filters(seq, p, d):
    lanes = V7X_LANES
    fw = p["hy_f_w1"].shape[1]
    bands = jnp.linspace(1e-4, HY_BANDS - 1, HY_BANDS, dtype=F32)
    bands_row = _pad_to(jnp.concatenate([jnp.zeros((1,), F32), bands, bands])[None, :], 1, lanes)
    w1 = _pad_to(_pad_to(p["hy_f_w1"].astype(F32), 0, lanes), 1, lanes)
    b1 = _pad_to(p["hy_f_b1"].astype(F32)[None, :], 1, lanes)
    w2 = _pad_to(_pad_to(p["hy_f_w2"].astype(F32), 0, lanes), 1, lanes)
    b2 = _pad_to(p["hy_f_b2"].astype(F32)[None, :], 1, lanes)
    fr = _pad_to(_pad_to(p["hy_f_freq"].astype(F32), 1, lanes), 0, V7X_SUBLANES)
    w3 = _pad_to(p["hy_f_w3"].astype(F32), 0, lanes)
    dec = p["hy_decay"].astype(F32).reshape(1, 4 * d)
    assert fw <= lanes
    tl = _tile(seq, 256, V7X_BF16_ROWS)
    full = lambda a: pl.BlockSpec(a.shape, lambda i: (0,) * a.ndim)
    ins = [bands_row, w1, b1, w2, b2, fr, w3, dec]
    return pl.pallas_call(
        functools.partial(_hy_filter_kernel, seq=seq, tl=tl, d=d),
        grid=(seq // tl,),
        in_specs=[full(a) for a in ins],
        out_specs=[pl.BlockSpec((tl, 2 * d), lambda i: (i, 0))] * 2,
        out_shape=[jax.ShapeDtypeStruct((seq, 2 * d), BF16)] * 2,
        compiler_params=_cparams(("parallel",)), name="hyena_filters",
    )(*ins)


def _dft_kernel(c_ref, s_ref, ct_ref, st_ref, *, seq, tr):
    i = pl.program_id(0)
    lanes = min(V7X_LANES, seq)
    n_hi = seq // lanes
    row = lax.broadcasted_iota(jnp.int32, (tr, lanes), 0) + i * tr
    col = lax.broadcasted_iota(jnp.int32, (tr, lanes), 1)
    scale = math.pi / (2 * seq)
    mask = 4 * seq - 1
    trig = lambda m: (jnp.cos((m & mask).astype(F32) * scale), jnp.sin((m & mask).astype(F32) * scale))
    lo_c, lo_s = trig((2 * row + 1) * col)
    hi_c, hi_s = trig((2 * row + 1) * ((col * lanes) & mask))
    tlo_c, tlo_s = trig((2 * col + 1) * row)
    thi_c, thi_s = trig(((2 * lanes * col) & mask) * row)
    for h in range(n_hi):
        cols = slice(h * lanes, (h + 1) * lanes)
        ac, as_ = hi_c[:, h:h + 1], hi_s[:, h:h + 1]
        c_ref[:, cols] = (ac * lo_c - as_ * lo_s).astype(BF16)
        s_ref[:, cols] = (as_ * lo_c + ac * lo_s).astype(BF16)
        tc, ts = thi_c[:, h:h + 1], thi_s[:, h:h + 1]
        ct_ref[:, cols] = (tc * tlo_c - ts * tlo_s).astype(BF16)
        st_ref[:, cols] = (ts * tlo_c + tc * tlo_s).astype(BF16)


def _dft_mats(seq):
    assert seq & (seq - 1) == 0, "token count must be a power of two"
    tr = _tile(seq, 256, V7X_BF16_ROWS)
    spec = pl.BlockSpec((tr, seq), lambda i: (i, 0))
    return pl.pallas_call(
        functools.partial(_dft_kernel, seq=seq, tr=tr), grid=(seq // tr,), in_specs=[],
        out_specs=[spec] * 4, out_shape=[jax.ShapeDtypeStruct((seq, seq), BF16)] * 4,
        compiler_params=_cparams(("parallel",)), name="dft_matrices")()


def _hy_fwd_kernel(c_ref, s_ref, v_ref, kr_ref, ki_ref, wr_ref, wi_ref):
    v = v_ref[0]
    cv = jnp.dot(c_ref[...], v, preferred_element_type=F32)
    sv = jnp.dot(s_ref[...], v, preferred_element_type=F32)
    kr, ki = kr_ref[...], ki_ref[...]
    wr_ref[...] = (cv * kr + sv * ki).astype(wr_ref.dtype)
    wi_ref[...] = (cv * ki - sv * kr).astype(wi_ref.dtype)


def _hy_inv_kernel(ct_ref, st_ref, wr_ref, wi_ref, v_ref, g_ref, skip_ref, o_ref, *, seq):
    y = (jnp.dot(ct_ref[...], wr_ref[...], preferred_element_type=F32)
         - jnp.dot(st_ref[...], wi_ref[...], preferred_element_type=F32)) * (1.0 / seq)
    y = y + v_ref[0].astype(F32) * skip_ref[...]
    o_ref[0] = (g_ref[0].astype(F32) * y).astype(o_ref.dtype)


def _hy_long_conv(vsrc, v_idx, gsrc, g_idx, mats, kr, ki, order, skip, d):
    c, s, ct, st = mats
    _, seq, cols = vsrc.shape
    tm = _tile(seq, 512, V7X_BF16_ROWS)
    tn = _tile(d, 512, V7X_LANES)
    cpd = d // tn
    kspec = pl.BlockSpec((tm, tn), lambda i, j: (i, order * cpd + j % cpd))
    wr, wi = pl.pallas_call(
        _hy_fwd_kernel, grid=(seq // tm, cols // tn),
        in_specs=[pl.BlockSpec((tm, seq), lambda i, j: (i, 0)), pl.BlockSpec((tm, seq), lambda i, j: (i, 0)),
                  pl.BlockSpec((1, seq, tn), lambda i, j: (v_idx, 0, j)), kspec, kspec],
        out_specs=[pl.BlockSpec((tm, tn), lambda i, j: (i, j))] * 2,
        out_shape=[jax.ShapeDtypeStruct((seq, cols), BF16)] * 2,
        compiler_params=_cparams(("parallel", "parallel")), name="hyena_dft_fwd",
    )(c, s, vsrc, kr, ki)
    return pl.pallas_call(
        functools.partial(_hy_inv_kernel, seq=seq), grid=(seq // tm, cols // tn),
        in_specs=[pl.BlockSpec((tm, seq), lambda i, j: (i, 0)), pl.BlockSpec((tm, seq), lambda i, j: (i, 0)),
                  pl.BlockSpec((seq, tn), lambda i, j: (0, j)), pl.BlockSpec((seq, tn), lambda i, j: (0, j)),
                  pl.BlockSpec((1, tm, tn), lambda i, j: (v_idx, i, j)),
                  pl.BlockSpec((1, tm, tn), lambda i, j: (g_idx, i, j)),
                  pl.BlockSpec((1, tn), lambda i, j: (0, j % cpd))],
        out_specs=pl.BlockSpec((1, tm, tn), lambda i, j: (0, i, j)),
        out_shape=jax.ShapeDtypeStruct((1, seq, cols), BF16),
        compiler_params=_cparams(("parallel", "parallel")), name="hyena_dft_inv",
    )(ct, st, wr, wi, vsrc, gsrc, skip.reshape(1, d).astype(F32))


def _hyena_mixer(u, nb, seq, p, w):
    d = u.shape[1]
    proj = _matmul(u, w["hy_w_in"], p["hy_b_in"], out_dtype=BF16, name="hyena_in_proj")
    planes = _conv3(proj, p["hy_conv_w"], p["hy_conv_b"], seq, time_major_nb=(nb, d), name="hyena_short_conv")
    ksum, kdif = _hy_filters(seq, p, d)
    mats = _dft_mats(seq)
    kr = _matmul(mats[0], ksum, out_dtype=F32, tn=512, name="hyena_filter_spec_re")
    ki = _matmul(mats[1], kdif, out_dtype=F32, tn=512, name="hyena_filter_spec_im")
    z1 = _hy_long_conv(planes, 0, planes, 1, mats, kr, ki, 0, p["hy_skip"][0], d)
    return _hy_long_conv(z1, 0, planes, 2, mats, kr, ki, 1, p["hy_skip"][1], d)


def _hyena_out(lay_rows, z, tail, p, w, nb, seq, *, alpha, name):
    d = w["hy_w_out"].shape[0]
    tm = lay_rows.tm
    tps = lay_rows.tps
    spec = pl.BlockSpec((1, tm, d), lay_rows._ix(lambda i: (0, i % tps, i // tps)))
    return _out_proj_call(lay_rows, lambda z_ref: z_ref[0], [(z, spec)], w["hy_w_out"], p["hy_b_out"], tail,
                          alpha=alpha, name=name)


def _rw_proj_kernel(*refs, names, tps, hd):
    r = dict(zip(names, refs))
    i = pl.program_id(0)
    cur = r["u"][...].astype(F32)
    prev_row, next_row = _halo_rows(i, tps, r["up"], r["un"], V7X_BF16_ROWS)
    up, dn = _shift_rows(cur, prev_row, next_row)
    xx = 0.5 * (up + dn) - cur
    mu = r["mu"][...]
    mix = lambda j: (cur + xx * mu[j:j + 1]).astype(BF16)
    xr, xw, xk, xv, xa, xg = [mix(j) for j in range(6)]
    qq = r["qq"][...]
    rr = jnp.dot(xr, r["w_rkv"][0], preferred_element_type=F32)
    kk0 = jnp.dot(xk, r["w_rkv"][1], preferred_element_type=F32)
    vv = jnp.dot(xv, r["w_rkv"][2], preferred_element_type=F32)
    gg = _bdot(jax.nn.sigmoid(jnp.dot(xg, r["g1"][...], preferred_element_type=F32)), r["g2"][...])
    kx = kk0 * r["k_k"][...]
    kkn = kx * lax.rsqrt(_head_sum(kx * kx, qq) + 1e-6)
    r["r"][...] = rr.astype(BF16)
    r["v"][...] = vv.astype(BF16)
    r["g"][...] = gg.astype(BF16)
    r["na"][...] = (-kkn).astype(BF16)
    kd_sum = jnp.zeros_like(kk0)
    for dd in range(2):
        hw = jnp.tanh(jnp.dot(xw, r["lora_w1"][dd], preferred_element_type=F32))
        wpre = r["bias_w"][dd] + _bdot(hw, r["lora_w2"][dd])
        logw = -_softplus(-wpre) - 0.5
        r["lw"][dd] = -jnp.exp(logw)
        a = jax.nn.sigmoid(
            r["bias_a"][dd] + _bdot(jnp.dot(xa, r["lora_a1"][dd], preferred_element_type=F32), r["lora_a2"][dd]))
        kd = kk0 * (1.0 + (a - 1.0) * r["k_a"][...])
        r["kd"][dd] = kd.astype(BF16)
        r["b"][dd] = (kkn * a).astype(BF16)
        kd_sum = kd_sum + kd
    r["bonus"][...] = (_head_sum(rr * kd_sum * r["r_k"][...], qq) * vv).astype(BF16)


_RW_OUTS = ["r", "v", "g", "na", "bonus", "lw", "kd", "b"]
_RW_STACKED = ("lw", "kd", "b")


def _rw_project(u, seq, p, w):
    n_rows, d = u.shape
    hb = V7X_BF16_ROWS
    tm = _tile(seq, 256, hb)
    tps = seq // tm
    lanes = V7X_LANES
    names, args, specs = [], [], []

    def add(nm, arr, spec=None):
        names.append(nm)
        args.append(arr)
        nd = arr.ndim
        specs.append(spec if spec is not None else pl.BlockSpec(arr.shape, lambda i: (0,) * nd))

    prev, nxt = _halo_rows_index(tm, n_rows, hb)
    add("u", u, pl.BlockSpec((tm, d), lambda i: (i, 0)))
    add("up", u, pl.BlockSpec((hb, d), lambda i: (prev(i), 0)))
    add("un", u, pl.BlockSpec((hb, d), lambda i: (nxt(i), 0)))
    add("mu", _pad_to(p["rw_mu"].astype(F32), 0, V7X_SUBLANES))
    add("w_rkv", w["rw_w_rkv"])
    add("g1", w["rw_g1"])
    add("g2", w["rw_g2"])
    add("bias_w", p["rw_w0"].astype(F32).reshape(2, 1, d))
    add("lora_w1", w["rw_w1"])
    add("lora_w2", w["rw_w2"])
    add("bias_a", p["rw_a0"].astype(F32).reshape(2, 1, d))
    add("lora_a1", w["rw_a1"])
    add("lora_a2", w["rw_a2"])
    add("k_k", p["rw_k_k"].astype(F32).reshape(1, d))
    add("k_a", p["rw_k_a"].astype(F32).reshape(1, d))
    add("r_k", p["rw_r_k"].astype(F32).reshape(1, d))
    add("qq", _expand_mat(V7X_LANES, d // RW_HEAD, RW_HEAD))
    out_dt = {nm: (F32 if nm == "lw" else BF16) for nm in _RW_OUTS}
    row_spec = pl.BlockSpec((tm, d), lambda i: (i, 0))
    dir_spec = pl.BlockSpec((2, tm, d), lambda i: (0, i, 0))
    outs = pl.pallas_call(
        functools.partial(_rw_proj_kernel, names=names + _RW_OUTS, tps=tps, hd=RW_HEAD),
        grid=(n_rows // tm,), in_specs=specs,
        out_specs=[dir_spec if nm in _RW_STACKED else row_spec for nm in _RW_OUTS],
        out_shape=[jax.ShapeDtypeStruct(((2,) if nm in _RW_STACKED else ()) + (n_rows, d), out_dt[nm])
                   for nm in _RW_OUTS],
        compiler_params=_cparams(("parallel",)), name="rwkv7_projections")(*args)
    return dict(zip(_RW_OUTS, outs))


TRI_BASE = 8


def _unit_tri_inverse(nmat, cl):
    ii = lax.broadcasted_iota(jnp.int32, (cl, cl), 0)
    jj = lax.broadcasted_iota(jnp.int32, (cl, cl), 1)
    zero = jnp.float32(0.0)
    ident = (ii == jj).astype(F32)
    block_gap = lambda k: lax.shift_right_logical(ii ^ jj, k)
    k0 = TRI_BASE.bit_length() - 1
    n0 = [jnp.where(block_gap(k0) == 0, n, zero) for n in nmat]
    minv = [ident + n for n in n0]
    pw = n0
    for _ in range(k0 - 1):
        pw = [_bdot(x, x) for x in pw]
        minv = [m + _bdot(m, x) for m, x in zip(minv, pw)]
    k = k0
    while (1 << k) < cl:
        off = block_gap(k) == 1
        minv = [m + _bdot(_bdot(m, jnp.where(off, n, zero)), m) for m, n in zip(minv, nmat)]
        k += 1
    return minv


def _rw_scan_kernel(r_ref, lw_ref, k_ref, v_ref, a_ref, b_ref, s0_ref, y_ref, sfin_ref, s_ref, *, n_chunks, hd):
    dirn, c = pl.program_id(1), pl.program_id(2)
    cl, d = r_ref.shape[1], r_ref.shape[2]
    pw_ = 2 * hd
    n_pairs = d // pw_

    @pl.when(c == 0)
    def _():
        s_ref[...] = s0_ref[0, 0]

    incl, strict = _time_masks(dirn, cl)
    lw = lw_ref[0, 0]
    cum = _hdot(incl.astype(F32), lw)
    p_tot = jnp.exp(jnp.sum(lw, axis=0, keepdims=True))
    p_inv = jnp.exp(-cum)
    at = a_ref[0] * jnp.exp(cum - lw)
    rt = r_ref[0] * jnp.exp(cum)
    bt = (b_ref[0, 0] * p_inv).astype(BF16)
    kt = (k_ref[0, 0] * p_inv).astype(BF16)
    v = v_ref[0]
    hd_shift = hd.bit_length() - 1
    head_of = lambda shape, axis: lax.shift_right_logical(lax.broadcasted_iota(jnp.int32, shape, axis), hd_shift)
    lane = head_of((cl, pw_), 1)
    lane2 = head_of((2 * cl, pw_), 1)
    blk = head_of((pw_, pw_), 0) == head_of((pw_, pw_), 1)
    zero = jnp.float32(0.0)
    pairs = range(n_pairs)
    halves = [(pr, hf) for pr in pairs for hf in range(2)]
    sl = lambda x, pr: x[:, pr * pw_:(pr + 1) * pw_]
    s_old = [s_ref[pr] for pr in pairs]
    s_bf = [s.astype(BF16) for s in s_old]
    xs = [jnp.concatenate([sl(at, pr), sl(rt, pr)], axis=0) for pr in pairs]
    xs0 = [_bdot_nt(xs[pr], s_bf[pr]) for pr in pairs]
    xm = [jnp.where(lane2 == hf, xs[pr], zero).astype(BF16) for pr, hf in halves]
    gb = [lax.dot_general(xm[i], sl(bt, pr), NT_DIMS, preferred_element_type=F32) for i, (pr, hf) in enumerate(halves)]
    gk = [lax.dot_general(xm[i], sl(kt, pr), NT_DIMS, preferred_element_type=F32) for i, (pr, hf) in enumerate(halves)]
    minv = _unit_tri_inverse([jnp.where(strict, g[:cl], zero) for g in gb], cl)
    vm = [jnp.where(lane == hf, sl(v, pr), zero).astype(BF16) for pr, hf in halves]
    rhs = [jnp.where(lane == hf, xs0[pr][:cl], zero)
           + jnp.dot(jnp.where(strict, gk[i][:cl], zero).astype(BF16), vm[i], preferred_element_type=F32)
           for i, (pr, hf) in enumerate(halves)]
    u = [_bdot(minv[i], rhs[i]) for i in range(len(halves))]
    u_b = [x.astype(BF16) for x in u]
    yh = [jnp.dot(jnp.where(incl, gb[i][cl:], zero).astype(BF16), u_b[i], preferred_element_type=F32)
          + jnp.dot(jnp.where(incl, gk[i][cl:], zero).astype(BF16), vm[i], preferred_element_type=F32)
          for i in range(len(halves))]
    for pr in pairs:
        y_ref[0, 0, :, pr * pw_:(pr + 1) * pw_] = (xs0[pr][cl:] + yh[2 * pr] + yh[2 * pr + 1]).astype(y_ref.dtype)
    for pr in pairs:
        uv = jnp.concatenate([u[2 * pr] + u[2 * pr + 1], sl(v, pr)], axis=0).astype(BF16)
        bk = jnp.concatenate([sl(bt, pr), sl(kt, pr)], axis=0)
        delta = lax.dot_general(uv, bk, TN_DIMS, preferred_element_type=F32)
        s_ref[pr] = (s_old[pr] + jnp.where(blk, delta, zero)) * sl(p_tot, pr)

    @pl.when(c == n_chunks - 1)
    def _():
        sfin_ref[0, 0] = s_ref[...]


def _rw_scan(q, s0, nb, seq, d):
    hd = RW_HEAD
    pw_ = 2 * hd
    assert pw_ == V7X_LANES and d % pw_ == 0
    n_pairs = d // pw_
    n_chunks = seq // CHUNK
    cidx = lambda dd, c: c + dd * (n_chunks - 1 - 2 * c)
    v3 = lambda a: a.reshape(nb, seq, d)
    v4 = lambda a: a.reshape(2, nb, seq, d)
    tok = pl.BlockSpec((1, CHUNK, d), lambda b, dd, c: (b, cidx(dd, c), 0))
    tok_d = pl.BlockSpec((1, 1, CHUNK, d), lambda b, dd, c: (dd, b, cidx(dd, c), 0))
    st_spec = pl.BlockSpec((1, 1, n_pairs, pw_, pw_), lambda b, dd, c: (b, dd, 0, 0, 0))
    y, sfin = pl.pallas_call(
        functools.partial(_rw_scan_kernel, n_chunks=n_chunks, hd=hd),
        grid=(nb, 2, n_chunks),
        in_specs=[tok, tok_d, tok_d, tok, tok, tok_d, st_spec],
        out_specs=[tok_d, st_spec],
        out_shape=[jax.ShapeDtypeStruct((2, nb, seq, d), BF16),
                   jax.ShapeDtypeStruct((nb, 2, n_pairs, pw_, pw_), F32)],
        scratch_shapes=[pltpu.VMEM((n_pairs, pw_, pw_), F32)],
        compiler_params=_cparams(("parallel", "arbitrary", "arbitrary")), name="rwkv7_scan",
    )(v3(q["r"]), v4(q["lw"]), v4(q["kd"]), v3(q["v"]), v3(q["na"]), v4(q["b"]), s0)
    return y, sfin


def _rw_out(lay, y, q, tail, p, w, *, alpha, name):
    d = w["rw_w_out"].shape[0]
    n_rows = lay.nb * lay.seq
    y2 = y.reshape(2, n_rows, d)
    qq = _expand_mat(V7X_LANES, d // RW_HEAD, RW_HEAD)
    lnx_g = p["rw_lnx_g"].astype(F32).reshape(1, d)
    lnx_b = p["rw_lnx_b"].astype(F32).reshape(1, d)
    yspec = pl.BlockSpec((2, lay.tm, d), lay._ix(lambda i: (0, i, 0)))

    def prologue(y_ref, bonus_ref, g_ref, qq_ref, lg_ref, lb_ref):
        yy = y_ref[0].astype(F32) + y_ref[1].astype(F32)
        inv = 1.0 / RW_HEAD
        mean = _head_sum(yy, qq_ref[...]) * inv
        yc = yy - mean
        var = _head_sum(yc * yc, qq_ref[...]) * inv
        yn = yc * lax.rsqrt(var + RW_GN_EPS) * lg_ref[...] + lb_ref[...]
        return (yn + bonus_ref[...].astype(F32)) * g_ref[...].astype(F32)

    pro = [(y2, yspec), lay.raster(q["bonus"], d), lay.raster(q["g"], d), lay.const(qq), lay.const(lnx_g),
           lay.const(lnx_b)]
    return _out_proj_call(lay, prologue, pro, w["rw_w_out"], None, tail, alpha=alpha, name=name)


def _ssd_scan_kernel(x_ref, b_ref, c_ref, dt_ref, dtb_ref, a_ref, s0_ref, xp_ref, eye_ref, y_ref, sfin_ref, s_ref,
                     *, n_chunks, ng, nr, hp, ns):
    dirn, c = pl.program_id(1), pl.program_id(2)
    cl = x_ref.shape[1]

    @pl.when(c == 0)
    def _():
        s_ref[...] = s0_ref[0, 0]

    incl, _ = _time_masks(dirn, cl)
    tri = incl.astype(F32)
    eye = eye_ref[...]
    xp = xp_ref[...]
    dt = _softplus(dt_ref[0] + dtb_ref[0])
    dta = dt * a_ref[0]
    cum = _hdot(tri, dta)
    tot = jnp.sum(dta, axis=0, keepdims=True)
    cum_t = _transpose_via_eye(cum, eye)
    dt_t = _transpose_via_eye(dt, eye)
    e_in = _split_dot(jnp.exp(cum), xp)
    e_end = _split_dot(jnp.exp(tot - cum) * dt, xp)
    e_tot = _split_dot(jnp.broadcast_to(jnp.exp(tot), (V7X_SUBLANES, tot.shape[1])), xp)[0:1]
    x = x_ref[0]
    xw = (x.astype(F32) * e_end).astype(BF16)
    gw = nr * hp
    pair = 2 * hp
    lane = lax.broadcasted_iota(jnp.int32, (cl, pair), 1)
    neg = jnp.float32(-jnp.inf)
    s_all = [s_ref[g] for g in range(ng)]
    y_parts, s_new = [], []
    for g in range(ng):
        bm = b_ref[0, :, g * ns:(g + 1) * ns]
        cm = c_ref[0, :, g * ns:(g + 1) * ns]
        cb = lax.dot_general(cm, bm, NT_DIMS, preferred_element_type=F32)
        s_g = s_all[g]
        y_inter = jnp.dot(cm, s_g.astype(BF16), preferred_element_type=F32) * e_in[:, g * gw:(g + 1) * gw]
        for rp in range(nr // 2):
            y_pair = jnp.zeros((cl, pair), F32)
            xpair = x[:, g * gw + rp * pair:g * gw + (rp + 1) * pair]
            for half in range(2):
                h = g * nr + rp * 2 + half
                dec = jnp.exp(jnp.where(incl, cum[:, h:h + 1] - cum_t[h:h + 1, :], neg))
                sc = (cb * dec * dt_t[h:h + 1, :]).astype(BF16)
                xh = jnp.where((lane >= half * hp) & (lane < (half + 1) * hp), xpair, jnp.zeros_like(xpair))
                y_pair = y_pair + jnp.dot(sc, xh, preferred_element_type=F32)
            y_parts.append((g * gw + rp * pair, y_pair + y_inter[:, rp * pair:(rp + 1) * pair]))
        s_new.append(s_g * e_tot[:, g * gw:(g + 1) * gw] + lax.dot_general(
            bm, xw[:, g * gw:(g + 1) * gw], TN_DIMS, preferred_element_type=F32))
    for lo, val in y_parts:
        y_ref[0, 0, :, lo:lo + pair] = val.astype(y_ref.dtype)
    for g in range(ng):
        s_ref[g] = s_new[g]

    @pl.when(c == n_chunks - 1)
    def _():
        sfin_ref[0, 0] = s_ref[...]


def _ssd_scan(xbc, dt_raw, s0, nb, seq, p):
    ng, ns, hp = SSD_GROUPS, SSD_STATE, SSD_HEAD
    inner = xbc.shape[1] - 2 * ng * ns
    nh = inner // hp
    nr = nh // ng
    lanes = V7X_LANES
    assert ns == lanes and nh <= lanes and nr % 2 == 0
    n_chunks = seq // CHUNK
    cidx = lambda dd, c: c + dd * (n_chunks - 1 - 2 * c)
    xbc3 = xbc.reshape(nb, seq, xbc.shape[1])
    dt3 = dt_raw.reshape(nb, seq, 2 * lanes)
    a = -jnp.exp(p["ssd_A_log"].astype(F32))
    a_pad = _pad_to(a, 1, lanes).reshape(2, 1, lanes)
    dtb = _pad_to(p["ssd_dt_bias"].astype(F32), 1, lanes).reshape(2, 1, lanes)
    xp = _expand_mat(lanes, nh, hp)
    eye = jnp.eye(lanes, dtype=F32)
    bw = ng * ns
    y, sfin = pl.pallas_call(
        functools.partial(_ssd_scan_kernel, n_chunks=n_chunks, ng=ng, nr=nr, hp=hp, ns=ns),
        grid=(nb, 2, n_chunks),
        in_specs=[pl.BlockSpec((1, CHUNK, inner), lambda b, dd, c: (b, cidx(dd, c), 0)),
                  pl.BlockSpec((1, CHUNK, bw), lambda b, dd, c: (b, cidx(dd, c), inner // bw)),
                  pl.BlockSpec((1, CHUNK, bw), lambda b, dd, c: (b, cidx(dd, c), inner // bw + 1)),
                  pl.BlockSpec((1, CHUNK, lanes), lambda b, dd, c: (b, cidx(dd, c), dd)),
                  pl.BlockSpec((1, 1, lanes), lambda b, dd, c: (dd, 0, 0)),
                  pl.BlockSpec((1, 1, lanes), lambda b, dd, c: (dd, 0, 0)),
                  pl.BlockSpec((1, 1, ng, ns, nr * hp), lambda b, dd, c: (b, dd, 0, 0, 0)),
                  pl.BlockSpec(xp.shape, lambda b, dd, c: (0, 0)),
                  pl.BlockSpec(eye.shape, lambda b, dd, c: (0, 0))],
        out_specs=[pl.BlockSpec((1, 1, CHUNK, inner), lambda b, dd, c: (dd, b, cidx(dd, c), 0)),
                   pl.BlockSpec((1, 1, ng, ns, nr * hp), lambda b, dd, c: (b, dd, 0, 0, 0))],
        out_shape=[jax.ShapeDtypeStruct((2, nb, seq, inner), BF16),
                   jax.ShapeDtypeStruct((nb, 2, ng, ns, nr * hp), F32)],
        scratch_shapes=[pltpu.VMEM((ng, ns, nr * hp), F32)],
        compiler_params=_cparams(("parallel", "arbitrary", "arbitrary")), name="ssd_scan",
    )(xbc3, xbc3, xbc3, dt3, dtb, a_pad, s0, xp, eye)
    return y.reshape(2, nb * seq, inner), sfin


def _ssd_project(u, nb, seq, p, w, colmajor):
    mm = (lambda x, wt, **kw: _matmul_r2c(x, wt, None, nb, **kw)) if colmajor else (
        lambda x, wt, **kw: _matmul(x, wt, None, **kw))
    z = mm(u, w["ssd_w_z"], out_dtype=BF16, name="ssd_in_proj_z")
    xbc_raw = mm(u, w["ssd_w_xbc"], out_dtype=BF16, name="ssd_in_proj_xbc")
    dt_raw = mm(u, w["ssd_w_dt"], out_dtype=F32, name="ssd_in_proj_dt")
    xbc = _conv3(xbc_raw, p["ssd_conv_w"], p["ssd_conv_b"], seq, act=_silu, name="ssd_short_conv")
    return z, xbc, dt_raw


def _ssd_out(lay, y, z, xbc, tail, p, w, *, alpha, name):
    inner = w["ssd_w_out"].shape[0]
    ng = SSD_GROUPS
    gwid = inner // ng
    d_row = jnp.repeat(p["ssd_d_skip"].astype(F32), SSD_HEAD).reshape(1, inner)
    ng_row = p["ssd_norm_g"].astype(F32).reshape(1, inner)

    def prologue(y0_ref, y1_ref, xs_ref, z_ref, d_ref, g_ref):
        flat = lambda ref: ref[...].reshape(-1, ref.shape[-1]).astype(F32)
        yy = flat(xs_ref) * d_ref[...] + flat(y0_ref) + flat(y1_ref)
        yy = yy * _silu(flat(z_ref))
        parts = []
        for g in range(ng):
            yg = yy[:, g * gwid:(g + 1) * gwid]
            ms = jnp.mean(yg * yg, axis=-1, keepdims=True)
            parts.append(yg * lax.rsqrt(ms + 1e-6))
        return jnp.concatenate(parts, axis=1) * g_ref[...]

    pro = [lay.native(y, inner, lead=0), lay.native(y, inner, lead=1), lay.native(xbc, inner),
           lay.native(z, inner), lay.const(d_row), lay.const(ng_row)]
    return _out_proj_call(lay, prologue, pro, w["ssd_w_out"], None, tail, alpha=alpha, name=name)


def _gdn_scan_kernel(q_ref, k_ref, v_ref, a_ref, bta_ref, dtb_ref, al_ref, s0_ref, eye_ref, o_ref, sfin_ref, s_ref,
                     *, n_chunks, hk, hv, dh):
    dirn, c = pl.program_id(1), pl.program_id(2)
    cl = q_ref.shape[1]

    @pl.when(c == 0)
    def _():
        s_ref[...] = s0_ref[0, 0]

    incl, strict = _time_masks(dirn, cl)
    tri = incl.astype(F32)
    eye = eye_ref[...]
    neg = jnp.float32(-jnp.inf)
    g = al_ref[0] * _softplus(a_ref[0] + dtb_ref[0])
    beta = jax.nn.sigmoid(bta_ref[0])
    gn = _hdot(tri, g)
    gtot = jnp.sum(g, axis=0, keepdims=True)
    gn_t = _transpose_via_eye(gn, eye)
    e_in = jnp.exp(gn)
    e_end = jnp.exp(gtot - gn)
    e_tot = jnp.exp(gtot)
    rep = hv // hk
    heads = range(hv)
    s_old = [s_ref[h] for h in heads]
    s_bf = [s.astype(BF16) for s in s_old]
    qn, kn, kk, qk = [], [], [], []
    for hq in range(hk):
        qh = q_ref[0, :, hq * dh:(hq + 1) * dh].astype(F32)
        kh = k_ref[0, :, hq * dh:(hq + 1) * dh].astype(F32)
        qn.append(qh * lax.rsqrt(jnp.sum(qh * qh, axis=-1, keepdims=True) + 1e-6) * (dh ** -0.5))
        kn.append(kh * lax.rsqrt(jnp.sum(kh * kh, axis=-1, keepdims=True) + 1e-6))
        gram = _bdot_nt(jnp.concatenate([kn[hq], qn[hq]], axis=0), kn[hq])
        kk.append(gram[:cl])
        qk.append(gram[cl:])
    bh = [beta[:, h:h + 1] for h in heads]
    diff = [gn[:, h:h + 1] - gn_t[h:h + 1, :] for h in heads]
    nmat = [-(kk[h // rep] * bh[h] * jnp.exp(jnp.where(strict, diff[h], neg))) for h in heads]
    attn = [(qk[h // rep] * jnp.exp(jnp.where(incl, diff[h], neg))).astype(BF16) for h in heads]
    minv = _unit_tri_inverse(nmat, cl)
    rhs =[jnp.concatenate([v_ref[0, :, h * dh:(h + 1) * dh].astype(F32) * bh[h],
                            kn[h // rep] * (bh[h] * e_in[:, h:h + 1])], axis=1) for h in heads]
    sol = [_bdot(minv[h], rhs[h]) for h in heads]
    u_b = [(sol[h][:, :dh] - _bdot(sol[h][:, dh:], s_bf[h])).astype(BF16) for h in heads]
    out = [_bdot(qn[h // rep] * e_in[:, h:h + 1], s_bf[h]) + jnp.dot(attn[h], u_b[h], preferred_element_type=F32)
           for h in heads]
    s_new = [s_old[h] * e_tot[:, h:h + 1]
             + lax.dot_general((kn[h // rep] * e_end[:, h:h + 1]).astype(BF16), u_b[h], TN_DIMS,
                               preferred_element_type=F32) for h in heads]
    for h in heads:
        o_ref[0, 0, :, h * dh:(h + 1) * dh] = out[h].astype(o_ref.dtype)
    for h in heads:
        s_ref[h] = s_new[h]

    @pl.when(c == n_chunks - 1)
    def _():
        sfin_ref[0, 0] = s_ref[...]


def _gdn_scan(qkv, ab, s0, nb, seq, p):
    dh = GDN_HEAD
    lanes = V7X_LANES
    hv = p["gdn_dt_bias"].shape[1]
    vw = hv * dh
    qk = (qkv.shape[1] - vw) // 2
    hk = qk // dh
    assert dh == lanes and hv <= lanes
    n_chunks = seq // CHUNK
    cidx = lambda dd, c: c + dd * (n_chunks - 1 - 2 * c)
    qkv3 = qkv.reshape(nb, seq, qkv.shape[1])
    ab3 = ab.reshape(nb, seq, 4 * lanes)
    dtb = _pad_to(p["gdn_dt_bias"].astype(F32), 1, lanes).reshape(2, 1, lanes)
    al = _pad_to(-jnp.exp(p["gdn_A_log"].astype(F32)), 1, lanes).reshape(2, 1, lanes)
    eye = jnp.eye(lanes, dtype=F32)
    o, sfin = pl.pallas_call(
        functools.partial(_gdn_scan_kernel, n_chunks=n_chunks, hk=hk, hv=hv, dh=dh),
        grid=(nb, 2, n_chunks),
        in_specs=[pl.BlockSpec((1, CHUNK, qk), lambda b, dd, c: (b, cidx(dd, c), 0)),
                  pl.BlockSpec((1, CHUNK, qk), lambda b, dd, c: (b, cidx(dd, c), 1)),
                  pl.BlockSpec((1, CHUNK, vw), lambda b, dd, c: (b, cidx(dd, c), 2 * qk // vw)),
                  pl.BlockSpec((1, CHUNK, lanes), lambda b, dd, c: (b, cidx(dd, c), 2 * dd)),
                  pl.BlockSpec((1, CHUNK, lanes), lambda b, dd, c: (b, cidx(dd, c), 2 * dd + 1)),
                  pl.BlockSpec((1, 1, lanes), lambda b, dd, c: (dd, 0, 0)),
                  pl.BlockSpec((1, 1, lanes), lambda b, dd, c: (dd, 0, 0)),
                  pl.BlockSpec((1, 1, hv, dh, dh), lambda b, dd, c: (b, dd, 0, 0, 0)),
                  pl.BlockSpec(eye.shape, lambda b, dd, c: (0, 0))],
        out_specs=[pl.BlockSpec((1, 1, CHUNK, vw), lambda b, dd, c: (dd, b, cidx(dd, c), 0)),
                   pl.BlockSpec((1, 1, hv, dh, dh), lambda b, dd, c: (b, dd, 0, 0, 0))],
        out_shape=[jax.ShapeDtypeStruct((2, nb, seq, vw), BF16),
                   jax.ShapeDtypeStruct((nb, 2, hv, dh, dh), F32)],
        scratch_shapes=[pltpu.VMEM((hv, dh, dh), F32)],
        compiler_params=_cparams(("parallel", "arbitrary", "arbitrary")), name="gdn_scan",
    )(qkv3, qkv3, qkv3, ab3, ab3, dtb, al, s0, eye)
    return o.reshape(2, nb * seq, vw), sfin


def _gdn_project(u, nb, seq, p, w, colmajor):
    mm = (lambda x, wt, **kw: _matmul_r2c(x, wt, None, nb, **kw)) if colmajor else (
        lambda x, wt, **kw: _matmul(x, wt, None, **kw))
    qkv_raw = mm(u, w["gdn_w_qkv"], out_dtype=BF16, name="gdn_in_proj_qkv")
    z = mm(u, w["gdn_w_z"], out_dtype=BF16, name="gdn_in_proj_z")
    ab = mm(u, w["gdn_w_ab"], out_dtype=F32, name="gdn_in_proj_ab")
    qkv = _conv3(qkv_raw, p["gdn_conv_w"], None, seq, act=_silu, name="gdn_short_conv")
    return qkv, z, ab


def _gdn_out(lay, o, z, tail, p, w, *, alpha, name):
    vw = w["gdn_w_out"].shape[0]
    dh = GDN_HEAD
    hv = vw // dh
    g_row = jnp.tile(p["gdn_norm_g"].astype(F32), hv).reshape(1, vw)

    def prologue(o0_ref, o1_ref, z_ref, g_ref):
        flat = lambda ref: ref[...].reshape(-1, ref.shape[-1]).astype(F32)
        oo = flat(o0_ref) + flat(o1_ref)
        parts = []
        for h in range(hv):
            oh = oo[:, h * dh:(h + 1) * dh]
            ms = jnp.mean(oh * oh, axis=-1, keepdims=True)
            parts.append(oh * lax.rsqrt(ms + 1e-6))
        return jnp.concatenate(parts, axis=1) * g_ref[...] * _silu(flat(z_ref))

    pro = [lay.native(o, vw, lead=0), lay.native(o, vw, lead=1), lay.native(z, vw), lay.const(g_row)]
    return _out_proj_call(lay, prologue, pro, w["gdn_w_out"], None, tail, alpha=alpha, name=name)


def _prep_weights(p):
    lanes = V7X_LANES
    bf = lambda a: a.astype(BF16)
    d = p["hy_w_out"].shape[0]
    w = {k: bf(p[k]) for k in ("hy_w_in", "hy_w_out", "rw_w_rkv", "rw_w_out", "ssd_w_out", "gdn_w_out",
                               "ffn_w13", "ffn_w2", "moe_w13", "moe_w2", "mod_w")}
    lora = lambda a, ax: bf(_pad_to(a, ax, lanes * ((a.shape[ax] + lanes - 1) // lanes)))
    w["rw_w1"], w["rw_w2"] = lora(p["rw_w1"], 2), lora(p["rw_w2"], 1)
    w["rw_a1"], w["rw_a2"] = lora(p["rw_a1"], 2), lora(p["rw_a2"], 1)
    w["rw_g1"], w["rw_g2"] = lora(p["rw_g1"], 1), lora(p["rw_g2"], 0)
    inner = p["ssd_w_out"].shape[0]
    nbc = 2 * SSD_GROUPS * SSD_STATE
    nh = inner // SSD_HEAD
    ws = p["ssd_w_in"]
    w["ssd_w_z"] = bf(ws[:, :inner])
    w["ssd_w_xbc"] = bf(ws[:, inner:2 * inner + nbc])
    wdt = ws[:, 2 * inner + nbc:].reshape(d, 2, nh)
    w["ssd_w_dt"] = bf(_pad_to(wdt, 2, lanes).reshape(d, 2 * lanes))
    vw = p["gdn_w_out"].shape[0]
    hv = vw // GDN_HEAD
    qkvw = p["gdn_conv_w"].shape[1]
    wg = p["gdn_w_in"]
    w["gdn_w_qkv"] = bf(wg[:, :qkvw])
    w["gdn_w_z"] = bf(wg[:, qkvw:qkvw + vw])
    wab = wg[:, qkvw + vw:].reshape(d, 2, 2, hv)
    wab = jnp.transpose(wab, (0, 2, 1, 3))
    w["gdn_w_ab"] = bf(_pad_to(wab, 3, lanes).reshape(d, 4 * lanes))
    w["moe_router"] = _pad_to(p["moe_router"].astype(F32), 2, lanes)
    return w


_PARAM_NAMES = (
    "mod_w mod_b ln_g ln_b hy_w_in hy_b_in hy_conv_w hy_conv_b hy_f_w1 hy_f_b1 hy_f_w2 hy_f_b2 hy_f_freq "
    "hy_f_w3 hy_decay hy_skip hy_w_out hy_b_out rw_mu rw_w_rkv rw_w0 rw_w1 rw_w2 rw_a0 rw_a1 rw_a2 rw_g1 rw_g2 "
    "rw_k_k rw_k_a rw_r_k rw_lnx_g rw_lnx_b rw_w_out ssd_w_in ssd_conv_w ssd_conv_b ssd_dt_bias ssd_A_log "
    "ssd_d_skip ssd_norm_g ssd_w_out gdn_w_in gdn_conv_w gdn_dt_bias gdn_A_log gdn_norm_g gdn_w_out ffn_w13 "
    "ffn_w2 moe_router moe_w13 moe_w2").split()


def _forward(x, c, ctx, c_ctx, p):
    nb, seq, d = x.shape
    lc = ctx.shape[1]
    depth = p["mod_w"].shape[0]
    alpha = (2 * depth) ** 0.25
    n_exp = p["moe_router"].shape[2]
    assert seq == GRID_W * GRID_W and seq % CHUNK == 0 and lc % CHUNK == 0
    w = _prep_weights(p)

    cc = _pad_to(jnp.concatenate([c, c_ctx[None, :]], axis=0).astype(F32), 0, V7X_BF16_ROWS)
    mods = [
        _matmul(cc, w["mod_w"][i], p["mod_b"][i], pre=_silu, out_dtype=F32, name="adaln_modulation")
        .reshape(cc.shape[0], 6, d) for i in range(depth)]

    def chunk(i, k, stream):
        m = mods[i][:, k]
        return m[:nb, None, :] if stream == "x" else m[nb:nb + 1, None, :]

    lay = {"x": _RowLayout(nb, seq, True), "c": _RowLayout(nb, lc, False),
           "xcol": _RowLayout(nb, seq, True, mode="cols")}
    seqs = {"x": seq, "c": lc}
    h = {"x": x.reshape(nb * seq, d).astype(F32), "c": ctx.reshape(nb * lc, d).astype(F32)}

    u = {s: _modulate(lay[s], h[s], chunk(0, 1, s), chunk(0, 0, s)) for s in ("x", "c")}

    for i in range(depth):
        last = i == depth - 1
        kind = i % 4
        moe = i % 2 == 1
        streams = ("x",) if last else ("x", "c")

        def tail1(s):
            t = dict(h=h[s], gate=chunk(i, 2, s), ln_g=p["ln_g"][i, 0], ln_b=p["ln_b"][i, 0],
                     scale=chunk(i, 4, s), shift=chunk(i, 3, s))
            if moe:
                t.update(router=w["moe_router"][i // 2], n_exp=n_exp)
            return t

        res = {}
        if kind == 0:
            for s in streams:
                z = _hyena_mixer(u[s], nb, seqs[s], p, w)
                res[s] = _hyena_out(lay[s], z, tail1(s), p, w, nb, seqs[s], alpha=alpha, name=f"hyena_out_{s}")
        elif kind == 1:
            qc = _rw_project(u["c"], lc, p, w)
            qx = _rw_project(u["x"], seq, p, w)
            s0 = jnp.zeros((nb, 2, d // (2 * RW_HEAD), 2 * RW_HEAD, 2 * RW_HEAD), F32)
            yc, s_c = _rw_scan(qc, s0, nb, lc, d)
            yx, _ = _rw_scan(qx, s_c, nb, seq, d)
            ys, qs = {"x": yx, "c": yc}, {"x": qx, "c": qc}
            for s in streams:
                res[s] = _rw_out(lay[s], ys[s], qs[s], tail1(s), p, w, alpha=alpha, name=f"rwkv7_out_{s}")
        elif kind == 2:
            zc, xbc_c, dt_c = _ssd_project(u["c"], nb, lc, p, w, False)
            zx, xbc_x, dt_x = _ssd_project(u["x"], nb, seq, p, w, True)
            inner = w["ssd_w_out"].shape[0]
            s0 = jnp.zeros((nb, 2, SSD_GROUPS, SSD_STATE, inner // SSD_GROUPS), F32)
            yc, s_c = _ssd_scan(xbc_c, dt_c, s0, nb, lc, p)
            yx, _ = _ssd_scan(xbc_x, dt_x, s_c, nb, seq, p)
            res["x"] = _ssd_out(lay["xcol"], yx, zx, xbc_x, tail1("x"), p, w, alpha=alpha, name="ssd_out_x")
            if not last:
                res["c"] = _ssd_out(lay["c"], yc, zc, xbc_c, tail1("c"), p, w, alpha=alpha, name="ssd_out_c")
        else:
            qkv_c, zc, ab_c = _gdn_project(u["c"], nb, lc, p, w, False)
            qkv_x, zx, ab_x = _gdn_project(u["x"], nb, seq, p, w, True)
            vw = w["gdn_w_out"].shape[0]
            s0 = jnp.zeros((nb, 2, vw // GDN_HEAD, GDN_HEAD, GDN_HEAD), F32)
            oc, s_c = _gdn_scan(qkv_c, ab_c, s0, nb, lc, p)
            ox, _ = _gdn_scan(qkv_x, ab_x, s_c, nb, seq, p)
            res["x"] = _gdn_out(lay["xcol"], ox, zx, tail1("x"), p, w, alpha=alpha, name="gdn_out_x")
            if not last:
                res["c"] = _gdn_out(lay["c"], oc, zc, tail1("c"), p, w, alpha=alpha, name="gdn_out_c")

        for s in streams:
            t = dict(h=res[s]["h"], gate=chunk(i, 5, s), ln_g=p["ln_g"][i, 1], ln_b=p["ln_b"][i, 1])
            if not last:
                t.update(scale=chunk(i + 1, 1, s), shift=chunk(i + 1, 0, s))
            if moe:
                out = _moe_call(lay[s], res[s]["u"], w["moe_w13"][i // 2], w["moe_w2"][i // 2], res[s]["comb"], t,
                                alpha=alpha, name=f"moe_ffn_{s}")
            else:
                out = _ffn_call(lay[s], res[s]["u"], w["ffn_w13"][i // 2], w["ffn_w2"][i // 2], t,
                                alpha=alpha, name=f"dense_ffn_{s}")
            h[s] = out["h"]
            if not last:
                u[s] = out["u"]
    return h["x"].reshape(nb, seq, d).astype(x.dtype)


def kernel(x, c, ctx, c_ctx, mod_w, mod_b, ln_g, ln_b, hy_w_in, hy_b_in, hy_conv_w, hy_conv_b, hy_f_w1, hy_f_b1, hy_f_w2, hy_f_b2, hy_f_freq, hy_f_w3, hy_decay, hy_skip, hy_w_out, hy_b_out, rw_mu, rw_w_rkv, rw_w0, rw_w1, rw_w2, rw_a0, rw_a1, rw_a2, rw_g1, rw_g2, rw_k_k, rw_k_a, rw_r_k, rw_lnx_g, rw_lnx_b, rw_w_out, ssd_w_in, ssd_conv_w, ssd_conv_b, ssd_dt_bias, ssd_A_log, ssd_d_skip, ssd_norm_g, ssd_w_out, gdn_w_in, gdn_conv_w, gdn_dt_bias, gdn_A_log, gdn_norm_g, gdn_w_out, ffn_w13, ffn_w2, moe_router, moe_w13, moe_w2):
    vals = (mod_w, mod_b, ln_g, ln_b, hy_w_in, hy_b_in, hy_conv_w, hy_conv_b, hy_f_w1, hy_f_b1, hy_f_w2, hy_f_b2,
            hy_f_freq, hy_f_w3, hy_decay, hy_skip, hy_w_out, hy_b_out, rw_mu, rw_w_rkv, rw_w0, rw_w1, rw_w2, rw_a0,
            rw_a1, rw_a2, rw_g1, rw_g2, rw_k_k, rw_k_a, rw_r_k, rw_lnx_g, rw_lnx_b, rw_w_out, ssd_w_in, ssd_conv_w,
            ssd_conv_b, ssd_dt_bias, ssd_A_log, ssd_d_skip, ssd_norm_g, ssd_w_out, gdn_w_in, gdn_conv_w,
            gdn_dt_bias, gdn_A_log, gdn_norm_g, gdn_w_out, ffn_w13, ffn_w2, moe_router, moe_w13, moe_w2)
    return _forward(x, c, ctx, c_ctx, dict(zip(_PARAM_NAMES, vals)))
```
